```python
import jax, jax.numpy as jnp
from jax import lax
import numpy as np

D_MODEL = 1024
BATCH = 1
SEQ = 16384
DEPTH = 2

HEAD_DIM = 64
MIX_WIDTH = D_MODEL
FOX_HEADS = MIX_WIDTH // (2 * HEAD_DIM)
DSA_HEADS = MIX_WIDTH // (2 * HEAD_DIM)
MOBA_HEADS = MIX_WIDTH // HEAD_DIM
FOX_W = FOX_HEADS * HEAD_DIM
DSA_W = DSA_HEADS * HEAD_DIM
MOBA_W = MOBA_HEADS * HEAD_DIM
ROT_DIM = HEAD_DIM // 4
ROPE_THETA = 500000.0
Q_BLOCK = 128
IDX_HEADS = 8
IDX_DIM = 64
DSA_TOPK = 256
MOBA_BLOCK = 256
MOBA_TOPK = 3
MOBA_Q_CHUNK = 64
MEM_TOKENS = 256
MEM_HEADS = 4
MEM_HEAD_DIM = 64
MEM_W = MEM_HEADS * MEM_HEAD_DIM
D_FF = 2816
RMS_EPS = 1e-6

EVEN_SPLITS = (FOX_W, FOX_W, FOX_W, FOX_HEADS,
               DSA_W, DSA_W, DSA_W,
               IDX_HEADS * IDX_DIM, IDX_DIM, IDX_HEADS)
EVEN_IN = sum(EVEN_SPLITS)
EVEN_OFFSETS = tuple(sum(EVEN_SPLITS[:i + 1]) for i in range(len(EVEN_SPLITS) - 1))

kernel_name = 'hybrid_fox_dsa_moba_macaron'


def rms_norm(x, g):
    xf = x.astype(jnp.float32)
    y = xf * lax.rsqrt(jnp.mean(xf * xf, axis=-1, keepdims=True) + RMS_EPS)
    return (y * g.astype(jnp.float32)).astype(x.dtype)


def rope_tables(seq):
    pos = jnp.arange(seq, dtype=jnp.float32)
    inv = ROPE_THETA ** (-jnp.arange(0, ROT_DIM, 2, dtype=jnp.float32) / ROT_DIM)
    ang = pos[:, None] * inv[None, :]
    return jnp.cos(ang), jnp.sin(ang)


def partial_rope(x, cos, sin):
    half = ROT_DIM // 2
    c = cos[None, :, None, :]
    s = sin[None, :, None, :]
    x1 = x[..., :half].astype(jnp.float32)
    x2 = x[..., half:ROT_DIM].astype(jnp.float32)
    rot = jnp.concatenate([x1 * c - x2 * s, x2 * c + x1 * s], axis=-1).astype(x.dtype)
    return jnp.concatenate([rot, x[..., ROT_DIM:]], axis=-1)


def swiglu_ffn(h, g, w_in, w_out):
    a, b = jnp.split(rms_norm(h, g) @ w_in, 2, axis=-1)
    return (jax.nn.silu(a) * b) @ w_out


def forgetting_attention(q, k, v, log_f):
    B, S, H, Dh = q.shape
    nblk = S // Q_BLOCK
    scale = Dh ** -0.5
    F = jnp.cumsum(log_f, axis=1)
    Fk = F.transpose(0, 2, 1)[:, :, None, :]
    qb = jnp.moveaxis(q.reshape(B, nblk, Q_BLOCK, H, Dh), 1, 0)
    Fq = jnp.moveaxis(F.reshape(B, nblk, Q_BLOCK, H), 1, 0)
    kpos = jnp.arange(S)

    def one_block(args):
        i, q_blk, fq_blk = args
        qpos = i * Q_BLOCK + jnp.arange(Q_BLOCK)
        s = jnp.einsum('bqhd,bkhd->bhqk', q_blk, k, preferred_element_type=jnp.float32) * scale
        s = s + fq_blk.transpose(0, 2, 1)[..., None] - Fk
        s = jnp.where(kpos[None, :] <= qpos[:, None], s, -jnp.inf)
        p = jax.nn.softmax(s, axis=-1)
        return jnp.einsum('bhqk,bkhd->bqhd', p.astype(v.dtype), v)

    out = lax.map(one_block, (jnp.arange(nblk), qb, Fq))
    return jnp.moveaxis(out, 0, 1).reshape(B, S, H, Dh)


def dsa_attention(q, k, v, iq, ik, iw):
    B, S, H, Dh = q.shape
    nblk = S // Q_BLOCK
    topk = min(DSA_TOPK, S // 4)
    scale = Dh ** -0.5
    qb = jnp.moveaxis(q.reshape(B, nblk, Q_BLOCK, H, Dh), 1, 0)
    iqb = jnp.moveaxis(iq.reshape(B, nblk, Q_BLOCK, IDX_HEADS, IDX_DIM), 1, 0)
    iwb = jnp.moveaxis(iw.reshape(B, nblk, Q_BLOCK, IDX_HEADS), 1, 0)
    kpos = jnp.arange(S)
    bidx = jnp.arange(B)[:, None, None]

    def one_block(args):
        i, q_blk, iq_blk, iw_blk = args
        qpos = i * Q_BLOCK + jnp.arange(Q_BLOCK)
        dots = jnp.einsum('bqhd,bkd->bqhk', iq_blk, ik, preferred_element_type=jnp.float32) * (IDX_DIM ** -0.5)
        w = iw_blk.astype(jnp.float32) * (IDX_HEADS ** -0.5)
        score = jnp.einsum('bqh,bqhk->bqk', w, jax.nn.relu(dots))
        score = jnp.where(kpos[None, None, :] <= qpos[None, :, None], score, -jnp.inf)
        _, sel = lax.top_k(score, topk)
        valid = sel <= qpos[None, :, None]
        ks = k[bidx, sel]
        vs = v[bidx, sel]
        s = jnp.einsum('bqhd,bqkhd->bhqk', q_blk, ks, preferred_element_type=jnp.float32) * scale
        s = jnp.where(valid[:, None], s, -jnp.inf)
        p = jax.nn.softmax(s, axis=-1)
        return jnp.einsum('bhqk,bqkhd->bqhd', p.astype(vs.dtype), vs)

    out = lax.map(one_block, (jnp.arange(nblk), qb, iqb, iwb))
    return jnp.moveaxis(out, 0, 1).reshape(B, S, H, Dh)


def moba_attention(q, k, v):
    B, S, H, Dh = q.shape
    scale = Dh ** -0.5
    nb = -(-S // MOBA_BLOCK)
    pad = ((0, 0), (0, nb * MOBA_BLOCK - S), (0, 0), (0, 0))
    kp = jnp.pad(k, pad)
    vp = jnp.pad(v, pad)
    kb = kp.reshape(B, nb, MOBA_BLOCK, H, Dh).transpose(0, 1, 3, 2, 4)
    vb = vp.reshape(B, nb, MOBA_BLOCK, H, Dh).transpose(0, 1, 3, 2, 4)
    kmean = jnp.mean(kb.astype(jnp.float32), axis=3)
    n_sel = min(MOBA_TOPK, nb - 1)
    nchunk = S // MOBA_Q_CHUNK
    qc = jnp.moveaxis(q.reshape(B, nchunk, MOBA_Q_CHUNK, H, Dh), 1, 0)
    bidx = jnp.arange(B)[:, None, None, None]
    hidx = jnp.arange(H)[None, None, :, None]
    blk_ids = jnp.arange(nb)

    def one_chunk(args):
        i, q_c = args
        start = i * MOBA_Q_CHUNK
        qpos = start + jnp.arange(MOBA_Q_CHUNK)
        own = start // MOBA_BLOCK
        k_own = lax.dynamic_slice_in_dim(kp, own * MOBA_BLOCK, MOBA_BLOCK, axis=1)
        v_own = lax.dynamic_slice_in_dim(vp, own * MOBA_BLOCK, MOBA_BLOCK, axis=1)
        own_pos = own * MOBA_BLOCK + jnp.arange(MOBA_BLOCK)
        s_own = jnp.einsum('bqhd,bkhd->bhqk', q_c, k_own, preferred_element_type=jnp.float32) * scale
        s_own = jnp.where(own_pos[None, :] <= qpos[:, None], s_own, -jnp.inf)
        if n_sel == 0:
            p = jax.nn.softmax(s_own, axis=-1)
            return jnp.einsum('bhqk,bkhd->bqhd', p.astype(v_own.dtype), v_own)
        gate = jnp.einsum('bqhd,bnhd->bqhn', q_c.astype(jnp.float32), kmean)
        gate = jnp.where(blk_ids < own, gate, -jnp.inf)
        _, sel = lax.top_k(gate, n_sel)
        valid = (sel < own).transpose(0, 2, 1, 3)[..., None]
        ks = kb[bidx, sel, hidx]
        vs = vb[bidx, sel, hidx]
        s_sel = jnp.einsum('bqhd,bqhnkd->bhqnk', q_c, ks, preferred_element_type=jnp.float32) * scale
        s_sel = jnp.where(valid, s_sel, -jnp.inf).reshape(B, H, MOBA_Q_CHUNK, n_sel * MOBA_BLOCK)
        p = jax.nn.softmax(jnp.concatenate([s_sel, s_own], axis=-1), axis=-1)
        p_sel = p[..., :n_sel * MOBA_BLOCK].reshape(B, H, MOBA_Q_CHUNK, n_sel, MOBA_BLOCK)
        p_own = p[..., n_sel * MOBA_BLOCK:]
        return (jnp.einsum('bhqnk,bqhnkd->bqhd', p_sel.astype(vs.dtype), vs)
                + jnp.einsum('bhqk,bkhd->bqhd', p_own.astype(v_own.dtype), v_own))

    out = lax.map(one_chunk, (jnp.arange(nchunk), qc))
    return jnp.moveaxis(out, 0, 1).reshape(B, S, H, Dh)


def fox_dsa_heads(hn, w_in, b_forget, fox_qn, fox_kn, dsa_qn, dsa_kn, cos, sin):
    B, S, _ = hn.shape
    fq, fk, fv, f_logit, dq, dk, dv, iq, ik, iw = jnp.split(hn @ w_in, EVEN_OFFSETS, axis=-1)
    fq = rms_norm(fq.reshape(B, S, FOX_HEADS, HEAD_DIM), fox_qn)
    fk = rms_norm(fk.reshape(B, S, FOX_HEADS, HEAD_DIM), fox_kn)
    fv = fv.reshape(B, S, FOX_HEADS, HEAD_DIM)
    log_f = jax.nn.log_sigmoid(f_logit.astype(jnp.float32) + b_forget.astype(jnp.float32))
    fox_out = forgetting_attention(fq, fk, fv, log_f)
    dq = partial_rope(rms_norm(dq.reshape(B, S, DSA_HEADS, HEAD_DIM), dsa_qn), cos, sin)
    dk = partial_rope(rms_norm(dk.reshape(B, S, DSA_HEADS, HEAD_DIM), dsa_kn), cos, sin)
    dv = dv.reshape(B, S, DSA_HEADS, HEAD_DIM)
    iq = partial_rope(iq.reshape(B, S, IDX_HEADS, IDX_DIM), cos, sin)
    ik = partial_rope(ik.reshape(B, S, 1, IDX_DIM), cos, sin)[:, :, 0]
    dsa_out = dsa_attention(dq, dk, dv, iq, ik, iw)
    return jnp.concatenate([fox_out.reshape(B, S, FOX_W), dsa_out.reshape(B, S, DSA_W)], axis=-1)


def moba_heads(hn, w_in, moba_qn, moba_kn, cos, sin):
    B, S, _ = hn.shape
    q, k, v = jnp.split(hn @ w_in, 3, axis=-1)
    q = partial_rope(rms_norm(q.reshape(B, S, MOBA_HEADS, HEAD_DIM), moba_qn), cos, sin)
    k = partial_rope(rms_norm(k.reshape(B, S, MOBA_HEADS, HEAD_DIM), moba_kn), cos, sin)
    v = v.reshape(B, S, MOBA_HEADS, HEAD_DIM)
    return moba_attention(q, k, v).reshape(B, S, MOBA_W)


def memory_cross_attention(h, mem, g_x, g_m, wq, wkv, qn, kn, wo):
    B, S, _ = h.shape
    M = mem.shape[1]
    q = rms_norm((rms_norm(h, g_x) @ wq).reshape(B, S, MEM_HEADS, MEM_HEAD_DIM), qn)
    k, v = jnp.split(rms_norm(mem, g_m) @ wkv, 2, axis=-1)
    k = rms_norm(k.reshape(B, M, MEM_HEADS, MEM_HEAD_DIM), kn)
    v = v.reshape(B, M, MEM_HEADS, MEM_HEAD_DIM)
    s = jnp.einsum('bqhd,bkhd->bhqk', q, k, preferred_element_type=jnp.float32) * (MEM_HEAD_DIM ** -0.5)
    p = jax.nn.softmax(s, axis=-1)
    o = jnp.einsum('bhqk,bkhd->bqhd', p.astype(v.dtype), v)
    return o.reshape(B, S, MEM_W) @ wo


def setup_inputs(seed: int = 0) -> dict:
    key = jax.random.key(seed)
    ks = jax.random.split(key, 32)
    n_even = (DEPTH + 1) // 2
    n_odd = DEPTH // 2

    def dense(k, shape, fan_in):
        return jax.random.normal(k, shape, jnp.float32) * (fan_in ** -0.5)

    def gain(k, shape):
        return 1.0 + 0.02 * jax.random.normal(k, shape, jnp.float32)

    return {
        'x': jax.random.normal(ks[0], (BATCH, SEQ, D_MODEL), jnp.float32),
        'mem': jax.random.normal(ks[1], (BATCH, MEM_TOKENS, D_MODEL), jnp.float32),
        'ffn1_norm': gain(ks[2], (DEPTH, D_MODEL)),
        'ffn1_w_in': dense(ks[3], (DEPTH, D_MODEL, 2 * D_FF), D_MODEL),
        'ffn1_w_out': dense(ks[4], (DEPTH, D_FF, D_MODEL), D_FF),
        'mix_norm': gain(ks[5], (DEPTH, D_MODEL)),
        'even_w_in': dense(ks[6], (n_even, D_MODEL, EVEN_IN), D_MODEL),
        'even_b_forget': jax.random.uniform(ks[7], (n_even, FOX_HEADS), jnp.float32, 1.0, 5.0),
        'fox_q_norm': gain(ks[8], (n_even, HEAD_DIM)),
        'fox_k_norm': gain(ks[9], (n_even, HEAD_DIM)),
        'dsa_q_norm': gain(ks[10], (n_even, HEAD_DIM)),
        'dsa_k_norm': gain(ks[11], (n_even, HEAD_DIM)),
        'odd_w_in': dense(ks[12], (n_odd, D_MODEL, 3 * MOBA_W), D_MODEL),
        'moba_q_norm': gain(ks[13], (n_odd, HEAD_DIM)),
        'moba_k_norm': gain(ks[14], (n_odd, HEAD_DIM)),
        'mix_w_out': dense(ks[15], (DEPTH, MIX_WIDTH, D_MODEL), MIX_WIDTH),
        'mem_norm_x': gain(ks[16], (DEPTH, D_MODEL)),
        'mem_norm_m': gain(ks[17], (DEPTH, D_MODEL)),
        'mem_wq': dense(ks[18], (DEPTH, D_MODEL, MEM_W), D_MODEL),
        'mem_wkv': dense(ks[19], (DEPTH, D_MODEL, 2 * MEM_W), D_MODEL),
        'mem_q_norm': gain(ks[20], (DEPTH, MEM_HEAD_DIM)),
        'mem_k_norm': gain(ks[21], (DEPTH, MEM_HEAD_DIM)),
        'mem_wo': dense(ks[22], (DEPTH, MEM_W, D_MODEL), MEM_W),
        'ffn2_norm': gain(ks[23], (DEPTH, D_MODEL)),
        'ffn2_w_in': dense(ks[24], (DEPTH, D_MODEL, 2 * D_FF), D_MODEL),
        'ffn2_w_out': dense(ks[25], (DEPTH, D_FF, D_MODEL), D_FF),
    }


def reference(x, mem, ffn1_norm, ffn1_w_in, ffn1_w_out, mix_norm, even_w_in, even_b_forget,
              fox_q_norm, fox_k_norm, dsa_q_norm, dsa_k_norm, odd_w_in, moba_q_norm, moba_k_norm,
              mix_w_out, mem_norm_x, mem_norm_m, mem_wq, mem_wkv, mem_q_norm, mem_k_norm, mem_wo,
              ffn2_norm, ffn2_w_in, ffn2_w_out):
    cos, sin = rope_tables(x.shape[1])
    h = x
    for layer in range(DEPTH):
        h = h + 0.5 * swiglu_ffn(h, ffn1_norm[layer], ffn1_w_in[layer], ffn1_w_out[layer])
        hn = rms_norm(h, mix_norm[layer])
        if layer % 2 == 0:
            e = layer // 2
            mixed = fox_dsa_heads(hn, even_w_in[e], even_b_forget[e], fox_q_norm[e], fox_k_norm[e],
                                  dsa_q_norm[e], dsa_k_norm[e], cos, sin)
        else:
            o = layer // 2
            mixed = moba_heads(hn, odd_w_in[o], moba_q_norm[o], moba_k_norm[o], cos, sin)
        h = h + mixed @ mix_w_out[layer]
        h = h + memory_cross_attention(h, mem, mem_norm_x[layer], mem_norm_m[layer], mem_wq[layer],
                                       mem_wkv[layer], mem_q_norm[layer], mem_k_norm[layer], mem_wo[layer])
        h = h + 0.5 * swiglu_ffn(h, ffn2_norm[layer], ffn2_w_in[layer], ffn2_w_out[layer])
    return h
```

```python
import functools

import numpy as np
import jax
import jax.numpy as jnp
from jax import lax
from jax.experimental import pallas as pl
from jax.experimental.pallas import tpu as pltpu

HEAD_DIM = 64
ROT_DIM = HEAD_DIM // 4
ROPE_THETA = 500000.0
FOX_HEADS = 8
DSA_HEADS = 8
MOBA_HEADS = 16
IDX_HEADS = 8
IDX_DIM = 64
DSA_TOPK = 256
MOBA_BLOCK = 256
MOBA_TOPK = 3
MEM_HEADS = 4
RMS_EPS = 1e-6

NEG = -1e30
INT_MIN = -(2 ** 31)
LANE = 128
VMEM_LIMIT = 56 * 1024 * 1024

F32 = jnp.float32
BF16 = jnp.bfloat16


def _cparams(n_axes=1):
    return pltpu.CompilerParams(dimension_semantics=("arbitrary",) * n_axes, vmem_limit_bytes=VMEM_LIMIT)


def _resident(shape):
    nd = len(shape)
    return pl.BlockSpec(shape, lambda *_: (0,) * nd, pipeline_mode=pl.Buffered(1))


def _rms_bf16(x, g):
    ms = jnp.mean(x * x, axis=-1, keepdims=True)
    return (x * lax.rsqrt(ms + RMS_EPS) * g).astype(BF16)


def _ffn_kernel(x_ref, g_ref, wa_ref, wb_ref, wo_ref, o_ref, *, fc):
    x = x_ref[...]
    xn = _rms_bf16(x, g_ref[...])
    acc = jnp.zeros(x.shape, F32)
    for c in range(wa_ref.shape[1] // fc):
        a = jnp.dot(xn, wa_ref[:, c * fc:(c + 1) * fc], preferred_element_type=F32)
        b = jnp.dot(xn, wb_ref[:, c * fc:(c + 1) * fc], preferred_element_type=F32)
        gate = (a * jax.nn.sigmoid(a) * b).astype(BF16)
        acc = acc + jnp.dot(gate, wo_ref[c * fc:(c + 1) * fc, :], preferred_element_type=F32)
    o_ref[...] = x + 0.5 * acc


def _ffn(h, g, w_in, w_out, tm):
    S, D = h.shape
    Fh = w_out.shape[0]
    return pl.pallas_call(
        functools.partial(_ffn_kernel, fc=256),
        grid=(S // tm,),
        in_specs=[pl.BlockSpec((tm, D), lambda i: (i, 0)),
                  _resident((1, D)),
                  pl.BlockSpec((D, Fh), lambda i: (0, 0), pipeline_mode=pl.Buffered(1)),
                  pl.BlockSpec((D, Fh), lambda i: (0, 1), pipeline_mode=pl.Buffered(1)),
                  _resident((Fh, D))],
        out_specs=pl.BlockSpec((tm, D), lambda i: (i, 0)),
        out_shape=jax.ShapeDtypeStruct((S, D), F32),
        compiler_params=_cparams(),
        name="ffn",
    )(h, g.reshape(1, D), w_in, w_in, w_out)


def _proj_kernel(x_ref, g_ref, w_ref, o_ref):
    o_ref[...] = jnp.dot(_rms_bf16(x_ref[...], g_ref[...]), w_ref[...], preferred_element_type=F32)


def _proj(h, g, w, tm):
    S, D = h.shape
    N = w.shape[1]
    return pl.pallas_call(
        _proj_kernel,
        grid=(S // tm,),
        in_specs=[pl.BlockSpec((tm, D), lambda i: (i, 0)), _resident((1, D)), _resident((D, N))],
        out_specs=pl.BlockSpec((tm, N), lambda i: (i, 0)),
        out_shape=jax.ShapeDtypeStruct((S, N), F32),
        compiler_params=_cparams(),
        name="proj",
    )(h, g.reshape(1, D), w)


def _outproj_kernel(*refs, n):
    r_ref, o_ref = refs[0], refs[-1]
    acc = r_ref[...]
    for k in range(n):
        acc = acc + jnp.dot(refs[1 + k][...].astype(BF16), refs[1 + n + k][...], preferred_element_type=F32)
    o_ref[...] = acc


def _outproj(res, xs, ws, tm):
    S, D = res.shape
    n = len(xs)
    in_specs = [pl.BlockSpec((tm, D), lambda i: (i, 0))]
    in_specs += [pl.BlockSpec((tm, x.shape[1]), lambda i: (i, 0)) for x in xs]
    in_specs += [_resident(w.shape) for w in ws]
    return pl.pallas_call(
        functools.partial(_outproj_kernel, n=n),
        grid=(S // tm,),
        in_specs=in_specs,
        out_specs=pl.BlockSpec((tm, D), lambda i: (i, 0)),
        out_shape=jax.ShapeDtypeStruct((S, D), F32),
        compiler_params=_cparams(),
        name="outproj",
    )(res, *xs, *ws)


def _softmax_step(s, h, vT_ref, m_ref, l_ref, acc_ref):
    m_prev = m_ref[h]
    m_new = jnp.maximum(m_prev, jnp.max(s, axis=0, keepdims=True))
    alpha = jnp.exp(m_prev - m_new)
    p = jnp.exp(s - m_new)
    l_ref[h] = alpha * l_ref[h] + jnp.sum(p, axis=0, keepdims=True)
    acc_ref[h] = alpha * acc_ref[h] + jnp.dot(vT_ref[h], p.astype(BF16), preferred_element_type=F32)
    m_ref[h] = m_new


def _attn_init(m_ref, l_ref, acc_ref):
    m_ref[...] = jnp.full(m_ref.shape, NEG, F32)
    l_ref[...] = jnp.zeros(l_ref.shape, F32)
    acc_ref[...] = jnp.zeros(acc_ref.shape, F32)


def _attn_finish(o_ref, l_ref, acc_ref, nh, hd):
    for h in range(0, nh, 2):
        pair = jnp.concatenate([acc_ref[h] / l_ref[h], acc_ref[h + 1] / l_ref[h + 1]], axis=0)
        o_ref[:, h * hd:(h + 2) * hd] = pair.T


def _attn_kernel(qi_ref, kj_ref, *refs, mode, nh, tq, tk, hd):
    if mode == "fox":
        qT_ref, k_ref, vT_ref, fq_ref, fk_ref, o_ref, m_ref, l_ref, acc_ref = refs
    elif mode == "moba":
        qT_ref, k_ref, vT_ref, sb_ref, o_ref, m_ref, l_ref, acc_ref = refs
    else:
        qT_ref, k_ref, vT_ref, o_ref, m_ref, l_ref, acc_ref = refs
    p = pl.program_id(0)
    i = qi_ref[p]
    j = kj_ref[p]
    causal = mode != "none"

    @pl.when(j == 0)
    def _():
        _attn_init(m_ref, l_ref, acc_ref)

    def process(diag):
        if diag:
            row = lax.broadcasted_iota(jnp.int32, (tk, tq), 0)
            col = lax.broadcasted_iota(jnp.int32, (tk, tq), 1)
            cbias = jnp.where(row <= col, 0.0, NEG)

        def body(h, carry):
            s = jnp.dot(k_ref[h], qT_ref[h], preferred_element_type=F32)
            if mode == "fox":
                s = s + fq_ref[h] - jnp.concatenate([fk_ref[h]] * (tq // LANE), axis=1)
            elif mode == "moba":
                nsub = tk // MOBA_BLOCK
                s = jnp.concatenate(
                    [s[u * MOBA_BLOCK:(u + 1) * MOBA_BLOCK] + sb_ref[h, pl.ds(j * nsub + u, 1), :]
                     for u in range(nsub)], axis=0)
            if diag:
                s = s + cbias
            _softmax_step(s, h, vT_ref, m_ref, l_ref, acc_ref)
            return carry

        lax.fori_loop(0, nh, body, 0)

    if causal:
        pl.when(j == i)(lambda: process(True))
        pl.when(j != i)(lambda: process(False))
        last = j == i
    else:
        process(False)
        last = j == 0

    @pl.when(last)
    def _():
        _attn_finish(o_ref, l_ref, acc_ref, nh, hd)


def _attention(mode, qT, k, vT, extra, tq, tk):
    nh, hd, S = qT.shape
    nq = S // tq
    if mode == "none":
        pairs = [(i, 0) for i in range(nq)]
    else:
        assert tq == tk
        pairs = [(i, j) for i in range(nq) for j in range(i + 1)]
    qi = jnp.asarray(np.array([a for a, _ in pairs], np.int32))
    kj = jnp.asarray(np.array([b for _, b in pairs], np.int32))
    in_specs = [pl.BlockSpec((nh, hd, tq), lambda p, qi, kj: (0, 0, qi[p])),
                pl.BlockSpec((nh, tk, hd), lambda p, qi, kj: (0, kj[p], 0)),
                pl.BlockSpec((nh, hd, tk), lambda p, qi, kj: (0, 0, kj[p]))]
    if mode == "fox":
        in_specs += [pl.BlockSpec((nh, 1, tq), lambda p, qi, kj: (0, 0, qi[p])),
                     pl.BlockSpec((nh, tk, LANE), lambda p, qi, kj: (0, kj[p], 0))]
    elif mode == "moba":
        nb = extra[0].shape[1]
        in_specs += [pl.BlockSpec((nh, nb, tq), lambda p, qi, kj: (0, 0, qi[p]))]
    return pl.pallas_call(
        functools.partial(_attn_kernel, mode=mode, nh=nh, tq=tq, tk=tk, hd=hd),
        grid_spec=pltpu.PrefetchScalarGridSpec(
            num_scalar_prefetch=2,
            grid=(len(pairs),),
            in_specs=in_specs,
            out_specs=pl.BlockSpec((tq, nh * hd), lambda p, qi, kj: (qi[p], 0)),
            scratch_shapes=[pltpu.VMEM((nh, 1, tq), F32), pltpu.VMEM((nh, 1, tq), F32),
                            pltpu.VMEM((nh, hd, tq), F32)]),
        out_shape=jax.ShapeDtypeStruct((S, nh * hd), F32),
        compiler_params=_cparams(),
        name="attn_" + mode,
    )(qi, kj, qT, k, vT, *extra)


def _gate_kernel(qT_ref, km_ref, o_ref, *, nh, nb, tq):
    i = pl.program_id(0)
    n = lax.broadcasted_iota(jnp.int32, (nb, tq), 0)
    own = (i * tq + lax.broadcasted_iota(jnp.int32, (nb, tq), 1)) // MOBA_BLOCK
    ninf = jnp.float32(-jnp.inf)

    def body(h, carry):
        g = jnp.dot(km_ref[h], qT_ref[h], preferred_element_type=F32)
        g = jnp.where(n < own, g, ninf)
        keep = n == own
        for _ in range(MOBA_TOPK):
            mx = jnp.max(g, axis=0, keepdims=True)
            first = jnp.min(jnp.where((g == mx) & (mx > ninf), n, nb), axis=0, keepdims=True)
            pick = n == first
            keep = keep | pick
            g = jnp.where(pick, ninf, g)
        o_ref[h] = jnp.where(keep, 0.0, NEG)
        return carry

    lax.fori_loop(0, nh, body, 0)


def _moba_gate(qT, kmean, tq):
    nh, hd, S = qT.shape
    nb = kmean.shape[1]
    return pl.pallas_call(
        functools.partial(_gate_kernel, nh=nh, nb=nb, tq=tq),
        grid=(S // tq,),
        in_specs=[pl.BlockSpec((nh, hd, tq), lambda i: (0, 0, i)), _resident((nh, nb, hd))],
        out_specs=pl.BlockSpec((nh, nb, tq), lambda i: (0, 0, i)),
        out_shape=jax.ShapeDtypeStruct((nh, nb, S), F32),
        compiler_params=_cparams(),
        name="moba_gate",
    )(qT, kmean)


def _dsa_kernel(qi_ref, kj_ref, ph_ref, ikb_ref, kvb_ref, nk_ref,
                iqT_ref, ik_ref, w_ref, qT_ref, k_ref, vT_ref, o_ref,
                keys_ref, thr_ref, cut_ref, m_ref, l_ref, acc_ref,
                *, nih, nh, tq, tk, hd, topk, seq, chunk):
    p = pl.program_id(0)
    i = qi_ref[p]
    j = kj_ref[p]
    phase = ph_ref[p]
    nkb = nk_ref[p]

    @pl.when(phase == 0)
    def _scores():
        ik = ik_ref[...]
        sc = jnp.zeros((tk, tq), F32)
        for h in range(nih):
            d = jnp.dot(ik, iqT_ref[h], preferred_element_type=F32)
            sc = sc + w_ref[h:h + 1, :] * jnp.maximum(d, 0.0)
        b = lax.bitcast_convert_type(sc, jnp.int32)
        key = b ^ ((b >> 31) & 0x7FFFFFFF)
        row = j * tk + lax.broadcasted_iota(jnp.int32, (tk, tq), 0)
        col = i * tq + lax.broadcasted_iota(jnp.int32, (tk, tq), 1)
        key = jnp.where(row <= col, key, INT_MIN)
        keys_ref[pl.ds(pl.multiple_of(j * tk, tk), tk), :] = key

    def count(hit):
        def body(c, acc):
            r0 = pl.multiple_of(c * chunk, chunk)
            return acc + jnp.sum(hit(keys_ref[pl.ds(r0, chunk), :], r0).reshape(chunk // 8, 8, tq), axis=0)
        acc = lax.fori_loop(0, nkb * (tk // chunk), body, jnp.zeros((8, tq), jnp.int32))
        return jnp.sum(acc, axis=0, keepdims=True)

    @pl.when((phase == 1) & (j == 0))
    def _select():
        c0 = count(lambda blk, r0: jnp.where(blk >= 0, 1, 0))
        thr = jnp.where(c0 >= topk, 0, INT_MIN)

        def bit_body(it, thr):
            cand = thr | jnp.left_shift(jnp.int32(1), 30 - it)
            cnt = count(lambda blk, r0: jnp.where(blk >= cand, 1, 0))
            return jnp.where(cnt >= topk, cand, thr)

        thr = lax.fori_loop(0, 31, bit_body, thr)
        thr = jnp.maximum(thr, INT_MIN + 1)
        thr_ref[...] = thr
        cut_ref[...] = jnp.full((1, tq), seq, jnp.int32)
        cge = count(lambda blk, r0: jnp.where(blk >= thr, 1, 0))

        @pl.when(jnp.max(cge) > topk)
        def _ties():
            need = topk - count(lambda blk, r0: jnp.where(blk > thr, 1, 0))
            nbits = max(1, int(seq - 1).bit_length())

            def pos_body(it, x):
                cand = x + jnp.left_shift(jnp.int32(1), nbits - 1 - it)
                pos = lambda r0: r0 + lax.broadcasted_iota(jnp.int32, (chunk, tq), 0)
                c = count(lambda blk, r0: jnp.where(blk == thr, jnp.where(pos(r0) < cand, 1, 0), 0))
                return jnp.where(c < need, cand, x)

            x = lax.fori_loop(0, nbits, pos_body, jnp.zeros((1, tq), jnp.int32))
            cut_ref[...] = jnp.where(cge > topk, x, seq)

        _attn_init(m_ref, l_ref, acc_ref)

    @pl.when(phase == 1)
    def _attend():
        key = keys_ref[pl.ds(pl.multiple_of(j * tk, tk), tk), :]
        thr = thr_ref[...]
        pos = j * tk + lax.broadcasted_iota(jnp.int32, (tk, tq), 0)
        tie = jnp.where(key == thr, jnp.where(pos <= cut_ref[...], 0.0, NEG), NEG)
        bias = jnp.where(key > thr, 0.0, tie)

        def body(h, carry):
            s = jnp.dot(k_ref[h], qT_ref[h], preferred_element_type=F32) + bias
            _softmax_step(s, h, vT_ref, m_ref, l_ref, acc_ref)
            return carry

        lax.fori_loop(0, nh, body, 0)

        @pl.when(j == nkb - 1)
        def _():
            _attn_finish(o_ref, l_ref, acc_ref, nh, hd)


def _dsa_attention(iqT, ik, wT, qT, k, vT, tq, tk, topk):
    nh, hd, S = qT.shape
    nih = iqT.shape[0]
    steps = []
    for i in range(S // tq):
        nkb = (i * tq + tq - 1) // tk + 1
        steps += [(i, j, 0, j, 0, nkb) for j in range(nkb)]
        steps += [(i, j, 1, nkb - 1, j, nkb) for j in range(nkb)]
    tab = [jnp.asarray(np.array([s[c] for s in steps], np.int32)) for c in range(6)]
    chunk = min(tk, 512)
    qmap = lambda p, qi, kj, ph, ikb, kvb, nk: (0, 0, qi[p])
    return pl.pallas_call(
        functools.partial(_dsa_kernel, nih=nih, nh=nh, tq=tq, tk=tk, hd=hd, topk=topk, seq=S, chunk=chunk),
        grid_spec=pltpu.PrefetchScalarGridSpec(
            num_scalar_prefetch=6,
            grid=(len(steps),),
            in_specs=[pl.BlockSpec((nih, IDX_DIM, tq), qmap),
                      pl.BlockSpec((tk, IDX_DIM), lambda p, qi, kj, ph, ikb, kvb, nk: (ikb[p], 0)),
                      pl.BlockSpec((nih, tq), lambda p, qi, kj, ph, ikb, kvb, nk: (0, qi[p])),
                      pl.BlockSpec((nh, hd, tq), qmap),
                      pl.BlockSpec((nh, tk, hd), lambda p, qi, kj, ph, ikb, kvb, nk: (0, kvb[p], 0)),
                      pl.BlockSpec((nh, hd, tk), lambda p, qi, kj, ph, ikb, kvb, nk: (0, 0, kvb[p]))],
            out_specs=pl.BlockSpec((tq, nh * hd), lambda p, qi, kj, ph, ikb, kvb, nk: (qi[p], 0)),
            scratch_shapes=[pltpu.VMEM((S, tq), jnp.int32),
                            pltpu.VMEM((1, tq), jnp.int32), pltpu.VMEM((1, tq), jnp.int32),
                            pltpu.VMEM((nh, 1, tq), F32), pltpu.VMEM((nh, 1, tq), F32),
                            pltpu.VMEM((nh, hd, tq), F32)]),
        out_shape=jax.ShapeDtypeStruct((S, nh * hd), F32),
        compiler_params=_cparams(),
        name="dsa",
    )(*tab, iqT, ik, wT, qT, k, vT)


def _head_norm(x, g):
    return x * lax.rsqrt(jnp.mean(x * x, axis=-1, keepdims=True) + RMS_EPS) * g


def _rope(x, cos, sin):
    half = ROT_DIM // 2
    c, s = cos[:, None, :], sin[:, None, :]
    x1, x2 = x[..., :half], x[..., half:ROT_DIM]
    return jnp.concatenate([x1 * c - x2 * s, x2 * c + x1 * s, x[..., ROT_DIM:]], axis=-1)


def _qT(x, scale=1.0):
    return (jnp.transpose(x, (1, 2, 0)) * scale).astype(BF16)


def _kH(x):
    return jnp.transpose(x, (1, 0, 2)).astype(BF16)


def _pad_cols(w, mult=LANE):
    n = w.shape[1]
    return jnp.pad(w, ((0, 0), (0, (-n) % mult)))


def kernel(x, mem, ffn1_norm, ffn1_w_in, ffn1_w_out, mix_norm, even_w_in, even_b_forget, fox_q_norm, fox_k_norm, dsa_q_norm, dsa_k_norm, odd_w_in, moba_q_norm, moba_k_norm, mix_w_out, mem_norm_x, mem_norm_m, mem_wq, mem_wkv, mem_q_norm, mem_k_norm, mem_wo, ffn2_norm, ffn2_w_in, ffn2_w_out):
    _, S, D = x.shape
    depth = ffn1_norm.shape[0]
    tm = min(512, S)
    ta = min(512, S)
    tdq = min(256, S)
    scale = HEAD_DIM ** -0.5
    pos = jnp.arange(S, dtype=F32)
    inv = ROPE_THETA ** (-jnp.arange(0, ROT_DIM, 2, dtype=F32) / ROT_DIM)
    ang = pos[:, None] * inv[None, :]
    cos, sin = jnp.cos(ang), jnp.sin(ang)

    h = x[0]
    m0 = mem[0]
    for layer in range(depth):
        h = _ffn(h, ffn1_norm[layer], ffn1_w_in[layer].astype(BF16), ffn1_w_out[layer].astype(BF16), tm)
        if layer % 2 == 0:
            e = layer // 2
            fw, dw = FOX_HEADS * HEAD_DIM, DSA_HEADS * HEAD_DIM
            y = _proj(h, mix_norm[layer], _pad_cols(even_w_in[e]).astype(BF16), tm)
            offs = np.cumsum([0, fw, fw, fw, FOX_HEADS, dw, dw, dw, IDX_HEADS * IDX_DIM, IDX_DIM, IDX_HEADS])
            fq, fk, fv, flog, dq, dk, dv, iq, ik, iw = [y[:, a:b] for a, b in zip(offs[:-1], offs[1:])]
            hv = lambda t, nh: t.reshape(S, nh, HEAD_DIM)
            fq = _head_norm(hv(fq, FOX_HEADS), fox_q_norm[e])
            fk = _head_norm(hv(fk, FOX_HEADS), fox_k_norm[e])
            Fc = jnp.cumsum(jax.nn.log_sigmoid(flog + even_b_forget[e]), axis=0)
            FcT = Fc.T
            fox = _attention("fox", _qT(fq, scale), _kH(fk), _qT(hv(fv, FOX_HEADS)),
                             (FcT[:, None, :], jnp.broadcast_to(FcT[:, :, None], (FOX_HEADS, S, LANE))), ta, ta)
            dq = _rope(_head_norm(hv(dq, DSA_HEADS), dsa_q_norm[e]), cos, sin)
            dk = _rope(_head_norm(hv(dk, DSA_HEADS), dsa_k_norm[e]), cos, sin)
            iq = _rope(hv(iq, IDX_HEADS), cos, sin)
            ik = _rope(ik.reshape(S, 1, IDX_DIM), cos, sin)[:, 0]
            wT = (iw * (IDX_HEADS ** -0.5 * IDX_DIM ** -0.5)).T
            dsa = _dsa_attention(_qT(iq), ik.astype(BF16), wT, _qT(dq, scale), _kH(dk), _qT(hv(dv, DSA_HEADS)),
                                 tdq, ta, min(DSA_TOPK, S // 4))
            wo = mix_w_out[layer].astype(BF16)
            h = _outproj(h, [fox, dsa], [wo[:fw], wo[fw:]], tm)
        else:
            o = layer // 2
            mw = MOBA_HEADS * HEAD_DIM
            y = _proj(h, mix_norm[layer], odd_w_in[o].astype(BF16), tm)
            hv = lambda t: t.reshape(S, MOBA_HEADS, HEAD_DIM)
            q = _rope(_head_norm(hv(y[:, :mw]), moba_q_norm[o]), cos, sin)
            k = _rope(_head_norm(hv(y[:, mw:2 * mw]), moba_k_norm[o]), cos, sin)
            v = hv(y[:, 2 * mw:])
            kmean = jnp.mean(k.reshape(S // MOBA_BLOCK, MOBA_BLOCK, MOBA_HEADS, HEAD_DIM), axis=1)
            qT = _qT(q, scale)
            selb = _moba_gate(qT, jnp.transpose(kmean, (1, 0, 2)).astype(BF16), ta)
            moba = _attention("moba", qT, _kH(k), _qT(v), (selb,), ta, ta)
            h = _outproj(h, [moba], [mix_w_out[layer].astype(BF16)], tm)
        M = m0.shape[0]
        q = _proj(h, mem_norm_x[layer], mem_wq[layer].astype(BF16), tm)
        q = _head_norm(q.reshape(S, MEM_HEADS, HEAD_DIM), mem_q_norm[layer])
        kv = _proj(m0, mem_norm_m[layer], mem_wkv[layer].astype(BF16), M)
        mwid = MEM_HEADS * HEAD_DIM
        mk = _head_norm(kv[:, :mwid].reshape(M, MEM_HEADS, HEAD_DIM), mem_k_norm[layer])
        mv = kv[:, mwid:].reshape(M, MEM_HEADS, HEAD_DIM)
        ma = _attention("none", _qT(q, scale), _kH(mk), _qT(mv), (), ta, M)
        h = _outproj(h, [ma], [mem_wo[layer].astype(BF16)], tm)
        h = _ffn(h, ffn2_norm[layer], ffn2_w_in[layer].astype(BF16), ffn2_w_out[layer].astype(BF16), tm)
    return h[None]
```

```python
import functools

import numpy as np
import jax
import jax.numpy as jnp
from jax import lax
from jax.experimental import pallas as pl
from jax.experimental.pallas import tpu as pltpu

HEAD_DIM = 64
ROT_DIM = HEAD_DIM // 4
ROPE_THETA = 500000.0
FOX_HEADS = 8
DSA_HEADS = 8
MOBA_HEADS = 16
IDX_HEADS = 8
IDX_DIM = 64
DSA_TOPK = 256
MOBA_BLOCK = 256
MOBA_TOPK = 3
MEM_HEADS = 4
RMS_EPS = 1e-6

NEG = -1e30
INT_MIN = -(2 ** 31)
LANE = 128
ATTN_ROWS = 128
VMEM_LIMIT = 56 * 1024 * 1024

F32 = jnp.float32
BF16 = jnp.bfloat16


def _cparams(n_axes=1):
    return pltpu.CompilerParams(dimension_semantics=("arbitrary",) * n_axes, vmem_limit_bytes=VMEM_LIMIT)


def _resident(shape):
    nd = len(shape)
    return pl.BlockSpec(shape, lambda *_: (0,) * nd, pipeline_mode=pl.Buffered(1))


def _rms_bf16(x, g):
    ms = jnp.mean(x * x, axis=-1, keepdims=True)
    return (x * lax.rsqrt(ms + RMS_EPS) * g).astype(BF16)


def _ffn_kernel(x_ref, g_ref, wa_ref, wb_ref, wo_ref, o_ref, *, fc):
    x = x_ref[...]
    xn = _rms_bf16(x, g_ref[...])
    acc = jnp.zeros(x.shape, F32)
    for c in range(wa_ref.shape[1] // fc):
        a = jnp.dot(xn, wa_ref[:, c * fc:(c + 1) * fc], preferred_element_type=F32)
        b = jnp.dot(xn, wb_ref[:, c * fc:(c + 1) * fc], preferred_element_type=F32)
        gate = (a * jax.nn.sigmoid(a) * b).astype(BF16)
        acc = acc + jnp.dot(gate, wo_ref[c * fc:(c + 1) * fc, :], preferred_element_type=F32)
    o_ref[...] = x + 0.5 * acc


def _ffn(h, g, w_in, w_out, tm):
    S, D = h.shape
    Fh = w_out.shape[0]
    return pl.pallas_call(
        functools.partial(_ffn_kernel, fc=256),
        grid=(S // tm,),
        in_specs=[pl.BlockSpec((tm, D), lambda i: (i, 0)),
                  _resident((1, D)),
                  pl.BlockSpec((D, Fh), lambda i: (0, 0), pipeline_mode=pl.Buffered(1)),
                  pl.BlockSpec((D, Fh), lambda i: (0, 1), pipeline_mode=pl.Buffered(1)),
                  _resident((Fh, D))],
        out_specs=pl.BlockSpec((tm, D), lambda i: (i, 0)),
        out_shape=jax.ShapeDtypeStruct((S, D), F32),
        compiler_params=_cparams(),
        name="ffn",
    )(h, g.reshape(1, D), w_in, w_in, w_out)


def _proj_kernel(x_ref, g_ref, w_ref, o_ref):
    o_ref[...] = jnp.dot(_rms_bf16(x_ref[...], g_ref[...]), w_ref[...], preferred_element_type=F32)


def _proj(h, g, w, tm):
    S, D = h.shape
    N = w.shape[1]
    return pl.pallas_call(
        _proj_kernel,
        grid=(S // tm,),
        in_specs=[pl.BlockSpec((tm, D), lambda i: (i, 0)), _resident((1, D)), _resident((D, N))],
        out_specs=pl.BlockSpec((tm, N), lambda i: (i, 0)),
        out_shape=jax.ShapeDtypeStruct((S, N), F32),
        compiler_params=_cparams(),
        name="proj",
    )(h, g.reshape(1, D), w)


def _outproj_kernel(*refs, n):
    r_ref, o_ref = refs[0], refs[-1]
    acc = r_ref[...]
    for k in range(n):
        acc = acc + jnp.dot(refs[1 + k][...].astype(BF16), refs[1 + n + k][...], preferred_element_type=F32)
    o_ref[...] = acc


def _outproj(res, xs, ws, tm):
    S, D = res.shape
    n = len(xs)
    in_specs = [pl.BlockSpec((tm, D), lambda i: (i, 0))]
    in_specs += [pl.BlockSpec((tm, x.shape[1]), lambda i: (i, 0)) for x in xs]
    in_specs += [_resident(w.shape) for w in ws]
    return pl.pallas_call(
        functools.partial(_outproj_kernel, n=n),
        grid=(S // tm,),
        in_specs=in_specs,
        out_specs=pl.BlockSpec((tm, D), lambda i: (i, 0)),
        out_shape=jax.ShapeDtypeStruct((S, D), F32),
        compiler_params=_cparams(),
        name="outproj",
    )(res, *xs, *ws)


def _heads_update(nh, qT_ref, k_ref, vT_ref, s_ref, p_ref, m_ref, l_ref, acc_ref, biased):
    _, tk, tq = s_ref.shape
    rc = min(ATTN_ROWS, tk)
    chunks = range(0, tk, rc)
    m_new = alpha = None
    for t in range(nh + 1):
        mx = ls = None
        if t < nh:
            qT = qT_ref[t]
        if t >= 1:
            hb, m_b, alpha_b = t - 1, m_new, alpha
        for r0 in chunks:
            if t < nh:
                s = biased(jnp.dot(k_ref[t, r0:r0 + rc, :], qT, preferred_element_type=F32), t, r0)
                s_ref[t % 2, r0:r0 + rc, :] = s
                cm = jnp.max(s.reshape(rc // 8, 8, tq), axis=0)
                mx = cm if mx is None else jnp.maximum(mx, cm)
            if t >= 1:
                p = jnp.exp(s_ref[hb % 2, r0:r0 + rc, :] - m_b)
                ps = jnp.sum(p.reshape(rc // 8, 8, tq), axis=0)
                ls = ps if ls is None else ls + ps
                p_ref[hb % 2, r0:r0 + rc, :] = p.astype(BF16)
        if t >= 1:
            l_ref[hb] = alpha_b * l_ref[hb] + jnp.sum(ls, axis=0, keepdims=True)
            acc_ref[hb] = alpha_b * acc_ref[hb] + jnp.dot(vT_ref[hb], p_ref[hb % 2], preferred_element_type=F32)
        if t < nh:
            m_prev = m_ref[t]
            m_new = jnp.maximum(m_prev, jnp.max(mx, axis=0, keepdims=True))
            alpha = jnp.exp(m_prev - m_new)
            m_ref[t] = m_new


def _attn_init(m_ref, l_ref, acc_ref):
    m_ref[...] = jnp.full(m_ref.shape, NEG, F32)
    l_ref[...] = jnp.zeros(l_ref.shape, F32)
    acc_ref[...] = jnp.zeros(acc_ref.shape, F32)


def _attn_finish(o_ref, l_ref, acc_ref, nh, hd):
    for h in range(0, nh, 2):
        pair = jnp.concatenate([acc_ref[h] / l_ref[h], acc_ref[h + 1] / l_ref[h + 1]], axis=0)
        o_ref[:, h * hd:(h + 2) * hd] = pair.T


def _attn_kernel(qi_ref, kj_ref, *refs, mode, nh, tq, tk, hd):
    if mode == "fox":
        qT_ref, k_ref, vT_ref, fq_ref, fk_ref, o_ref, m_ref, l_ref, acc_ref, s_ref, p_ref, cb_ref = refs
    elif mode == "moba":
        qT_ref, k_ref, vT_ref, sb_ref, o_ref, m_ref, l_ref, acc_ref, s_ref, p_ref, cb_ref = refs
    else:
        qT_ref, k_ref, vT_ref, o_ref, m_ref, l_ref, acc_ref, s_ref, p_ref = refs
    p = pl.program_id(0)
    i = qi_ref[p]
    j = kj_ref[p]
    causal = mode != "none"

    if causal:
        @pl.when(p == 0)
        def _():
            row = lax.broadcasted_iota(jnp.int32, (tk, tq), 0)
            col = lax.broadcasted_iota(jnp.int32, (tk, tq), 1)
            cb_ref[...] = jnp.where(row <= col, 0.0, NEG)

    @pl.when(j == 0)
    def _():
        _attn_init(m_ref, l_ref, acc_ref)

    def process(diag):
        def biased(s, h, r0):
            rc = s.shape[0]
            if mode == "fox":
                s = s + fq_ref[h] - jnp.concatenate([fk_ref[h, r0:r0 + rc, :]] * (tq // LANE), axis=1)
            elif mode == "moba":
                s = s + sb_ref[h, pl.ds(j * (tk // MOBA_BLOCK) + r0 // MOBA_BLOCK, 1), :]
            if diag:
                s = s + cb_ref[r0:r0 + rc, :]
            return s

        _heads_update(nh, qT_ref, k_ref, vT_ref, s_ref, p_ref, m_ref, l_ref, acc_ref, biased)

    if causal:
        pl.when(j == i)(lambda: process(True))
        pl.when(j != i)(lambda: process(False))
        last = j == i
    else:
        process(False)
        last = j == 0

    @pl.when(last)
    def _():
        _attn_finish(o_ref, l_ref, acc_ref, nh, hd)


def _attention(mode, qT, k, vT, extra, tq, tk):
    nh, hd, S = qT.shape
    nq = S // tq
    if mode == "none":
        pairs = [(i, 0) for i in range(nq)]
    else:
        assert tq == tk
        pairs = [(i, j) for i in range(nq) for j in range(i + 1)]
    qi = jnp.asarray(np.array([a for a, _ in pairs], np.int32))
    kj = jnp.asarray(np.array([b for _, b in pairs], np.int32))
    in_specs = [pl.BlockSpec((nh, hd, tq), lambda p, qi, kj: (0, 0, qi[p])),
                pl.BlockSpec((nh, tk, hd), lambda p, qi, kj: (0, kj[p], 0)),
                pl.BlockSpec((nh, hd, tk), lambda p, qi, kj: (0, 0, kj[p]))]
    if mode == "fox":
        in_specs += [pl.BlockSpec((nh, 1, tq), lambda p, qi, kj: (0, 0, qi[p])),
                     pl.BlockSpec((nh, tk, LANE), lambda p, qi, kj: (0, kj[p], 0))]
    elif mode == "moba":
        nb = extra[0].shape[1]
        in_specs += [pl.BlockSpec((nh, nb, tq), lambda p, qi, kj: (0, 0, qi[p]))]
    scratch = [pltpu.VMEM((nh, 1, tq), F32), pltpu.VMEM((nh, 1, tq), F32), pltpu.VMEM((nh, hd, tq), F32),
               pltpu.VMEM((2, tk, tq), F32), pltpu.VMEM((2, tk, tq), BF16)]
    if mode != "none":
        scratch.append(pltpu.VMEM((tk, tq), F32))
    return pl.pallas_call(
        functools.partial(_attn_kernel, mode=mode, nh=nh, tq=tq, tk=tk, hd=hd),
        grid_spec=pltpu.PrefetchScalarGridSpec(
            num_scalar_prefetch=2,
            grid=(len(pairs),),
            in_specs=in_specs,
            out_specs=pl.BlockSpec((tq, nh * hd), lambda p, qi, kj: (qi[p], 0)),
            scratch_shapes=scratch),
        out_shape=jax.ShapeDtypeStruct((S, nh * hd), F32),
        compiler_params=_cparams(),
        name="attn_" + mode,
    )(qi, kj, qT, k, vT, *extra)


def _gate_kernel(qT_ref, km_ref, o_ref, *, nh, nb, tq):
    i = pl.program_id(0)
    n = lax.broadcasted_iota(jnp.int32, (nb, tq), 0)
    own = (i * tq + lax.broadcasted_iota(jnp.int32, (nb, tq), 1)) // MOBA_BLOCK
    ninf = jnp.float32(-jnp.inf)

    def body(h, carry):
        g = jnp.dot(km_ref[h], qT_ref[h], preferred_element_type=F32)
        g = jnp.where(n < own, g, ninf)
        keep = n == own
        for _ in range(MOBA_TOPK):
            mx = jnp.max(g, axis=0, keepdims=True)
            first = jnp.min(jnp.where((g == mx) & (mx > ninf), n, nb), axis=0, keepdims=True)
            pick = n == first
            keep = keep | pick
            g = jnp.where(pick, ninf, g)
        o_ref[h] = jnp.where(keep, 0.0, NEG)
        return carry

    lax.fori_loop(0, nh, body, 0)


def _moba_gate(qT, kmean, tq):
    nh, hd, S = qT.shape
    nb = kmean.shape[1]
    return pl.pallas_call(
        functools.partial(_gate_kernel, nh=nh, nb=nb, tq=tq),
        grid=(S // tq,),
        in_specs=[pl.BlockSpec((nh, hd, tq), lambda i: (0, 0, i)), _resident((nh, nb, hd))],
        out_specs=pl.BlockSpec((nh, nb, tq), lambda i: (0, 0, i)),
        out_shape=jax.ShapeDtypeStruct((nh, nb, S), F32),
        compiler_params=_cparams(),
        name="moba_gate",
    )(qT, kmean)


def _dsa_kernel(qi_ref, kj_ref, ph_ref, ikb_ref, kvb_ref, nk_ref,
                iqT_ref, ik_ref, w_ref, qT_ref, k_ref, vT_ref, o_ref,
                keys_ref, thr_ref, cut_ref, m_ref, l_ref, acc_ref, s_ref, p_ref, bias_ref,
                *, nih, nh, tq, tk, hd, topk, seq, chunk):
    p = pl.program_id(0)
    i = qi_ref[p]
    j = kj_ref[p]
    phase = ph_ref[p]
    nkb = nk_ref[p]

    @pl.when(phase == 0)
    def _scores():
        ik = ik_ref[...]
        sc = jnp.zeros((tk, tq), F32)
        for h in range(nih):
            d = jnp.dot(ik, iqT_ref[h], preferred_element_type=F32)
            sc = sc + w_ref[h:h + 1, :] * jnp.maximum(d, 0.0)
        b = lax.bitcast_convert_type(sc, jnp.int32)
        key = b ^ ((b >> 31) & 0x7FFFFFFF)
        row = j * tk + lax.broadcasted_iota(jnp.int32, (tk, tq), 0)
        col = i * tq + lax.broadcasted_iota(jnp.int32, (tk, tq), 1)
        key = jnp.where(row <= col, key, INT_MIN)
        keys_ref[pl.ds(pl.multiple_of(j * tk, tk), tk), :] = key

    def count(hit):
        def body(c, acc):
            r0 = pl.multiple_of(c * chunk, chunk)
            return acc + jnp.sum(hit(keys_ref[pl.ds(r0, chunk), :], r0).reshape(chunk // 8, 8, tq), axis=0)
        acc = lax.fori_loop(0, nkb * (tk // chunk), body, jnp.zeros((8, tq), jnp.int32))
        return jnp.sum(acc, axis=0, keepdims=True)

    @pl.when((phase == 1) & (j == 0))
    def _select():
        c0 = count(lambda blk, r0: jnp.where(blk >= 0, 1, 0))
        thr = jnp.where(c0 >= topk, 0, INT_MIN)

        def bit_body(it, thr):
            cand = thr | jnp.left_shift(jnp.int32(1), 30 - it)
            cnt = count(lambda blk, r0: jnp.where(blk >= cand, 1, 0))
            return jnp.where(cnt >= topk, cand, thr)

        thr = lax.fori_loop(0, 31, bit_body, thr)
        thr = jnp.maximum(thr, INT_MIN + 1)
        thr_ref[...] = thr
        cut_ref[...] = jnp.full((1, tq), seq, jnp.int32)
        cge = count(lambda blk, r0: jnp.where(blk >= thr, 1, 0))

        @pl.when(jnp.max(cge) > topk)
        def _ties():
            need = topk - count(lambda blk, r0: jnp.where(blk > thr, 1, 0))
            nbits = max(1, int(seq - 1).bit_length())

            def pos_body(it, x):
                cand = x + jnp.left_shift(jnp.int32(1), nbits - 1 - it)
                pos = lambda r0: r0 + lax.broadcasted_iota(jnp.int32, (chunk, tq), 0)
                c = count(lambda blk, r0: jnp.where(blk == thr, jnp.where(pos(r0) < cand, 1, 0), 0))
                return jnp.where(c < need, cand, x)

            x = lax.fori_loop(0, nbits, pos_body, jnp.zeros((1, tq), jnp.int32))
            cut_ref[...] = jnp.where(cge > topk, x, seq)

        _attn_init(m_ref, l_ref, acc_ref)

    @pl.when(phase == 1)
    def _attend():
        key = keys_ref[pl.ds(pl.multiple_of(j * tk, tk), tk), :]
        thr = thr_ref[...]
        pos = j * tk + lax.broadcasted_iota(jnp.int32, (tk, tq), 0)
        tie = jnp.where(key == thr, jnp.where(pos <= cut_ref[...], 0.0, NEG), NEG)
        bias_ref[...] = jnp.where(key > thr, 0.0, tie)

        def biased(s, h, r0):
            return s + bias_ref[r0:r0 + s.shape[0], :]

        _heads_update(nh, qT_ref, k_ref, vT_ref, s_ref, p_ref, m_ref, l_ref, acc_ref, biased)

        @pl.when(j == nkb - 1)
        def _():
            _attn_finish(o_ref, l_ref, acc_ref, nh, hd)


def _dsa_attention(iqT, ik, wT, qT, k, vT, tq, tk, topk):
    nh, hd, S = qT.shape
    nih = iqT.shape[0]
    steps = []
    for i in range(S // tq):
        nkb = (i * tq + tq - 1) // tk + 1
        steps += [(i, j, 0, j, 0, nkb) for j in range(nkb)]
        steps += [(i, j, 1, nkb - 1, j, nkb) for j in range(nkb)]
    tab = [jnp.asarray(np.array([s[c] for s in steps], np.int32)) for c in range(6)]
    chunk = min(tk, 512)
    qmap = lambda p, qi, kj, ph, ikb, kvb, nk: (0, 0, qi[p])
    return pl.pallas_call(
        functools.partial(_dsa_kernel, nih=nih, nh=nh, tq=tq, tk=tk, hd=hd, topk=topk, seq=S, chunk=chunk),
        grid_spec=pltpu.PrefetchScalarGridSpec(
            num_scalar_prefetch=6,
            grid=(len(steps),),
            in_specs=[pl.BlockSpec((nih, IDX_DIM, tq), qmap),
                      pl.BlockSpec((tk, IDX_DIM), lambda p, qi, kj, ph, ikb, kvb, nk: (ikb[p], 0)),
                      pl.BlockSpec((nih, tq), lambda p, qi, kj, ph, ikb, kvb, nk: (0, qi[p])),
                      pl.BlockSpec((nh, hd, tq), qmap),
                      pl.BlockSpec((nh, tk, hd), lambda p, qi, kj, ph, ikb, kvb, nk: (0, kvb[p], 0)),
                      pl.BlockSpec((nh, hd, tk), lambda p, qi, kj, ph, ikb, kvb, nk: (0, 0, kvb[p]))],
            out_specs=pl.BlockSpec((tq, nh * hd), lambda p, qi, kj, ph, ikb, kvb, nk: (qi[p], 0)),
            scratch_shapes=[pltpu.VMEM((S, tq), jnp.int32),
                            pltpu.VMEM((1, tq), jnp.int32), pltpu.VMEM((1, tq), jnp.int32),
                            pltpu.VMEM((nh, 1, tq), F32), pltpu.VMEM((nh, 1, tq), F32),
                            pltpu.VMEM((nh, hd, tq), F32),
                            pltpu.VMEM((2, tk, tq), F32), pltpu.VMEM((2, tk, tq), BF16),
                            pltpu.VMEM((tk, tq), F32)]),
        out_shape=jax.ShapeDtypeStruct((S, nh * hd), F32),
        compiler_params=_cparams(),
        name="dsa",
    )(*tab, iqT, ik, wT, qT, k, vT)


def _head_norm(x, g):
    return x * lax.rsqrt(jnp.mean(x * x, axis=-1, keepdims=True) + RMS_EPS) * g


def _rope(x, cos, sin):
    half = ROT_DIM // 2
    c, s = cos[:, None, :], sin[:, None, :]
    x1, x2 = x[..., :half], x[..., half:ROT_DIM]
    return jnp.concatenate([x1 * c - x2 * s, x2 * c + x1 * s, x[..., ROT_DIM:]], axis=-1)


def _qT(x, scale=1.0):
    return (jnp.transpose(x, (1, 2, 0)) * scale).astype(BF16)


def _kH(x):
    return jnp.transpose(x, (1, 0, 2)).astype(BF16)


def _pad_cols(w, mult=LANE):
    n = w.shape[1]
    return jnp.pad(w, ((0, 0), (0, (-n) % mult)))


def kernel(x, mem, ffn1_norm, ffn1_w_in, ffn1_w_out, mix_norm, even_w_in, even_b_forget, fox_q_norm, fox_k_norm, dsa_q_norm, dsa_k_norm, odd_w_in, moba_q_norm, moba_k_norm, mix_w_out, mem_norm_x, mem_norm_m, mem_wq, mem_wkv, mem_q_norm, mem_k_norm, mem_wo, ffn2_norm, ffn2_w_in, ffn2_w_out):
    _, S, D = x.shape
    depth = ffn1_norm.shape[0]
    tm = min(512, S)
    ta = min(512, S)
    tdq = min(256, S)
    scale = HEAD_DIM ** -0.5
    pos = jnp.arange(S, dtype=F32)
    inv = ROPE_THETA ** (-jnp.arange(0, ROT_DIM, 2, dtype=F32) / ROT_DIM)
    ang = pos[:, None] * inv[None, :]
    cos, sin = jnp.cos(ang), jnp.sin(ang)

    h = x[0]
    m0 = mem[0]
    for layer in range(depth):
        h = _ffn(h, ffn1_norm[layer], ffn1_w_in[layer].astype(BF16), ffn1_w_out[layer].astype(BF16), tm)
        if layer % 2 == 0:
            e = layer // 2
            fw, dw = FOX_HEADS * HEAD_DIM, DSA_HEADS * HEAD_DIM
            y = _proj(h, mix_norm[layer], _pad_cols(even_w_in[e]).astype(BF16), tm)
            offs = np.cumsum([0, fw, fw, fw, FOX_HEADS, dw, dw, dw, IDX_HEADS * IDX_DIM, IDX_DIM, IDX_HEADS])
            fq, fk, fv, flog, dq, dk, dv, iq, ik, iw = [y[:, a:b] for a, b in zip(offs[:-1], offs[1:])]
            hv = lambda t, nh: t.reshape(S, nh, HEAD_DIM)
            fq = _head_norm(hv(fq, FOX_HEADS), fox_q_norm[e])
            fk = _head_norm(hv(fk, FOX_HEADS), fox_k_norm[e])
            Fc = jnp.cumsum(jax.nn.log_sigmoid(flog + even_b_forget[e]), axis=0)
            FcT = Fc.T
            fox = _attention("fox", _qT(fq, scale), _kH(fk), _qT(hv(fv, FOX_HEADS)),
                             (FcT[:, None, :], jnp.broadcast_to(FcT[:, :, None], (FOX_HEADS, S, LANE))), ta, ta)
            dq = _rope(_head_norm(hv(dq, DSA_HEADS), dsa_q_norm[e]), cos, sin)
            dk = _rope(_head_norm(hv(dk, DSA_HEADS), dsa_k_norm[e]), cos, sin)
            iq = _rope(hv(iq, IDX_HEADS), cos, sin)
            ik = _rope(ik.reshape(S, 1, IDX_DIM), cos, sin)[:, 0]
            wT = (iw * (IDX_HEADS ** -0.5 * IDX_DIM ** -0.5)).T
            dsa = _dsa_attention(_qT(iq), ik.astype(BF16), wT, _qT(dq, scale), _kH(dk), _qT(hv(dv, DSA_HEADS)),
                                 tdq, ta, min(DSA_TOPK, S // 4))
            wo = mix_w_out[layer].astype(BF16)
            h = _outproj(h, [fox, dsa], [wo[:fw], wo[fw:]], tm)
        else:
            o = layer // 2
            mw = MOBA_HEADS * HEAD_DIM
            y = _proj(h, mix_norm[layer], odd_w_in[o].astype(BF16), tm)
            hv = lambda t: t.reshape(S, MOBA_HEADS, HEAD_DIM)
            q = _rope(_head_norm(hv(y[:, :mw]), moba_q_norm[o]), cos, sin)
            k = _rope(_head_norm(hv(y[:, mw:2 * mw]), moba_k_norm[o]), cos, sin)
            v = hv(y[:, 2 * mw:])
            kmean = jnp.mean(k.reshape(S // MOBA_BLOCK, MOBA_BLOCK, MOBA_HEADS, HEAD_DIM), axis=1)
            qT = _qT(q, scale)
            selb = _moba_gate(qT, jnp.transpose(kmean, (1, 0, 2)).astype(BF16), ta)
            moba = _attention("moba", qT, _kH(k), _qT(v), (selb,), ta, ta)
            h = _outproj(h, [moba], [mix_w_out[layer].astype(BF16)], tm)
        M = m0.shape[0]
        q = _proj(h, mem_norm_x[layer], mem_wq[layer].astype(BF16), tm)
        q = _head_norm(q.reshape(S, MEM_HEADS, HEAD_DIM), mem_q_norm[layer])
        kv = _proj(m0, mem_norm_m[layer], mem_wkv[layer].astype(BF16), M)
        mwid = MEM_HEADS * HEAD_DIM
        mk = _head_norm(kv[:, :mwid].reshape(M, MEM_HEADS, HEAD_DIM), mem_k_norm[layer])
        mv = kv[:, mwid:].reshape(M, MEM_HEADS, HEAD_DIM)
        ma = _attention("none", _qT(q, scale), _kH(mk), _qT(mv), (), ta, M)
        h = _outproj(h, [ma], [mem_wo[layer].astype(BF16)], tm)
        h = _ffn(h, ffn2_norm[layer], ffn2_w_in[layer].astype(BF16), ffn2_w_out[layer].astype(BF16), tm)
    return h[None]
```

```python
import functools

import numpy as np
import jax
import jax.numpy as jnp
from jax import lax
from jax.experimental import pallas as pl
from jax.experimental.pallas import tpu as pltpu

HEAD_DIM = 64
ROT_DIM = HEAD_DIM // 4
ROPE_THETA = 500000.0
FOX_HEADS = 8
DSA_HEADS = 8
MOBA_HEADS = 16
IDX_HEADS = 8
IDX_DIM = 64
DSA_TOPK = 256
MOBA_BLOCK = 256
MOBA_TOPK = 3
MEM_HEADS = 4
RMS_EPS = 1e-6

NEG = -1e30
INT_MIN = -(2 ** 31)
LANE = 128
SUBLANE = 8
KPAD = 128
ATTN_ROWS = 128
VMEM_LIMIT = 56 * 1024 * 1024
LOG2E = 1.4426950408889634
QSCALE = HEAD_DIM ** -0.5 * LOG2E

F32 = jnp.float32
BF16 = jnp.bfloat16


def _cparams(n_axes=1):
    return pltpu.CompilerParams(dimension_semantics=("arbitrary",) * n_axes, vmem_limit_bytes=VMEM_LIMIT)


def _resident(shape):
    nd = len(shape)
    return pl.BlockSpec(shape, lambda *_: (0,) * nd, pipeline_mode=pl.Buffered(1))


def _rms_bf16(x, g):
    ms = jnp.mean(x * x, axis=-1, keepdims=True)
    return (x * lax.rsqrt(ms + RMS_EPS) * g).astype(BF16)


def _ffn_kernel(x_ref, g_ref, wa_ref, wb_ref, wo_ref, o_ref, *, fc):
    x = x_ref[...]
    xn = _rms_bf16(x, g_ref[...])
    acc = jnp.zeros(x.shape, F32)
    for c in range(wa_ref.shape[1] // fc):
        a = jnp.dot(xn, wa_ref[:, c * fc:(c + 1) * fc], preferred_element_type=F32)
        b = jnp.dot(xn, wb_ref[:, c * fc:(c + 1) * fc], preferred_element_type=F32)
        gate = (a * jax.nn.sigmoid(a) * b).astype(BF16)
        acc = acc + jnp.dot(gate, wo_ref[c * fc:(c + 1) * fc, :], preferred_element_type=F32)
    o_ref[...] = x + 0.5 * acc


def _ffn(h, g, w_in, w_out, tm):
    S, D = h.shape
    Fh = w_out.shape[0]
    return pl.pallas_call(
        functools.partial(_ffn_kernel, fc=256),
        grid=(S // tm,),
        in_specs=[pl.BlockSpec((tm, D), lambda i: (i, 0)),
                  _resident((1, D)),
                  pl.BlockSpec((D, Fh), lambda i: (0, 0), pipeline_mode=pl.Buffered(1)),
                  pl.BlockSpec((D, Fh), lambda i: (0, 1), pipeline_mode=pl.Buffered(1)),
                  _resident((Fh, D))],
        out_specs=pl.BlockSpec((tm, D), lambda i: (i, 0)),
        out_shape=jax.ShapeDtypeStruct((S, D), F32),
        compiler_params=_cparams(),
        name="ffn",
    )(h, g.reshape(1, D), w_in, w_in, w_out)


def _outproj_kernel(*refs, n):
    r_ref, o_ref = refs[0], refs[-1]
    acc = r_ref[...]
    for k in range(n):
        acc = acc + jnp.dot(refs[1 + k][...].astype(BF16), refs[1 + n + k][...], preferred_element_type=F32)
    o_ref[...] = acc


def _outproj(res, xs, ws, tm):
    S, D = res.shape
    n = len(xs)
    in_specs = [pl.BlockSpec((tm, D), lambda i: (i, 0))]
    in_specs += [pl.BlockSpec((tm, x.shape[1]), lambda i: (i, 0)) for x in xs]
    in_specs += [_resident(w.shape) for w in ws]
    return pl.pallas_call(
        functools.partial(_outproj_kernel, n=n),
        grid=(S // tm,),
        in_specs=in_specs,
        out_specs=pl.BlockSpec((tm, D), lambda i: (i, 0)),
        out_shape=jax.ShapeDtypeStruct((S, D), F32),
        compiler_params=_cparams(),
        name="outproj",
    )(res, *xs, *ws)


def _lanes(ref, idx, tm):
    return jnp.concatenate([ref[idx]] * (tm // LANE), axis=1)


def _bf16_split3(x):
    a = x.astype(BF16).astype(F32)
    b = (x - a).astype(BF16).astype(F32)
    return a, b, x - a - b


def _heads_kernel(*refs, plan, tm, n_gain, rope, fox, kmean):
    it = iter(refs)
    x_ref, g_ref, w_ref = next(it), next(it), next(it)
    gn_ref = next(it) if n_gain else None
    cos = sin = None
    if rope:
        cos, sin = next(it)[...], next(it)[...]
    bf_ref = next(it) if fox else None
    n_out = 1 + max(p[6] for p in plan)
    outs = [next(it) for _ in range(n_out)]
    wt_ref = next(it) if fox else None
    km_ref = next(it) if kmean else None
    carry_ref = next(it) if fox else None

    y = jnp.dot(_rms_bf16(x_ref[...], g_ref[...]), w_ref[...], preferred_element_type=F32)
    zeros_pad = jnp.zeros((KPAD - HEAD_DIM, tm), F32)
    half = ROT_DIM // 2

    fq_rows = fk_rows = None
    if fox:
        @pl.when(pl.program_id(0) == 0)
        def _():
            carry_ref[...] = jnp.zeros(carry_ref.shape, F32)

        gT = y[:, y.shape[1] - LANE:].T
        xg = gT[0:FOX_HEADS] + _lanes(bf_ref, slice(None), tm)
        lf = jnp.minimum(xg, 0.0) - jnp.log1p(jnp.exp(-jnp.abs(xg)))
        tri = jnp.where(lax.broadcasted_iota(jnp.int32, (tm, tm), 0)
                        <= lax.broadcasted_iota(jnp.int32, (tm, tm), 1), 1.0, 0.0).astype(BF16)
        cs = sum(jnp.dot(part.astype(BF16), tri, preferred_element_type=F32) for part in _bf16_split3(lf))
        fcum = (carry_ref[...] + cs) * LOG2E
        carry_ref[...] = carry_ref[...] + jnp.sum(lf, axis=1, keepdims=True)
        wt_ref[...] = gT[FOX_HEADS:FOX_HEADS + IDX_HEADS] * (IDX_HEADS ** -0.5 * IDX_DIM ** -0.5)
        rid = lax.broadcasted_iota(jnp.int32, (SUBLANE, tm), 0)
        fq_rows, fk_rows = [], []
        for h in range(FOX_HEADS):
            f1, f2, f3 = _bf16_split3(fcum[h:h + 1])
            sel = lambda a, b, c, one: jnp.where(rid == 0, a, jnp.where(rid == 1, b, jnp.where(rid == 2, c, one)))
            ones_lo = jnp.where(rid < 6, 1.0, 0.0)
            fq_rows.append(sel(f1, f2, f3, ones_lo))
            fk_rows.append(jnp.where(rid < 3, 1.0, jnp.where(rid == 3, -f1, jnp.where(rid == 4, -f2,
                           jnp.where(rid == 5, -f3, 0.0)))))

    for kind, col, nheads, head0, gain, use_rope, out in plan:
        width = -(-nheads * HEAD_DIM // LANE) * LANE
        segT = y[:, col:col + width].T
        o_ref = outs[out]
        for hh in range(nheads):
            r = segT[hh * HEAD_DIM:(hh + 1) * HEAD_DIM]
            if gain is not None:
                r = r * lax.rsqrt(jnp.mean(r * r, axis=0, keepdims=True) + RMS_EPS) * _lanes(gn_ref, gain, tm)
            if use_rope:
                x1, x2 = r[:half], r[half:ROT_DIM]
                r = jnp.concatenate([x1 * cos - x2 * sin, x2 * cos + x1 * sin, r[ROT_DIM:]], axis=0)
            h = head0 + hh
            if kind == "v":
                o_ref[h] = r.astype(BF16)
                continue
            if kind == "q":
                r = r * QSCALE
            extra = None
            if fox and out == 0:
                extra = fq_rows[h]
            elif fox and out == 1:
                extra = fk_rows[h]
            if extra is None:
                padded = jnp.concatenate([r, zeros_pad], axis=0)
            else:
                padded = jnp.concatenate([r, extra, zeros_pad[SUBLANE:]], axis=0)
            if kind == "q":
                o_ref[h] = padded.astype(BF16)
            else:
                o_ref[h] = padded.T.astype(BF16)
                if kmean:
                    lane = lax.broadcasted_iota(jnp.int32, (HEAD_DIM, LANE), 1)
                    tile = jnp.zeros((HEAD_DIM, LANE), F32)
                    for b in range(tm // MOBA_BLOCK):
                        mb = jnp.sum(r[:, b * MOBA_BLOCK:(b + 1) * MOBA_BLOCK], axis=1, keepdims=True)
                        tile = jnp.where(lane == b, mb * (1.0 / MOBA_BLOCK), tile)
                    km_ref[0, h] = tile


def _heads_proj(x, g, w, plan, out_heads, tm, gains=None, rope=None, b_forget=None, kmean=False):
    S, D = x.shape
    N = w.shape[1]
    fox = b_forget is not None
    in_specs = [pl.BlockSpec((tm, D), lambda i: (i, 0)), _resident((1, D)), _resident((D, N))]
    args = [x, g.reshape(1, D), w]
    if gains is not None:
        gt = jnp.broadcast_to(gains[:, :, None], gains.shape + (LANE,)).astype(F32)
        in_specs.append(_resident(gt.shape))
        args.append(gt)
    if rope is not None:
        in_specs += [pl.BlockSpec((ROT_DIM // 2, tm), lambda i: (0, i))] * 2
        args += list(rope)
    if fox:
        in_specs.append(_resident((FOX_HEADS, LANE)))
        args.append(jnp.broadcast_to(b_forget[:, None], (FOX_HEADS, LANE)).astype(F32))
    out_specs, out_shapes = [], []
    for kind, nh in out_heads:
        if kind == "q":
            out_specs.append(pl.BlockSpec((nh, KPAD, tm), lambda i: (0, 0, i)))
            out_shapes.append(jax.ShapeDtypeStruct((nh, KPAD, S), BF16))
        elif kind == "k":
            out_specs.append(pl.BlockSpec((nh, tm, KPAD), lambda i: (0, i, 0)))
            out_shapes.append(jax.ShapeDtypeStruct((nh, S, KPAD), BF16))
        else:
            out_specs.append(pl.BlockSpec((nh, HEAD_DIM, tm), lambda i: (0, 0, i)))
            out_shapes.append(jax.ShapeDtypeStruct((nh, HEAD_DIM, S), BF16))
    if fox:
        out_specs.append(pl.BlockSpec((IDX_HEADS, tm), lambda i: (0, i)))
        out_shapes.append(jax.ShapeDtypeStruct((IDX_HEADS, S), F32))
    if kmean:
        nhk = out_heads[1][1]
        out_specs.append(pl.BlockSpec((1, nhk, HEAD_DIM, LANE), lambda i: (i, 0, 0, 0)))
        out_shapes.append(jax.ShapeDtypeStruct((S // tm, nhk, HEAD_DIM, LANE), F32))
    return pl.pallas_call(
        functools.partial(_heads_kernel, plan=tuple(plan), tm=tm, n_gain=0 if gains is None else gains.shape[0],
                          rope=rope is not None, fox=fox, kmean=kmean),
        grid=(S // tm,),
        in_specs=in_specs,
        out_specs=out_specs,
        out_shape=out_shapes,
        scratch_shapes=[pltpu.VMEM((FOX_HEADS, 1), F32)] if fox else [],
        compiler_params=_cparams(),
        name="heads_proj",
    )(*args)


def _heads_update(nh, qT_ref, k_ref, vT_ref, s_ref, p_ref, m_ref, l_ref, acc_ref, biased):
    _, tk, tq = s_ref.shape
    rc = min(ATTN_ROWS, tk)
    chunks = range(0, tk, rc)
    m_new = alpha = None
    for t in range(nh + 1):
        mx = ls = None
        if t < nh:
            qT = qT_ref[t]
        if t >= 1:
            hb, m_b, alpha_b = t - 1, m_new, alpha
        for r0 in chunks:
            if t < nh:
                s = biased(jnp.dot(k_ref[t, r0:r0 + rc, :], qT, preferred_element_type=F32), t, r0)
                s_ref[t % 2, r0:r0 + rc, :] = s
                cm = jnp.max(s.reshape(rc // SUBLANE, SUBLANE, tq), axis=0)
                mx = cm if mx is None else jnp.maximum(mx, cm)
            if t >= 1:
                p = jnp.exp2(s_ref[hb % 2, r0:r0 + rc, :] - m_b)
                ps = jnp.sum(p.reshape(rc // SUBLANE, SUBLANE, tq), axis=0)
                ls = ps if ls is None else ls + ps
                p_ref[hb % 2, r0:r0 + rc, :] = p.astype(BF16)
        if t >= 1:
            l_ref[hb] = alpha_b * l_ref[hb] + jnp.sum(ls, axis=0, keepdims=True)
            acc_ref[hb] = alpha_b * acc_ref[hb] + jnp.dot(vT_ref[hb], p_ref[hb % 2], preferred_element_type=F32)
        if t < nh:
            m_prev = m_ref[t]
            m_new = jnp.maximum(m_prev, jnp.max(mx, axis=0, keepdims=True))
            alpha = jnp.exp2(m_prev - m_new)
            m_ref[t] = m_new


def _attn_init(m_ref, l_ref, acc_ref):
    m_ref[...] = jnp.full(m_ref.shape, NEG, F32)
    l_ref[...] = jnp.zeros(l_ref.shape, F32)
    acc_ref[...] = jnp.zeros(acc_ref.shape, F32)


def _attn_finish(o_ref, l_ref, acc_ref, nh, hd):
    for h in range(0, nh, 2):
        pair = jnp.concatenate([acc_ref[h] / l_ref[h], acc_ref[h + 1] / l_ref[h + 1]], axis=0)
        o_ref[:, h * hd:(h + 2) * hd] = pair.T


def _attn_kernel(qi_ref, kj_ref, *refs, mode, nh, tq, tk, hd):
    if mode == "moba":
        qT_ref, k_ref, vT_ref, sb_ref, o_ref, m_ref, l_ref, acc_ref, s_ref, p_ref, cb_ref = refs
    elif mode == "causal":
        qT_ref, k_ref, vT_ref, o_ref, m_ref, l_ref, acc_ref, s_ref, p_ref, cb_ref = refs
    else:
        qT_ref, k_ref, vT_ref, o_ref, m_ref, l_ref, acc_ref, s_ref, p_ref = refs
    p = pl.program_id(0)
    i = qi_ref[p]
    j = kj_ref[p]
    causal = mode != "none"

    if causal:
        @pl.when(p == 0)
        def _():
            row = lax.broadcasted_iota(jnp.int32, (tk, tq), 0)
            col = lax.broadcasted_iota(jnp.int32, (tk, tq), 1)
            cb_ref[...] = jnp.where(row <= col, 0.0, NEG)

    @pl.when(j == 0)
    def _():
        _attn_init(m_ref, l_ref, acc_ref)

    def process(diag):
        def biased(s, h, r0):
            if mode == "moba":
                s = s + sb_ref[h, pl.ds(j * (tk // MOBA_BLOCK) + r0 // MOBA_BLOCK, 1), :]
            if diag:
                s = s + cb_ref[r0:r0 + s.shape[0], :]
            return s

        _heads_update(nh, qT_ref, k_ref, vT_ref, s_ref, p_ref, m_ref, l_ref, acc_ref, biased)

    if causal:
        pl.when(j == i)(lambda: process(True))
        pl.when(j != i)(lambda: process(False))
        last = j == i
    else:
        process(False)
        last = j == 0

    @pl.when(last)
    def _():
        _attn_finish(o_ref, l_ref, acc_ref, nh, hd)


def _attention(mode, qT, k, vT, extra, tq, tk):
    nh, _, S = qT.shape
    hd = vT.shape[1]
    nq = S // tq
    if mode == "none":
        pairs = [(i, 0) for i in range(nq)]
    else:
        assert tq == tk
        pairs = [(i, j) for i in range(nq) for j in range(i + 1)]
    qi = jnp.asarray(np.array([a for a, _ in pairs], np.int32))
    kj = jnp.asarray(np.array([b for _, b in pairs], np.int32))
    in_specs = [pl.BlockSpec((nh, KPAD, tq), lambda p, qi, kj: (0, 0, qi[p])),
                pl.BlockSpec((nh, tk, KPAD), lambda p, qi, kj: (0, kj[p], 0)),
                pl.BlockSpec((nh, hd, tk), lambda p, qi, kj: (0, 0, kj[p]))]
    if mode == "moba":
        nb = extra[0].shape[1]
        in_specs += [pl.BlockSpec((nh, nb, tq), lambda p, qi, kj: (0, 0, qi[p]))]
    scratch = [pltpu.VMEM((nh, 1, tq), F32), pltpu.VMEM((nh, 1, tq), F32), pltpu.VMEM((nh, hd, tq), F32),
               pltpu.VMEM((2, tk, tq), F32), pltpu.VMEM((2, tk, tq), BF16)]
    if mode != "none":
        scratch.append(pltpu.VMEM((tk, tq), F32))
    return pl.pallas_call(
        functools.partial(_attn_kernel, mode=mode, nh=nh, tq=tq, tk=tk, hd=hd),
        grid_spec=pltpu.PrefetchScalarGridSpec(
            num_scalar_prefetch=2,
            grid=(len(pairs),),
            in_specs=in_specs,
            out_specs=pl.BlockSpec((tq, nh * hd), lambda p, qi, kj: (qi[p], 0)),
            scratch_shapes=scratch),
        out_shape=jax.ShapeDtypeStruct((S, nh * hd), F32),
        compiler_params=_cparams(),
        name="attn_" + mode,
    )(qi, kj, qT, k, vT, *extra)


def _gate_kernel(qT_ref, km_ref, o_ref, *, nh, nb, tq):
    i = pl.program_id(0)
    n = lax.broadcasted_iota(jnp.int32, (nb, tq), 0)
    own = (i * tq + lax.broadcasted_iota(jnp.int32, (nb, tq), 1)) // MOBA_BLOCK
    ninf = jnp.float32(-jnp.inf)

    def body(h, carry):
        g = jnp.dot(km_ref[h], qT_ref[h], preferred_element_type=F32)
        g = jnp.where(n < own, g, ninf)
        keep = n == own
        for _ in range(MOBA_TOPK):
            mx = jnp.max(g, axis=0, keepdims=True)
            first = jnp.min(jnp.where((g == mx) & (mx > ninf), n, nb), axis=0, keepdims=True)
            pick = n == first
            keep = keep | pick
            g = jnp.where(pick, ninf, g)
        o_ref[h] = jnp.where(keep, 0.0, NEG)
        return carry

    lax.fori_loop(0, nh, body, 0)


def _moba_gate(qT, kmean, tq):
    nh, _, S = qT.shape
    nb = kmean.shape[1]
    return pl.pallas_call(
        functools.partial(_gate_kernel, nh=nh, nb=nb, tq=tq),
        grid=(S // tq,),
        in_specs=[pl.BlockSpec((nh, KPAD, tq), lambda i: (0, 0, i)), _resident((nh, nb, KPAD))],
        out_specs=pl.BlockSpec((nh, nb, tq), lambda i: (0, 0, i)),
        out_shape=jax.ShapeDtypeStruct((nh, nb, S), F32),
        compiler_params=_cparams(),
        name="moba_gate",
    )(qT, kmean)


def _dsa_kernel(qi_ref, kj_ref, ph_ref, ikb_ref, kvb_ref, nk_ref,
                iqT_ref, ik_ref, w_ref, qT_ref, k_ref, vT_ref, o_ref,
                keys_ref, thr_ref, cut_ref, m_ref, l_ref, acc_ref, s_ref, p_ref, bias_ref,
                *, nih, nh, tq, tk, hd, topk, seq, chunk):
    p = pl.program_id(0)
    i = qi_ref[p]
    j = kj_ref[p]
    phase = ph_ref[p]
    nkb = nk_ref[p]

    @pl.when(phase == 0)
    def _scores():
        ik = ik_ref[...]
        sc = jnp.zeros((tk, tq), F32)
        for h in range(nih):
            d = jnp.dot(ik, iqT_ref[h], preferred_element_type=F32)
            sc = sc + w_ref[h:h + 1, :] * jnp.maximum(d, 0.0)
        b = lax.bitcast_convert_type(sc, jnp.int32)
        key = b ^ ((b >> 31) & 0x7FFFFFFF)
        row = j * tk + lax.broadcasted_iota(jnp.int32, (tk, tq), 0)
        col = i * tq + lax.broadcasted_iota(jnp.int32, (tk, tq), 1)
        key = jnp.where(row <= col, key, INT_MIN)
        keys_ref[pl.ds(pl.multiple_of(j * tk, tk), tk), :] = key

    def count(hit):
        def body(c, acc):
            r0 = pl.multiple_of(c * chunk, chunk)
            return acc + jnp.sum(hit(keys_ref[pl.ds(r0, chunk), :], r0).reshape(chunk // SUBLANE, SUBLANE, tq), axis=0)
        acc = lax.fori_loop(0, nkb * (tk // chunk), body, jnp.zeros((SUBLANE, tq), jnp.int32))
        return jnp.sum(acc, axis=0, keepdims=True)

    @pl.when((phase == 1) & (j == 0))
    def _select():
        c0 = count(lambda blk, r0: jnp.where(blk >= 0, 1, 0))
        thr = jnp.where(c0 >= topk, 0, INT_MIN)

        def bit_body(it, thr):
            cand = thr | jnp.left_shift(jnp.int32(1), 30 - it)
            cnt = count(lambda blk, r0: jnp.where(blk >= cand, 1, 0))
            return jnp.where(cnt >= topk, cand, thr)

        thr = lax.fori_loop(0, 31, bit_body, thr)
        thr = jnp.maximum(thr, INT_MIN + 1)
        thr_ref[...] = thr
        cut_ref[...] = jnp.full((1, tq), seq, jnp.int32)
        cge = count(lambda blk, r0: jnp.where(blk >= thr, 1, 0))

        @pl.when(jnp.max(cge) > topk)
        def _ties():
            need = topk - count(lambda blk, r0: jnp.where(blk > thr, 1, 0))
            nbits = max(1, int(seq - 1).bit_length())

            def pos_body(it, x):
                cand = x + jnp.left_shift(jnp.int32(1), nbits - 1 - it)
                pos = lambda r0: r0 + lax.broadcasted_iota(jnp.int32, (chunk, tq), 0)
                c = count(lambda blk, r0: jnp.where(blk == thr, jnp.where(pos(r0) < cand, 1, 0), 0))
                return jnp.where(c < need, cand, x)

            x = lax.fori_loop(0, nbits, pos_body, jnp.zeros((1, tq), jnp.int32))
            cut_ref[...] = jnp.where(cge > topk, x, seq)

        _attn_init(m_ref, l_ref, acc_ref)

    @pl.when(phase == 1)
    def _attend():
        key = keys_ref[pl.ds(pl.multiple_of(j * tk, tk), tk), :]
        thr = thr_ref[...]
        pos = j * tk + lax.broadcasted_iota(jnp.int32, (tk, tq), 0)
        tie = jnp.where(key == thr, jnp.where(pos <= cut_ref[...], 0.0, NEG), NEG)
        bias_ref[...] = jnp.where(key > thr, 0.0, tie)

        def biased(s, h, r0):
            return s + bias_ref[r0:r0 + s.shape[0], :]

        _heads_update(nh, qT_ref, k_ref, vT_ref, s_ref, p_ref, m_ref, l_ref, acc_ref, biased)

        @pl.when(j == nkb - 1)
        def _():
            _attn_finish(o_ref, l_ref, acc_ref, nh, hd)


def _dsa_attention(iqT, ik, wT, qT, k, vT, tq, tk, topk):
    nh, _, S = qT.shape
    hd = vT.shape[1]
    nih = iqT.shape[0]
    steps = []
    for i in range(S // tq):
        nkb = (i * tq + tq - 1) // tk + 1
        steps += [(i, j, 0, j, 0, nkb) for j in range(nkb)]
        steps += [(i, j, 1, nkb - 1, j, nkb) for j in range(nkb)]
    tab = [jnp.asarray(np.array([s[c] for s in steps], np.int32)) for c in range(6)]
    chunk = min(tk, 512)
    qmap = lambda p, qi, kj, ph, ikb, kvb, nk: (0, 0, qi[p])
    return pl.pallas_call(
        functools.partial(_dsa_kernel, nih=nih, nh=nh, tq=tq, tk=tk, hd=hd, topk=topk, seq=S, chunk=chunk),
        grid_spec=pltpu.PrefetchScalarGridSpec(
            num_scalar_prefetch=6,
            grid=(len(steps),),
            in_specs=[pl.BlockSpec((nih, KPAD, tq), qmap),
                      pl.BlockSpec((tk, KPAD), lambda p, qi, kj, ph, ikb, kvb, nk: (ikb[p], 0)),
                      pl.BlockSpec((nih, tq), lambda p, qi, kj, ph, ikb, kvb, nk: (0, qi[p])),
                      pl.BlockSpec((nh, KPAD, tq), qmap),
                      pl.BlockSpec((nh, tk, KPAD), lambda p, qi, kj, ph, ikb, kvb, nk: (0, kvb[p], 0)),
                      pl.BlockSpec((nh, hd, tk), lambda p, qi, kj, ph, ikb, kvb, nk: (0, 0, kvb[p]))],
            out_specs=pl.BlockSpec((tq, nh * hd), lambda p, qi, kj, ph, ikb, kvb, nk: (qi[p], 0)),
            scratch_shapes=[pltpu.VMEM((S, tq), jnp.int32),
                            pltpu.VMEM((1, tq), jnp.int32), pltpu.VMEM((1, tq), jnp.int32),
                            pltpu.VMEM((nh, 1, tq), F32), pltpu.VMEM((nh, 1, tq), F32),
                            pltpu.VMEM((nh, hd, tq), F32),
                            pltpu.VMEM((2, tk, tq), F32), pltpu.VMEM((2, tk, tq), BF16),
                            pltpu.VMEM((tk, tq), F32)]),
        out_shape=jax.ShapeDtypeStruct((S, nh * hd), F32),
        compiler_params=_cparams(),
        name="dsa",
    )(*tab, iqT, ik, wT, qT, k, vT)


def kernel(x, mem, ffn1_norm, ffn1_w_in, ffn1_w_out, mix_norm, even_w_in, even_b_forget, fox_q_norm, fox_k_norm, dsa_q_norm, dsa_k_norm, odd_w_in, moba_q_norm, moba_k_norm, mix_w_out, mem_norm_x, mem_norm_m, mem_wq, mem_wkv, mem_q_norm, mem_k_norm, mem_wo, ffn2_norm, ffn2_w_in, ffn2_w_out):
    _, S, D = x.shape
    M = mem.shape[1]
    depth = ffn1_norm.shape[0]
    tm = min(512, S)
    ta = min(512, S)
    tdq = min(256, S)
    pos = jnp.arange(S, dtype=F32)
    inv = ROPE_THETA ** (-jnp.arange(0, ROT_DIM, 2, dtype=F32) / ROT_DIM)
    ang = inv[:, None] * pos[None, :]
    rope = (jnp.cos(ang), jnp.sin(ang))
    W8 = 8 * HEAD_DIM
    zcols = lambda n: jnp.zeros((D, n), F32)

    h = x[0]
    m0 = mem[0]
    for layer in range(depth):
        h = _ffn(h, ffn1_norm[layer], ffn1_w_in[layer].astype(BF16), ffn1_w_out[layer].astype(BF16), tm)
        if layer % 2 == 0:
            e = layer // 2
            o = np.cumsum([0, W8, W8, W8, FOX_HEADS, W8, W8, W8, IDX_HEADS * IDX_DIM, IDX_DIM, IDX_HEADS])
            w = even_w_in[e]
            seg = lambda n: w[:, o[n]:o[n + 1]]
            w = jnp.concatenate([seg(0), seg(1), seg(2), seg(4), seg(5), seg(6), seg(7),
                                 seg(8), zcols(KPAD - IDX_DIM),
                                 seg(3), seg(9), zcols(LANE - FOX_HEADS - IDX_HEADS)], axis=1).astype(BF16)
            plan = [("q", 0, 8, 0, 0, False, 0), ("k", W8, 8, 0, 1, False, 1), ("v", 2 * W8, 8, 0, None, False, 2),
                    ("q", 3 * W8, 8, 0, 2, True, 3), ("k", 4 * W8, 8, 0, 3, True, 4), ("v", 5 * W8, 8, 0, None, False, 5),
                    ("q", 6 * W8, 8, 0, None, True, 6), ("k", 7 * W8, 1, 0, None, True, 7)]
            gains = jnp.stack([fox_q_norm[e], fox_k_norm[e], dsa_q_norm[e], dsa_k_norm[e]])
            fqT, fk, fvT, dqT, dk, dvT, iqT, ik, wT = _heads_proj(
                h, mix_norm[layer], w, plan,
                [("q", 8), ("k", 8), ("v", 8), ("q", 8), ("k", 8), ("v", 8), ("q", 8), ("k", 1)],
                tm, gains=gains, rope=rope, b_forget=even_b_forget[e])
            fox = _attention("causal", fqT, fk, fvT, (), ta, ta)
            dsa = _dsa_attention(iqT, ik[0], wT, dqT, dk, dvT, tdq, ta, min(DSA_TOPK, S // 4))
            wo = mix_w_out[layer].astype(BF16)
            h = _outproj(h, [fox, dsa], [wo[:W8], wo[W8:]], tm)
        else:
            od = layer // 2
            w = odd_w_in[od].astype(BF16)
            plan = [("q", 0, 8, 0, 0, True, 0), ("q", W8, 8, 8, 0, True, 0),
                    ("k", 2 * W8, 8, 0, 1, True, 1), ("k", 3 * W8, 8, 8, 1, True, 1),
                    ("v", 4 * W8, 8, 0, None, False, 2), ("v", 5 * W8, 8, 8, None, False, 2)]
            gains = jnp.stack([moba_q_norm[od], moba_k_norm[od]])
            qT, k, vT, km = _heads_proj(h, mix_norm[layer], w, plan, [("q", 16), ("k", 16), ("v", 16)],
                                        tm, gains=gains, rope=rope, kmean=True)
            nbt = tm // MOBA_BLOCK
            km = jnp.transpose(km[..., :nbt], (1, 0, 3, 2)).reshape(MOBA_HEADS, S // MOBA_BLOCK, HEAD_DIM)
            km = jnp.pad(km, ((0, 0), (0, 0), (0, KPAD - HEAD_DIM))).astype(BF16)
            selb = _moba_gate(qT, km, ta)
            moba = _attention("moba", qT, k, vT, (selb,), ta, ta)
            h = _outproj(h, [moba], [mix_w_out[layer].astype(BF16)], tm)
        mw = MEM_HEADS * HEAD_DIM
        (mqT,) = _heads_proj(h, mem_norm_x[layer], mem_wq[layer].astype(BF16),
                             [("q", 0, MEM_HEADS, 0, 0, False, 0)], [("q", MEM_HEADS)], tm,
                             gains=mem_q_norm[layer][None])
        mk, mvT = _heads_proj(m0, mem_norm_m[layer], mem_wkv[layer].astype(BF16),
                              [("k", 0, MEM_HEADS, 0, 0, False, 0), ("v", mw, MEM_HEADS, 0, None, False, 1)],
                              [("k", MEM_HEADS), ("v", MEM_HEADS)], M, gains=mem_k_norm[layer][None])
        ma = _attention("none", mqT, mk, mvT, (), ta, M)
        h = _outproj(h, [ma], [mem_wo[layer].astype(BF16)], tm)
        h = _ffn(h, ffn2_norm[layer], ffn2_w_in[layer].astype(BF16), ffn2_w_out[layer].astype(BF16), tm)
    return h[None]
```

```python
import functools

import numpy as np
import jax
import jax.numpy as jnp
from jax import lax
from jax.experimental import pallas as pl
from jax.experimental.pallas import tpu as pltpu

HEAD_DIM = 64
ROT_DIM = HEAD_DIM // 4
ROPE_THETA = 500000.0
FOX_HEADS = 8
DSA_HEADS = 8
MOBA_HEADS = 16
IDX_HEADS = 8
IDX_DIM = 64
DSA_TOPK = 256
MOBA_BLOCK = 256
MOBA_TOPK = 3
MEM_HEADS = 4
RMS_EPS = 1e-6

NEG = -1e30
INT_MIN = -(2 ** 31)
LANE = 128
SUBLANE = 8
KPAD = 128
VROWS = HEAD_DIM + 16
ATTN_ROWS = 256
VMEM_LIMIT = 56 * 1024 * 1024
LOG2E = 1.4426950408889634
QSCALE = HEAD_DIM ** -0.5 * LOG2E

F32 = jnp.float32
BF16 = jnp.bfloat16


def _cparams(n_axes=1):
    return pltpu.CompilerParams(dimension_semantics=("arbitrary",) * n_axes, vmem_limit_bytes=VMEM_LIMIT)


def _resident(shape):
    nd = len(shape)
    return pl.BlockSpec(shape, lambda *_: (0,) * nd, pipeline_mode=pl.Buffered(1))


def _rms_bf16(x, g):
    ms = jnp.mean(x * x, axis=-1, keepdims=True)
    return (x * lax.rsqrt(ms + RMS_EPS) * g).astype(BF16)


def _ffn_kernel(x_ref, g_ref, wa_ref, wb_ref, wo_ref, o_ref, *, fc):
    x = x_ref[...]
    xn = _rms_bf16(x, g_ref[...])
    acc = jnp.zeros(x.shape, F32)
    for c in range(wa_ref.shape[1] // fc):
        a = jnp.dot(xn, wa_ref[:, c * fc:(c + 1) * fc], preferred_element_type=F32)
        b = jnp.dot(xn, wb_ref[:, c * fc:(c + 1) * fc], preferred_element_type=F32)
        gate = (a * jax.nn.sigmoid(a) * b).astype(BF16)
        acc = acc + jnp.dot(gate, wo_ref[c * fc:(c + 1) * fc, :], preferred_element_type=F32)
    o_ref[...] = x + 0.5 * acc


def _ffn(h, g, w_in, w_out, tm):
    S, D = h.shape
    Fh = w_out.shape[0]
    return pl.pallas_call(
        functools.partial(_ffn_kernel, fc=256),
        grid=(S // tm,),
        in_specs=[pl.BlockSpec((tm, D), lambda i: (i, 0)),
                  _resident((1, D)),
                  pl.BlockSpec((D, Fh), lambda i: (0, 0), pipeline_mode=pl.Buffered(1)),
                  pl.BlockSpec((D, Fh), lambda i: (0, 1), pipeline_mode=pl.Buffered(1)),
                  _resident((Fh, D))],
        out_specs=pl.BlockSpec((tm, D), lambda i: (i, 0)),
        out_shape=jax.ShapeDtypeStruct((S, D), F32),
        compiler_params=_cparams(),
        name="ffn",
    )(h, g.reshape(1, D), w_in, w_in, w_out)


def _outproj_kernel(*refs, n):
    r_ref, o_ref = refs[0], refs[-1]
    acc = r_ref[...]
    for k in range(n):
        acc = acc + jnp.dot(refs[1 + k][...].astype(BF16), refs[1 + n + k][...], preferred_element_type=F32)
    o_ref[...] = acc


def _outproj(res, xs, ws, tm):
    S, D = res.shape
    n = len(xs)
    in_specs = [pl.BlockSpec((tm, D), lambda i: (i, 0))]
    in_specs += [pl.BlockSpec((tm, x.shape[1]), lambda i: (i, 0)) for x in xs]
    in_specs += [_resident(w.shape) for w in ws]
    return pl.pallas_call(
        functools.partial(_outproj_kernel, n=n),
        grid=(S // tm,),
        in_specs=in_specs,
        out_specs=pl.BlockSpec((tm, D), lambda i: (i, 0)),
        out_shape=jax.ShapeDtypeStruct((S, D), F32),
        compiler_params=_cparams(),
        name="outproj",
    )(res, *xs, *ws)


def _lanes(ref, idx, tm):
    return jnp.concatenate([ref[idx]] * (tm // LANE), axis=1)


def _bf16_split3(x):
    a = x.astype(BF16).astype(F32)
    b = (x - a).astype(BF16).astype(F32)
    return a, b, x - a - b


def _heads_kernel(*refs, plan, tm, n_gain, rope, fox, kmean):
    it = iter(refs)
    x_ref, g_ref, w_ref = next(it), next(it), next(it)
    gn_ref = next(it) if n_gain else None
    cos = sin = None
    if rope:
        cos, sin = next(it)[...], next(it)[...]
    bf_ref = next(it) if fox else None
    n_out = 1 + max(p[6] for p in plan)
    outs = [next(it) for _ in range(n_out)]
    wt_ref = next(it) if fox else None
    km_ref = next(it) if kmean else None
    carry_ref = next(it) if fox else None

    y = jnp.dot(_rms_bf16(x_ref[...], g_ref[...]), w_ref[...], preferred_element_type=F32)
    zeros_pad = jnp.zeros((KPAD - HEAD_DIM, tm), F32)
    half = ROT_DIM // 2

    fq_rows = fk_rows = None
    if fox:
        @pl.when(pl.program_id(0) == 0)
        def _():
            carry_ref[...] = jnp.zeros(carry_ref.shape, F32)

        gT = y[:, y.shape[1] - LANE:].T
        xg = gT[0:FOX_HEADS] + _lanes(bf_ref, slice(None), tm)
        lf = jnp.minimum(xg, 0.0) - jnp.log1p(jnp.exp(-jnp.abs(xg)))
        tri = jnp.where(lax.broadcasted_iota(jnp.int32, (tm, tm), 0)
                        <= lax.broadcasted_iota(jnp.int32, (tm, tm), 1), 1.0, 0.0).astype(BF16)
        cs = sum(jnp.dot(part.astype(BF16), tri, preferred_element_type=F32) for part in _bf16_split3(lf))
        fcum = (carry_ref[...] + cs) * LOG2E
        carry_ref[...] = carry_ref[...] + jnp.sum(lf, axis=1, keepdims=True)
        wt_ref[...] = gT[FOX_HEADS:FOX_HEADS + IDX_HEADS] * (IDX_HEADS ** -0.5 * IDX_DIM ** -0.5)
        rid = lax.broadcasted_iota(jnp.int32, (SUBLANE, tm), 0)
        fq_rows, fk_rows = [], []
        for h in range(FOX_HEADS):
            f1, f2, f3 = _bf16_split3(fcum[h:h + 1])
            sel = lambda a, b, c, one: jnp.where(rid == 0, a, jnp.where(rid == 1, b, jnp.where(rid == 2, c, one)))
            ones_lo = jnp.where(rid < 6, 1.0, 0.0)
            fq_rows.append(sel(f1, f2, f3, ones_lo))
            fk_rows.append(jnp.where(rid < 3, 1.0, jnp.where(rid == 3, -f1, jnp.where(rid == 4, -f2,
                           jnp.where(rid == 5, -f3, 0.0)))))

    for kind, col, nheads, head0, gain, use_rope, out in plan:
        width = -(-nheads * HEAD_DIM // LANE) * LANE
        segT = y[:, col:col + width].T
        o_ref = outs[out]
        for hh in range(nheads):
            r = segT[hh * HEAD_DIM:(hh + 1) * HEAD_DIM]
            if gain is not None:
                r = r * lax.rsqrt(jnp.mean(r * r, axis=0, keepdims=True) + RMS_EPS) * _lanes(gn_ref, gain, tm)
            if use_rope:
                x1, x2 = r[:half], r[half:ROT_DIM]
                r = jnp.concatenate([x1 * cos - x2 * sin, x2 * cos + x1 * sin, r[ROT_DIM:]], axis=0)
            h = head0 + hh
            if kind == "v":
                o_ref[h] = jnp.concatenate([r, jnp.ones((VROWS - HEAD_DIM, tm), F32)], axis=0).astype(BF16)
                continue
            if kind == "q":
                r = r * QSCALE
            extra = None
            if fox and out == 0:
                extra = fq_rows[h]
            elif fox and out == 1:
                extra = fk_rows[h]
            if extra is None:
                padded = jnp.concatenate([r, zeros_pad], axis=0)
            else:
                padded = jnp.concatenate([r, extra, zeros_pad[SUBLANE:]], axis=0)
            if kind == "q":
                o_ref[h] = padded.astype(BF16)
            else:
                o_ref[h] = padded.T.astype(BF16)
                if kmean:
                    lane = lax.broadcasted_iota(jnp.int32, (HEAD_DIM, LANE), 1)
                    tile = jnp.zeros((HEAD_DIM, LANE), F32)
                    for b in range(tm // MOBA_BLOCK):
                        mb = jnp.sum(r[:, b * MOBA_BLOCK:(b + 1) * MOBA_BLOCK], axis=1, keepdims=True)
                        tile = jnp.where(lane == b, mb * (1.0 / MOBA_BLOCK), tile)
                    km_ref[0, h] = tile


def _heads_proj(x, g, w, plan, out_heads, tm, gains=None, rope=None, b_forget=None, kmean=False):
    S, D = x.shape
    N = w.shape[1]
    fox = b_forget is not None
    in_specs = [pl.BlockSpec((tm, D), lambda i: (i, 0)), _resident((1, D)), _resident((D, N))]
    args = [x, g.reshape(1, D), w]
    if gains is not None:
        gt = jnp.broadcast_to(gains[:, :, None], gains.shape + (LANE,)).astype(F32)
        in_specs.append(_resident(gt.shape))
        args.append(gt)
    if rope is not None:
        in_specs += [pl.BlockSpec((ROT_DIM // 2, tm), lambda i: (0, i))] * 2
        args += list(rope)
    if fox:
        in_specs.append(_resident((FOX_HEADS, LANE)))
        args.append(jnp.broadcast_to(b_forget[:, None], (FOX_HEADS, LANE)).astype(F32))
    out_specs, out_shapes = [], []
    for kind, nh in out_heads:
        if kind == "q":
            out_specs.append(pl.BlockSpec((nh, KPAD, tm), lambda i: (0, 0, i)))
            out_shapes.append(jax.ShapeDtypeStruct((nh, KPAD, S), BF16))
        elif kind == "k":
            out_specs.append(pl.BlockSpec((nh, tm, KPAD), lambda i: (0, i, 0)))
            out_shapes.append(jax.ShapeDtypeStruct((nh, S, KPAD), BF16))
        else:
            out_specs.append(pl.BlockSpec((nh, VROWS, tm), lambda i: (0, 0, i)))
            out_shapes.append(jax.ShapeDtypeStruct((nh, VROWS, S), BF16))
    if fox:
        out_specs.append(pl.BlockSpec((IDX_HEADS, tm), lambda i: (0, i)))
        out_shapes.append(jax.ShapeDtypeStruct((IDX_HEADS, S), F32))
    if kmean:
        nhk = out_heads[1][1]
        out_specs.append(pl.BlockSpec((1, nhk, HEAD_DIM, LANE), lambda i: (i, 0, 0, 0)))
        out_shapes.append(jax.ShapeDtypeStruct((S // tm, nhk, HEAD_DIM, LANE), F32))
    return pl.pallas_call(
        functools.partial(_heads_kernel, plan=tuple(plan), tm=tm, n_gain=0 if gains is None else gains.shape[0],
                          rope=rope is not None, fox=fox, kmean=kmean),
        grid=(S // tm,),
        in_specs=in_specs,
        out_specs=out_specs,
        out_shape=out_shapes,
        scratch_shapes=[pltpu.VMEM((FOX_HEADS, 1), F32)] if fox else [],
        compiler_params=_cparams(),
        name="heads_proj",
    )(*args)


def _heads_update(nh, qT_ref, k_ref, vT_ref, s_ref, p_ref, m_ref, acc_ref, biased):
    _, tk, tq = s_ref.shape
    rc = min(ATTN_ROWS, tk)
    chunks = range(0, tk, rc)
    m_new = alpha = None
    for t in range(nh + 1):
        mx = None
        if t < nh:
            qT = qT_ref[t]
        if t >= 1:
            hb, m_b, alpha_b = t - 1, m_new, alpha
        for r0 in chunks:
            if t < nh:
                s = biased(jnp.dot(k_ref[t, r0:r0 + rc, :], qT, preferred_element_type=F32), t, r0)
                s_ref[t % 2, r0:r0 + rc, :] = s
                cm = jnp.max(s.reshape(rc // SUBLANE, SUBLANE, tq), axis=0)
                mx = cm if mx is None else jnp.maximum(mx, cm)
            if t >= 1:
                p_ref[hb % 2, r0:r0 + rc, :] = jnp.exp2(s_ref[hb % 2, r0:r0 + rc, :] - m_b).astype(BF16)
        if t >= 1:
            acc_ref[hb] = alpha_b * acc_ref[hb] + jnp.dot(vT_ref[hb], p_ref[hb % 2], preferred_element_type=F32)
        if t < nh:
            m_prev = m_ref[t]
            m_new = jnp.maximum(m_prev, jnp.max(mx, axis=0, keepdims=True))
            alpha = jnp.exp2(m_prev - m_new)
            m_ref[t] = m_new


def _attn_init(m_ref, acc_ref):
    m_ref[...] = jnp.full(m_ref.shape, NEG, F32)
    acc_ref[...] = jnp.zeros(acc_ref.shape, F32)


def _attn_finish(o_ref, acc_ref, nh):
    hd = HEAD_DIM
    out = lambda h: acc_ref[h, :hd, :] / acc_ref[h, hd:hd + 1, :]
    for h in range(0, nh, 2):
        o_ref[:, h * hd:(h + 2) * hd] = jnp.concatenate([out(h), out(h + 1)], axis=0).T


def _attn_kernel(qi_ref, kj_ref, *refs, mode, nh, tq, tk):
    if mode == "moba":
        qT_ref, k_ref, vT_ref, sb_ref, o_ref, m_ref, acc_ref, s_ref, p_ref, cb_ref = refs
    elif mode == "causal":
        qT_ref, k_ref, vT_ref, o_ref, m_ref, acc_ref, s_ref, p_ref, cb_ref = refs
    else:
        qT_ref, k_ref, vT_ref, o_ref, m_ref, acc_ref, s_ref, p_ref = refs
    p = pl.program_id(0)
    i = qi_ref[p]
    j = kj_ref[p]
    causal = mode != "none"

    if causal:
        @pl.when(p == 0)
        def _():
            row = lax.broadcasted_iota(jnp.int32, (tk, tq), 0)
            col = lax.broadcasted_iota(jnp.int32, (tk, tq), 1)
            cb_ref[...] = jnp.where(row <= col, 0.0, NEG)

    @pl.when(j == 0)
    def _():
        _attn_init(m_ref, acc_ref)

    def process(diag):
        def biased(s, h, r0):
            if mode == "moba":
                s = s + sb_ref[h, pl.ds(j * (tk // MOBA_BLOCK) + r0 // MOBA_BLOCK, 1), :]
            if diag:
                s = s + cb_ref[r0:r0 + s.shape[0], :]
            return s

        _heads_update(nh, qT_ref, k_ref, vT_ref, s_ref, p_ref, m_ref, acc_ref, biased)

    if causal:
        pl.when(j == i)(lambda: process(True))
        pl.when(j != i)(lambda: process(False))
        last = j == i
    else:
        process(False)
        last = j == 0

    @pl.when(last)
    def _():
        _attn_finish(o_ref, acc_ref, nh)


def _attention(mode, qT, k, vT, extra, tq, tk):
    nh, _, S = qT.shape
    hd = HEAD_DIM
    nq = S // tq
    if mode == "none":
        pairs = [(i, 0) for i in range(nq)]
    else:
        assert tq == tk
        pairs = [(i, j) for i in range(nq) for j in range(i + 1)]
    qi = jnp.asarray(np.array([a for a, _ in pairs], np.int32))
    kj = jnp.asarray(np.array([b for _, b in pairs], np.int32))
    in_specs = [pl.BlockSpec((nh, KPAD, tq), lambda p, qi, kj: (0, 0, qi[p])),
                pl.BlockSpec((nh, tk, KPAD), lambda p, qi, kj: (0, kj[p], 0)),
                pl.BlockSpec((nh, VROWS, tk), lambda p, qi, kj: (0, 0, kj[p]))]
    if mode == "moba":
        nb = extra[0].shape[1]
        in_specs += [pl.BlockSpec((nh, nb, tq), lambda p, qi, kj: (0, 0, qi[p]))]
    scratch = [pltpu.VMEM((nh, 1, tq), F32), pltpu.VMEM((nh, VROWS, tq), F32),
               pltpu.VMEM((2, tk, tq), F32), pltpu.VMEM((2, tk, tq), BF16)]
    if mode != "none":
        scratch.append(pltpu.VMEM((tk, tq), F32))
    return pl.pallas_call(
        functools.partial(_attn_kernel, mode=mode, nh=nh, tq=tq, tk=tk),
        grid_spec=pltpu.PrefetchScalarGridSpec(
            num_scalar_prefetch=2,
            grid=(len(pairs),),
            in_specs=in_specs,
            out_specs=pl.BlockSpec((tq, nh * hd), lambda p, qi, kj: (qi[p], 0)),
            scratch_shapes=scratch),
        out_shape=jax.ShapeDtypeStruct((S, nh * hd), F32),
        compiler_params=_cparams(),
        name="attn_" + mode,
    )(qi, kj, qT, k, vT, *extra)


def _gate_kernel(qT_ref, km_ref, o_ref, *, nh, nb, tq):
    i = pl.program_id(0)
    n = lax.broadcasted_iota(jnp.int32, (nb, tq), 0)
    own = (i * tq + lax.broadcasted_iota(jnp.int32, (nb, tq), 1)) // MOBA_BLOCK
    ninf = jnp.float32(-jnp.inf)

    def body(h, carry):
        g = jnp.dot(km_ref[h], qT_ref[h], preferred_element_type=F32)
        g = jnp.where(n < own, g, ninf)
        keep = n == own
        for _ in range(MOBA_TOPK):
            mx = jnp.max(g, axis=0, keepdims=True)
            first = jnp.min(jnp.where((g == mx) & (mx > ninf), n, nb), axis=0, keepdims=True)
            pick = n == first
            keep = keep | pick
            g = jnp.where(pick, ninf, g)
        o_ref[h] = jnp.where(keep, 0.0, NEG)
        return carry

    lax.fori_loop(0, nh, body, 0)


def _moba_gate(qT, kmean, tq):
    nh, _, S = qT.shape
    nb = kmean.shape[1]
    return pl.pallas_call(
        functools.partial(_gate_kernel, nh=nh, nb=nb, tq=tq),
        grid=(S // tq,),
        in_specs=[pl.BlockSpec((nh, KPAD, tq), lambda i: (0, 0, i)), _resident((nh, nb, KPAD))],
        out_specs=pl.BlockSpec((nh, nb, tq), lambda i: (0, 0, i)),
        out_shape=jax.ShapeDtypeStruct((nh, nb, S), F32),
        compiler_params=_cparams(),
        name="moba_gate",
    )(qT, kmean)


def _dsa_kernel(qi_ref, kj_ref, ph_ref, ikb_ref, kvb_ref, nk_ref,
                iqT_ref, ik_ref, w_ref, qT_ref, k_ref, vT_ref, o_ref,
                keys_ref, thr_ref, cut_ref, m_ref, acc_ref, s_ref, p_ref, bias_ref,
                *, nih, nh, tq, tk, topk, seq, chunk):
    p = pl.program_id(0)
    i = qi_ref[p]
    j = kj_ref[p]
    phase = ph_ref[p]
    nkb = nk_ref[p]

    @pl.when(phase == 0)
    def _scores():
        ik = ik_ref[...]
        sc = jnp.zeros((tk, tq), F32)
        for h in range(nih):
            d = jnp.dot(ik, iqT_ref[h], preferred_element_type=F32)
            sc = sc + w_ref[h:h + 1, :] * jnp.maximum(d, 0.0)
        b = lax.bitcast_convert_type(sc, jnp.int32)
        key = b ^ ((b >> 31) & 0x7FFFFFFF)
        row = j * tk + lax.broadcasted_iota(jnp.int32, (tk, tq), 0)
        col = i * tq + lax.broadcasted_iota(jnp.int32, (tk, tq), 1)
        key = jnp.where(row <= col, key, INT_MIN)
        keys_ref[pl.ds(pl.multiple_of(j * tk, tk), tk), :] = key

    def count(hit):
        def body(c, acc):
            r0 = pl.multiple_of(c * chunk, chunk)
            return acc + jnp.sum(hit(keys_ref[pl.ds(r0, chunk), :], r0).reshape(chunk // SUBLANE, SUBLANE, tq), axis=0)
        acc = lax.fori_loop(0, nkb * (tk // chunk), body, jnp.zeros((SUBLANE, tq), jnp.int32))
        return jnp.sum(acc, axis=0, keepdims=True)

    @pl.when((phase == 1) & (j == 0))
    def _select():
        c0 = count(lambda blk, r0: jnp.where(blk >= 0, 1, 0))
        thr = jnp.where(c0 >= topk, 0, INT_MIN)

        def bit_body(it, thr):
            cand = thr | jnp.left_shift(jnp.int32(1), 30 - it)
            cnt = count(lambda blk, r0: jnp.where(blk >= cand, 1, 0))
            return jnp.where(cnt >= topk, cand, thr)

        thr = lax.fori_loop(0, 31, bit_body, thr)
        thr = jnp.maximum(thr, INT_MIN + 1)
        thr_ref[...] = thr
        cut_ref[...] = jnp.full((1, tq), seq, jnp.int32)
        cge = count(lambda blk, r0: jnp.where(blk >= thr, 1, 0))

        @pl.when(jnp.max(cge) > topk)
        def _ties():
            need = topk - count(lambda blk, r0: jnp.where(blk > thr, 1, 0))
            nbits = max(1, int(seq - 1).bit_length())

            def pos_body(it, x):
                cand = x + jnp.left_shift(jnp.int32(1), nbits - 1 - it)
                pos = lambda r0: r0 + lax.broadcasted_iota(jnp.int32, (chunk, tq), 0)
                c = count(lambda blk, r0: jnp.where(blk == thr, jnp.where(pos(r0) < cand, 1, 0), 0))
                return jnp.where(c < need, cand, x)

            x = lax.fori_loop(0, nbits, pos_body, jnp.zeros((1, tq), jnp.int32))
            cut_ref[...] = jnp.where(cge > topk, x, seq)

        _attn_init(m_ref, acc_ref)

    @pl.when(phase == 1)
    def _attend():
        key = keys_ref[pl.ds(pl.multiple_of(j * tk, tk), tk), :]
        thr = thr_ref[...]
        pos = j * tk + lax.broadcasted_iota(jnp.int32, (tk, tq), 0)
        tie = jnp.where(key == thr, jnp.where(pos <= cut_ref[...], 0.0, NEG), NEG)
        bias_ref[...] = jnp.where(key > thr, 0.0, tie)

        def biased(s, h, r0):
            return s + bias_ref[r0:r0 + s.shape[0], :]

        _heads_update(nh, qT_ref, k_ref, vT_ref, s_ref, p_ref, m_ref, acc_ref, biased)

        @pl.when(j == nkb - 1)
        def _():
            _attn_finish(o_ref, acc_ref, nh)


def _dsa_attention(iqT, ik, wT, qT, k, vT, tq, tk, topk):
    nh, _, S = qT.shape
    hd = HEAD_DIM
    nih = iqT.shape[0]
    steps = []
    for i in range(S // tq):
        nkb = (i * tq + tq - 1) // tk + 1
        steps += [(i, j, 0, j, 0, nkb) for j in range(nkb)]
        steps += [(i, j, 1, nkb - 1, j, nkb) for j in range(nkb)]
    tab = [jnp.asarray(np.array([s[c] for s in steps], np.int32)) for c in range(6)]
    chunk = min(tk, 512)
    qmap = lambda p, qi, kj, ph, ikb, kvb, nk: (0, 0, qi[p])
    return pl.pallas_call(
        functools.partial(_dsa_kernel, nih=nih, nh=nh, tq=tq, tk=tk, topk=topk, seq=S, chunk=chunk),
        grid_spec=pltpu.PrefetchScalarGridSpec(
            num_scalar_prefetch=6,
            grid=(len(steps),),
            in_specs=[pl.BlockSpec((nih, KPAD, tq), qmap),
                      pl.BlockSpec((tk, KPAD), lambda p, qi, kj, ph, ikb, kvb, nk: (ikb[p], 0)),
                      pl.BlockSpec((nih, tq), lambda p, qi, kj, ph, ikb, kvb, nk: (0, qi[p])),
                      pl.BlockSpec((nh, KPAD, tq), qmap),
                      pl.BlockSpec((nh, tk, KPAD), lambda p, qi, kj, ph, ikb, kvb, nk: (0, kvb[p], 0)),
                      pl.BlockSpec((nh, VROWS, tk), lambda p, qi, kj, ph, ikb, kvb, nk: (0, 0, kvb[p]))],
            out_specs=pl.BlockSpec((tq, nh * hd), lambda p, qi, kj, ph, ikb, kvb, nk: (qi[p], 0)),
            scratch_shapes=[pltpu.VMEM((S, tq), jnp.int32),
                            pltpu.VMEM((1, tq), jnp.int32), pltpu.VMEM((1, tq), jnp.int32),
                            pltpu.VMEM((nh, 1, tq), F32), pltpu.VMEM((nh, VROWS, tq), F32),
                            pltpu.VMEM((2, tk, tq), F32), pltpu.VMEM((2, tk, tq), BF16),
                            pltpu.VMEM((tk, tq), F32)]),
        out_shape=jax.ShapeDtypeStruct((S, nh * hd), F32),
        compiler_params=_cparams(),
        name="dsa",
    )(*tab, iqT, ik, wT, qT, k, vT)


def kernel(x, mem, ffn1_norm, ffn1_w_in, ffn1_w_out, mix_norm, even_w_in, even_b_forget, fox_q_norm, fox_k_norm, dsa_q_norm, dsa_k_norm, odd_w_in, moba_q_norm, moba_k_norm, mix_w_out, mem_norm_x, mem_norm_m, mem_wq, mem_wkv, mem_q_norm, mem_k_norm, mem_wo, ffn2_norm, ffn2_w_in, ffn2_w_out):
    _, S, D = x.shape
    M = mem.shape[1]
    depth = ffn1_norm.shape[0]
    tm = min(512, S)
    ta = min(512, S)
    tdq = min(512, S)
    pos = jnp.arange(S, dtype=F32)
    inv = ROPE_THETA ** (-jnp.arange(0, ROT_DIM, 2, dtype=F32) / ROT_DIM)
    ang = inv[:, None] * pos[None, :]
    rope = (jnp.cos(ang), jnp.sin(ang))
    W8 = 8 * HEAD_DIM
    zcols = lambda n: jnp.zeros((D, n), F32)

    h = x[0]
    m0 = mem[0]
    for layer in range(depth):
        h = _ffn(h, ffn1_norm[layer], ffn1_w_in[layer].astype(BF16), ffn1_w_out[layer].astype(BF16), tm)
        if layer % 2 == 0:
            e = layer // 2
            o = np.cumsum([0, W8, W8, W8, FOX_HEADS, W8, W8, W8, IDX_HEADS * IDX_DIM, IDX_DIM, IDX_HEADS])
            w = even_w_in[e]
            seg = lambda n: w[:, o[n]:o[n + 1]]
            w = jnp.concatenate([seg(0), seg(1), seg(2), seg(4), seg(5), seg(6), seg(7),
                                 seg(8), zcols(KPAD - IDX_DIM),
                                 seg(3), seg(9), zcols(LANE - FOX_HEADS - IDX_HEADS)], axis=1).astype(BF16)
            plan = [("q", 0, 8, 0, 0, False, 0), ("k", W8, 8, 0, 1, False, 1), ("v", 2 * W8, 8, 0, None, False, 2),
                    ("q", 3 * W8, 8, 0, 2, True, 3), ("k", 4 * W8, 8, 0, 3, True, 4), ("v", 5 * W8, 8, 0, None, False, 5),
                    ("q", 6 * W8, 8, 0, None, True, 6), ("k", 7 * W8, 1, 0, None, True, 7)]
            gains = jnp.stack([fox_q_norm[e], fox_k_norm[e], dsa_q_norm[e], dsa_k_norm[e]])
            fqT, fk, fvT, dqT, dk, dvT, iqT, ik, wT = _heads_proj(
                h, mix_norm[layer], w, plan,
                [("q", 8), ("k", 8), ("v", 8), ("q", 8), ("k", 8), ("v", 8), ("q", 8), ("k", 1)],
                tm, gains=gains, rope=rope, b_forget=even_b_forget[e])
            fox = _attention("causal", fqT, fk, fvT, (), ta, ta)
            dsa = _dsa_attention(iqT, ik[0], wT, dqT, dk, dvT, tdq, ta, min(DSA_TOPK, S // 4))
            wo = mix_w_out[layer].astype(BF16)
            h = _outproj(h, [fox, dsa], [wo[:W8], wo[W8:]], tm)
        else:
            od = layer // 2
            w = odd_w_in[od].astype(BF16)
            plan = [("q", 0, 8, 0, 0, True, 0), ("q", W8, 8, 8, 0, True, 0),
                    ("k", 2 * W8, 8, 0, 1, True, 1), ("k", 3 * W8, 8, 8, 1, True, 1),
                    ("v", 4 * W8, 8, 0, None, False, 2), ("v", 5 * W8, 8, 8, None, False, 2)]
            gains = jnp.stack([moba_q_norm[od], moba_k_norm[od]])
            qT, k, vT, km = _heads_proj(h, mix_norm[layer], w, plan, [("q", 16), ("k", 16), ("v", 16)],
                                        tm, gains=gains, rope=rope, kmean=True)
            nbt = tm // MOBA_BLOCK
            km = jnp.transpose(km[..., :nbt], (1, 0, 3, 2)).reshape(MOBA_HEADS, S // MOBA_BLOCK, HEAD_DIM)
            km = jnp.pad(km, ((0, 0), (0, 0), (0, KPAD - HEAD_DIM))).astype(BF16)
            selb = _moba_gate(qT, km, ta)
            moba = _attention("moba", qT, k, vT, (selb,), ta, ta)
            h = _outproj(h, [moba], [mix_w_out[layer].astype(BF16)], tm)
        mw = MEM_HEADS * HEAD_DIM
        (mqT,) = _heads_proj(h, mem_norm_x[layer], mem_wq[layer].astype(BF16),
                             [("q", 0, MEM_HEADS, 0, 0, False, 0)], [("q", MEM_HEADS)], tm,
                             gains=mem_q_norm[layer][None])
        mk, mvT = _heads_proj(m0, mem_norm_m[layer], mem_wkv[layer].astype(BF16),
                              [("k", 0, MEM_HEADS, 0, 0, False, 0), ("v", mw, MEM_HEADS, 0, None, False, 1)],
                              [("k", MEM_HEADS), ("v", MEM_HEADS)], M, gains=mem_k_norm[layer][None])
        ma = _attention("none", mqT, mk, mvT, (), ta, M)
        h = _outproj(h, [ma], [mem_wo[layer].astype(BF16)], tm)
        h = _ffn(h, ffn2_norm[layer], ffn2_w_in[layer].astype(BF16), ffn2_w_out[layer].astype(BF16), tm)
    return h[None]
```

```python
import functools

import numpy as np
import jax
import jax.numpy as jnp
from jax import lax
from jax.experimental import pallas as pl
from jax.experimental.pallas import tpu as pltpu

HEAD_DIM = 64
ROT_DIM = HEAD_DIM // 4
ROPE_THETA = 500000.0
FOX_HEADS = 8
DSA_HEADS = 8
MOBA_HEADS = 16
IDX_HEADS = 8
IDX_DIM = 64
DSA_TOPK = 256
MOBA_BLOCK = 256
MOBA_TOPK = 3
MEM_HEADS = 4
RMS_EPS = 1e-6

NEG = -1e30
INT_MIN = -(2 ** 31)
LANE = 128
SUBLANE = 8
KPAD = 128
VROWS = HEAD_DIM + 16
ATTN_ROWS = 256
VMEM_LIMIT = 56 * 1024 * 1024
LOG2E = 1.4426950408889634
QSCALE = HEAD_DIM ** -0.5 * LOG2E

F32 = jnp.float32
BF16 = jnp.bfloat16


def _cparams(n_axes=1):
    return pltpu.CompilerParams(dimension_semantics=("arbitrary",) * n_axes, vmem_limit_bytes=VMEM_LIMIT)


def _resident(shape):
    nd = len(shape)
    return pl.BlockSpec(shape, lambda *_: (0,) * nd, pipeline_mode=pl.Buffered(1))


def _rms_bf16(x, g):
    ms = jnp.mean(x * x, axis=-1, keepdims=True)
    return (x * lax.rsqrt(ms + RMS_EPS) * g).astype(BF16)


def _ffn_kernel(x_ref, g_ref, wa_ref, wb_ref, wo_ref, o_ref, *, fc):
    x = x_ref[...]
    xn = _rms_bf16(x, g_ref[...])
    acc = jnp.zeros(x.shape, F32)
    for c in range(wa_ref.shape[1] // fc):
        a = jnp.dot(xn, wa_ref[:, c * fc:(c + 1) * fc], preferred_element_type=F32)
        b = jnp.dot(xn, wb_ref[:, c * fc:(c + 1) * fc], preferred_element_type=F32)
        gate = (a * jax.nn.sigmoid(a) * b).astype(BF16)
        acc = acc + jnp.dot(gate, wo_ref[c * fc:(c + 1) * fc, :], preferred_element_type=F32)
    o_ref[...] = x + 0.5 * acc


def _ffn(h, g, w_in, w_out, tm):
    S, D = h.shape
    Fh = w_out.shape[0]
    return pl.pallas_call(
        functools.partial(_ffn_kernel, fc=256),
        grid=(S // tm,),
        in_specs=[pl.BlockSpec((tm, D), lambda i: (i, 0)),
                  _resident((1, D)),
                  pl.BlockSpec((D, Fh), lambda i: (0, 0), pipeline_mode=pl.Buffered(1)),
                  pl.BlockSpec((D, Fh), lambda i: (0, 1), pipeline_mode=pl.Buffered(1)),
                  _resident((Fh, D))],
        out_specs=pl.BlockSpec((tm, D), lambda i: (i, 0)),
        out_shape=jax.ShapeDtypeStruct((S, D), F32),
        compiler_params=_cparams(),
        name="ffn",
    )(h, g.reshape(1, D), w_in, w_in, w_out)


def _outproj_kernel(*refs, n):
    r_ref, o_ref = refs[0], refs[-1]
    acc = r_ref[...]
    for k in range(n):
        acc = acc + jnp.dot(refs[1 + k][...].astype(BF16), refs[1 + n + k][...], preferred_element_type=F32)
    o_ref[...] = acc


def _outproj(res, xs, ws, tm):
    S, D = res.shape
    n = len(xs)
    in_specs = [pl.BlockSpec((tm, D), lambda i: (i, 0))]
    in_specs += [pl.BlockSpec((tm, x.shape[1]), lambda i: (i, 0)) for x in xs]
    in_specs += [_resident(w.shape) for w in ws]
    return pl.pallas_call(
        functools.partial(_outproj_kernel, n=n),
        grid=(S // tm,),
        in_specs=in_specs,
        out_specs=pl.BlockSpec((tm, D), lambda i: (i, 0)),
        out_shape=jax.ShapeDtypeStruct((S, D), F32),
        compiler_params=_cparams(),
        name="outproj",
    )(res, *xs, *ws)


def _lanes(ref, idx, tm):
    return jnp.concatenate([ref[idx]] * (tm // LANE), axis=1)


def _bf16_split3(x):
    a = x.astype(BF16).astype(F32)
    b = (x - a).astype(BF16).astype(F32)
    return a, b, x - a - b


def _heads_kernel(*refs, plan, tm, n_gain, rope, fox, kmean):
    it = iter(refs)
    x_ref, g_ref, w_ref = next(it), next(it), next(it)
    gn_ref = next(it) if n_gain else None
    cos = sin = None
    if rope:
        cos, sin = next(it)[...], next(it)[...]
    bf_ref = next(it) if fox else None
    n_out = 1 + max(p[6] for p in plan)
    outs = [next(it) for _ in range(n_out)]
    wt_ref = next(it) if fox else None
    km_ref = next(it) if kmean else None
    carry_ref = next(it) if fox else None

    y = jnp.dot(_rms_bf16(x_ref[...], g_ref[...]), w_ref[...], preferred_element_type=F32)
    zeros_pad = jnp.zeros((KPAD - HEAD_DIM, tm), F32)
    half = ROT_DIM // 2

    fq_rows = fk_rows = None
    if fox:
        @pl.when(pl.program_id(0) == 0)
        def _():
            carry_ref[...] = jnp.zeros(carry_ref.shape, F32)

        gT = y[:, y.shape[1] - LANE:].T
        xg = gT[0:FOX_HEADS] + _lanes(bf_ref, slice(None), tm)
        lf = jnp.minimum(xg, 0.0) - jnp.log1p(jnp.exp(-jnp.abs(xg)))
        tri = jnp.where(lax.broadcasted_iota(jnp.int32, (tm, tm), 0)
                        <= lax.broadcasted_iota(jnp.int32, (tm, tm), 1), 1.0, 0.0).astype(BF16)
        cs = sum(jnp.dot(part.astype(BF16), tri, preferred_element_type=F32) for part in _bf16_split3(lf))
        fcum = (carry_ref[...] + cs) * LOG2E
        carry_ref[...] = carry_ref[...] + jnp.sum(lf, axis=1, keepdims=True)
        wt_ref[...] = gT[FOX_HEADS:FOX_HEADS + IDX_HEADS] * (IDX_HEADS ** -0.5 * IDX_DIM ** -0.5)
        rid = lax.broadcasted_iota(jnp.int32, (SUBLANE, tm), 0)
        fq_rows, fk_rows = [], []
        for h in range(FOX_HEADS):
            f1, f2, f3 = _bf16_split3(fcum[h:h + 1])
            sel = lambda a, b, c, one: jnp.where(rid == 0, a, jnp.where(rid == 1, b, jnp.where(rid == 2, c, one)))
            ones_lo = jnp.where(rid < 6, 1.0, 0.0)
            fq_rows.append(sel(f1, f2, f3, ones_lo))
            fk_rows.append(jnp.where(rid < 3, 1.0, jnp.where(rid == 3, -f1, jnp.where(rid == 4, -f2,
                           jnp.where(rid == 5, -f3, 0.0)))))

    for kind, col, nheads, head0, gain, use_rope, out in plan:
        width = -(-nheads * HEAD_DIM // LANE) * LANE
        segT = y[:, col:col + width].T
        o_ref = outs[out]
        for hh in range(nheads):
            r = segT[hh * HEAD_DIM:(hh + 1) * HEAD_DIM]
            if gain is not None:
                r = r * lax.rsqrt(jnp.mean(r * r, axis=0, keepdims=True) + RMS_EPS) * _lanes(gn_ref, gain, tm)
            if use_rope:
                x1, x2 = r[:half], r[half:ROT_DIM]
                r = jnp.concatenate([x1 * cos - x2 * sin, x2 * cos + x1 * sin, r[ROT_DIM:]], axis=0)
            h = head0 + hh
            if kind == "v":
                o_ref[h] = jnp.concatenate([r, jnp.ones((VROWS - HEAD_DIM, tm), F32)], axis=0).astype(BF16)
                continue
            if kind == "q":
                r = r * QSCALE
            extra = None
            if fox and out == 0:
                extra = fq_rows[h]
            elif fox and out == 1:
                extra = fk_rows[h]
            if extra is None:
                padded = jnp.concatenate([r, zeros_pad], axis=0)
            else:
                padded = jnp.concatenate([r, extra, zeros_pad[SUBLANE:]], axis=0)
            if kind == "q":
                o_ref[h] = padded.astype(BF16)
            else:
                o_ref[h] = padded.T.astype(BF16)
                if kmean:
                    lane = lax.broadcasted_iota(jnp.int32, (HEAD_DIM, LANE), 1)
                    tile = jnp.zeros((HEAD_DIM, LANE), F32)
                    for b in range(tm // MOBA_BLOCK):
                        mb = jnp.sum(r[:, b * MOBA_BLOCK:(b + 1) * MOBA_BLOCK], axis=1, keepdims=True)
                        tile = jnp.where(lane == b, mb * (1.0 / MOBA_BLOCK), tile)
                    km_ref[0, h] = tile


def _heads_proj(x, g, w, plan, out_heads, tm, gains=None, rope=None, b_forget=None, kmean=False):
    S, D = x.shape
    N = w.shape[1]
    fox = b_forget is not None
    in_specs = [pl.BlockSpec((tm, D), lambda i: (i, 0)), _resident((1, D)), _resident((D, N))]
    args = [x, g.reshape(1, D), w]
    if gains is not None:
        gt = jnp.broadcast_to(gains[:, :, None], gains.shape + (LANE,)).astype(F32)
        in_specs.append(_resident(gt.shape))
        args.append(gt)
    if rope is not None:
        in_specs += [pl.BlockSpec((ROT_DIM // 2, tm), lambda i: (0, i))] * 2
        args += list(rope)
    if fox:
        in_specs.append(_resident((FOX_HEADS, LANE)))
        args.append(jnp.broadcast_to(b_forget[:, None], (FOX_HEADS, LANE)).astype(F32))
    out_specs, out_shapes = [], []
    for kind, nh in out_heads:
        if kind == "q":
            out_specs.append(pl.BlockSpec((nh, KPAD, tm), lambda i: (0, 0, i)))
            out_shapes.append(jax.ShapeDtypeStruct((nh, KPAD, S), BF16))
        elif kind == "k":
            out_specs.append(pl.BlockSpec((nh, tm, KPAD), lambda i: (0, i, 0)))
            out_shapes.append(jax.ShapeDtypeStruct((nh, S, KPAD), BF16))
        else:
            out_specs.append(pl.BlockSpec((nh, VROWS, tm), lambda i: (0, 0, i)))
            out_shapes.append(jax.ShapeDtypeStruct((nh, VROWS, S), BF16))
    if fox:
        out_specs.append(pl.BlockSpec((IDX_HEADS, tm), lambda i: (0, i)))
        out_shapes.append(jax.ShapeDtypeStruct((IDX_HEADS, S), F32))
    if kmean:
        nhk = out_heads[1][1]
        out_specs.append(pl.BlockSpec((1, nhk, HEAD_DIM, LANE), lambda i: (i, 0, 0, 0)))
        out_shapes.append(jax.ShapeDtypeStruct((S // tm, nhk, HEAD_DIM, LANE), F32))
    return pl.pallas_call(
        functools.partial(_heads_kernel, plan=tuple(plan), tm=tm, n_gain=0 if gains is None else gains.shape[0],
                          rope=rope is not None, fox=fox, kmean=kmean),
        grid=(S // tm,),
        in_specs=in_specs,
        out_specs=out_specs,
        out_shape=out_shapes,
        scratch_shapes=[pltpu.VMEM((FOX_HEADS, 1), F32)] if fox else [],
        compiler_params=_cparams(),
        name="heads_proj",
    )(*args)


def _heads_update(nh, qT_ref, k_ref, vT_ref, s_ref, p_ref, m_ref, acc_ref, biased):
    _, tk, tq = s_ref.shape
    rc = min(ATTN_ROWS, tk)
    chunks = range(0, tk, rc)
    m_new = alpha = None
    for t in range(nh + 1):
        mx = None
        if t < nh:
            qT = qT_ref[t]
        if t >= 1:
            hb, m_b, alpha_b = t - 1, m_new, alpha
        for r0 in chunks:
            if t < nh:
                s = biased(jnp.dot(k_ref[t, r0:r0 + rc, :], qT, preferred_element_type=F32), t, r0)
                s_ref[t % 2, r0:r0 + rc, :] = s
                cm = jnp.max(s.reshape(rc // SUBLANE, SUBLANE, tq), axis=0)
                mx = cm if mx is None else jnp.maximum(mx, cm)
            if t >= 1:
                p_ref[hb % 2, r0:r0 + rc, :] = jnp.exp2(s_ref[hb % 2, r0:r0 + rc, :] - m_b).astype(BF16)
        if t >= 1:
            acc_ref[hb] = alpha_b * acc_ref[hb] + jnp.dot(vT_ref[hb], p_ref[hb % 2], preferred_element_type=F32)
        if t < nh:
            m_prev = m_ref[t]
            m_new = jnp.maximum(m_prev, jnp.max(mx, axis=0, keepdims=True))
            alpha = jnp.exp2(m_prev - m_new)
            m_ref[t] = m_new


def _attn_init(m_ref, acc_ref):
    m_ref[...] = jnp.full(m_ref.shape, NEG, F32)
    acc_ref[...] = jnp.zeros(acc_ref.shape, F32)


def _attn_finish(o_ref, acc_ref, nh):
    hd = HEAD_DIM
    out = lambda h: acc_ref[h, :hd, :] / acc_ref[h, hd:hd + 1, :]
    for h in range(0, nh, 2):
        o_ref[:, h * hd:(h + 2) * hd] = jnp.concatenate([out(h), out(h + 1)], axis=0).T


def _attn_kernel(qi_ref, kj_ref, *refs, mode, nh, tq, tk):
    if mode == "moba":
        qT_ref, k_ref, vT_ref, sb_ref, o_ref, m_ref, acc_ref, s_ref, p_ref, cb_ref = refs
    elif mode == "causal":
        qT_ref, k_ref, vT_ref, o_ref, m_ref, acc_ref, s_ref, p_ref, cb_ref = refs
    else:
        qT_ref, k_ref, vT_ref, o_ref, m_ref, acc_ref, s_ref, p_ref = refs
    p = pl.program_id(0)
    i = qi_ref[p]
    j = kj_ref[p]
    causal = mode != "none"

    if causal:
        @pl.when(p == 0)
        def _():
            row = lax.broadcasted_iota(jnp.int32, (tk, tq), 0)
            col = lax.broadcasted_iota(jnp.int32, (tk, tq), 1)
            cb_ref[...] = jnp.where(row <= col, 0.0, NEG)

    @pl.when(j == 0)
    def _():
        _attn_init(m_ref, acc_ref)

    def process(diag):
        def biased(s, h, r0):
            if mode == "moba":
                s = s + sb_ref[h, pl.ds(j * (tk // MOBA_BLOCK) + r0 // MOBA_BLOCK, 1), :]
            if diag:
                s = s + cb_ref[r0:r0 + s.shape[0], :]
            return s

        _heads_update(nh, qT_ref, k_ref, vT_ref, s_ref, p_ref, m_ref, acc_ref, biased)

    if causal:
        pl.when(j == i)(lambda: process(True))
        pl.when(j != i)(lambda: process(False))
        last = j == i
    else:
        process(False)
        last = j == 0

    @pl.when(last)
    def _():
        _attn_finish(o_ref, acc_ref, nh)


def _attention(mode, qT, k, vT, extra, tq, tk):
    nh, _, S = qT.shape
    hd = HEAD_DIM
    nq = S // tq
    if mode == "none":
        pairs = [(i, 0) for i in range(nq)]
    else:
        assert tq == tk
        pairs = [(i, j) for i in range(nq) for j in range(i + 1)]
    qi = jnp.asarray(np.array([a for a, _ in pairs], np.int32))
    kj = jnp.asarray(np.array([b for _, b in pairs], np.int32))
    in_specs = [pl.BlockSpec((nh, KPAD, tq), lambda p, qi, kj: (0, 0, qi[p])),
                pl.BlockSpec((nh, tk, KPAD), lambda p, qi, kj: (0, kj[p], 0)),
                pl.BlockSpec((nh, VROWS, tk), lambda p, qi, kj: (0, 0, kj[p]))]
    if mode == "moba":
        nb = extra[0].shape[1]
        in_specs += [pl.BlockSpec((nh, nb, tq), lambda p, qi, kj: (0, 0, qi[p]))]
    scratch = [pltpu.VMEM((nh, 1, tq), F32), pltpu.VMEM((nh, VROWS, tq), F32),
               pltpu.VMEM((2, tk, tq), F32), pltpu.VMEM((2, tk, tq), BF16)]
    if mode != "none":
        scratch.append(pltpu.VMEM((tk, tq), F32))
    return pl.pallas_call(
        functools.partial(_attn_kernel, mode=mode, nh=nh, tq=tq, tk=tk),
        grid_spec=pltpu.PrefetchScalarGridSpec(
            num_scalar_prefetch=2,
            grid=(len(pairs),),
            in_specs=in_specs,
            out_specs=pl.BlockSpec((tq, nh * hd), lambda p, qi, kj: (qi[p], 0)),
            scratch_shapes=scratch),
        out_shape=jax.ShapeDtypeStruct((S, nh * hd), F32),
        compiler_params=_cparams(),
        name="attn_" + mode,
    )(qi, kj, qT, k, vT, *extra)


def _gate_kernel(qT_ref, km_ref, o_ref, *, nh, nb, tq):
    i = pl.program_id(0)
    n = lax.broadcasted_iota(jnp.int32, (nb, tq), 0)
    own = (i * tq + lax.broadcasted_iota(jnp.int32, (nb, tq), 1)) // MOBA_BLOCK
    ninf = jnp.float32(-jnp.inf)

    def body(h, carry):
        g = jnp.dot(km_ref[h], qT_ref[h], preferred_element_type=F32)
        g = jnp.where(n < own, g, ninf)
        keep = n == own
        for _ in range(MOBA_TOPK):
            mx = jnp.max(g, axis=0, keepdims=True)
            first = jnp.min(jnp.where((g == mx) & (mx > ninf), n, nb), axis=0, keepdims=True)
            pick = n == first
            keep = keep | pick
            g = jnp.where(pick, ninf, g)
        o_ref[h] = jnp.where(keep, 0.0, NEG)
        return carry

    lax.fori_loop(0, nh, body, 0)


def _moba_gate(qT, kmean, tq):
    nh, _, S = qT.shape
    nb = kmean.shape[1]
    return pl.pallas_call(
        functools.partial(_gate_kernel, nh=nh, nb=nb, tq=tq),
        grid=(S // tq,),
        in_specs=[pl.BlockSpec((nh, KPAD, tq), lambda i: (0, 0, i)), _resident((nh, nb, KPAD))],
        out_specs=pl.BlockSpec((nh, nb, tq), lambda i: (0, 0, i)),
        out_shape=jax.ShapeDtypeStruct((nh, nb, S), F32),
        compiler_params=_cparams(),
        name="moba_gate",
    )(qT, kmean)


def _dsa_kernel(qi_ref, kj_ref, ph_ref, ikb_ref, kvb_ref, nk_ref,
                iqT_ref, ik_ref, w_ref, qT_ref, k_ref, vT_ref, o_ref,
                hi_ref, lo_ref, thi_ref, tlo_ref, cut_ref, m_ref, acc_ref, s_ref, p_ref, bias_ref,
                *, nih, nh, tq, tk, topk, seq, chunk):
    p = pl.program_id(0)
    i = qi_ref[p]
    j = kj_ref[p]
    phase = ph_ref[p]
    nkb = nk_ref[p]
    I16, I32 = jnp.int16, jnp.int32
    lo_min = -(2 ** 15)
    one, zero = jnp.ones((), I16), jnp.zeros((), I16)
    row16 = lambda x: x.astype(I16)

    @pl.when(phase == 0)
    def _scores():
        ik = ik_ref[...]
        sc = jnp.zeros((tk, tq), F32)
        for h in range(nih):
            d = jnp.dot(ik, iqT_ref[h], preferred_element_type=F32)
            sc = sc + w_ref[h:h + 1, :] * jnp.maximum(d, 0.0)
        b = lax.bitcast_convert_type(sc, I32)
        key = b ^ ((b >> 31) & 0x7FFFFFFF)
        row = j * tk + lax.broadcasted_iota(I32, (tk, tq), 0)
        col = i * tq + lax.broadcasted_iota(I32, (tk, tq), 1)
        key = jnp.where(row <= col, key, INT_MIN)
        rows = pl.ds(pl.multiple_of(j * tk, tk), tk)
        hi_ref[rows, :] = (key >> 16).astype(I16)
        lo_ref[rows, :] = ((key & 0xFFFF) + lo_min).astype(I16)

    def count(hit):
        def body(c, acc):
            r0 = pl.multiple_of(c * chunk, chunk)
            h = hit(hi_ref[pl.ds(r0, chunk), :], lo_ref[pl.ds(r0, chunk), :], r0)
            part = h[0:2 * SUBLANE]
            for r in range(2 * SUBLANE, chunk, 2 * SUBLANE):
                part = part + h[r:r + 2 * SUBLANE]
            return acc + part.astype(I32)
        acc = lax.fori_loop(0, nkb * (tk // chunk), body, jnp.zeros((2 * SUBLANE, tq), I32))
        return jnp.sum(acc, axis=0, keepdims=True)

    def kth_largest(pick, k):
        c0 = count(lambda hi, lo, r0: jnp.where(pick(hi, lo) >= zero, one, zero))
        t = jnp.where(c0 >= k, 0, lo_min)

        def bit_body(it, t):
            cand = t | jnp.left_shift(jnp.int32(1), 14 - it)
            c16 = row16(cand)
            cnt = count(lambda hi, lo, r0: jnp.where(pick(hi, lo) >= c16, one, zero))
            return jnp.where(cnt >= k, cand, t)

        return lax.fori_loop(0, 15, bit_body, t)

    @pl.when((phase == 1) & (j == 0))
    def _select():
        thi = kth_largest(lambda hi, lo: hi, topk)
        thi16 = row16(thi)
        above = count(lambda hi, lo, r0: jnp.where(hi > thi16, one, zero))
        need = topk - above

        def mark(c, carry):
            rows = pl.ds(pl.multiple_of(c * chunk, chunk), chunk)
            lo_ref[rows, :] = jnp.where(hi_ref[rows, :] == thi16, lo_ref[rows, :], jnp.full((), lo_min, I16))
            return carry

        lax.fori_loop(0, nkb * (tk // chunk), mark, 0)
        tlo = kth_largest(lambda hi, lo: lo, need)
        tlo = jnp.where((thi == lo_min) & (tlo == lo_min), lo_min + 1, tlo)
        tlo16 = row16(tlo)
        thi_ref[...] = thi
        tlo_ref[...] = tlo
        cut_ref[...] = jnp.full((1, tq), seq, I32)
        in_bucket = lambda hi, x: jnp.where(hi == thi16, x, zero)
        cge = above + count(lambda hi, lo, r0: in_bucket(hi, jnp.where(lo >= tlo16, one, zero)))

        @pl.when(jnp.max(cge) > topk)
        def _ties():
            need_tie = need - count(lambda hi, lo, r0: jnp.where(lo > tlo16, one, zero))
            nbits = max(1, int(seq - 1).bit_length())

            def pos_body(it, x):
                cand = x + jnp.left_shift(jnp.int32(1), nbits - 1 - it)
                before = lambda r0: jnp.where(r0 + lax.broadcasted_iota(I32, (chunk, tq), 0) < cand, 1, 0).astype(I16)
                c = count(lambda hi, lo, r0: in_bucket(hi, jnp.where(lo == tlo16, before(r0), zero)))
                return jnp.where(c < need_tie, cand, x)

            x = lax.fori_loop(0, nbits, pos_body, jnp.zeros((1, tq), I32))
            cut_ref[...] = jnp.where(cge > topk, x, seq)

        _attn_init(m_ref, acc_ref)

    @pl.when(phase == 1)
    def _attend():
        rows = pl.ds(pl.multiple_of(j * tk, tk), tk)
        hi, lo = hi_ref[rows, :], lo_ref[rows, :]
        thi16, tlo16 = row16(thi_ref[...]), row16(tlo_ref[...])
        kept = jnp.where(j * tk + lax.broadcasted_iota(I32, (tk, tq), 0) <= cut_ref[...], 1, 0).astype(I16)
        tie = jnp.where(lo == tlo16, kept, zero)
        sel = jnp.where(hi > thi16, one, jnp.where(hi == thi16, jnp.where(lo > tlo16, one, tie), zero))
        bias_ref[...] = jnp.where(sel.astype(I32) > 0, 0.0, NEG)

        def biased(s, h, r0):
            return s + bias_ref[r0:r0 + s.shape[0], :]

        _heads_update(nh, qT_ref, k_ref, vT_ref, s_ref, p_ref, m_ref, acc_ref, biased)

        @pl.when(j == nkb - 1)
        def _():
            _attn_finish(o_ref, acc_ref, nh)


def _dsa_attention(iqT, ik, wT, qT, k, vT, tq, tk, topk):
    nh, _, S = qT.shape
    hd = HEAD_DIM
    nih = iqT.shape[0]
    steps = []
    for i in range(S // tq):
        nkb = (i * tq + tq - 1) // tk + 1
        steps += [(i, j, 0, j, 0, nkb) for j in range(nkb)]
        steps += [(i, j, 1, nkb - 1, j, nkb) for j in range(nkb)]
    tab = [jnp.asarray(np.array([s[c] for s in steps], np.int32)) for c in range(6)]
    chunk = min(tk, 512)
    qmap = lambda p, qi, kj, ph, ikb, kvb, nk: (0, 0, qi[p])
    return pl.pallas_call(
        functools.partial(_dsa_kernel, nih=nih, nh=nh, tq=tq, tk=tk, topk=topk, seq=S, chunk=chunk),
        grid_spec=pltpu.PrefetchScalarGridSpec(
            num_scalar_prefetch=6,
            grid=(len(steps),),
            in_specs=[pl.BlockSpec((nih, KPAD, tq), qmap),
                      pl.BlockSpec((tk, KPAD), lambda p, qi, kj, ph, ikb, kvb, nk: (ikb[p], 0)),
                      pl.BlockSpec((nih, tq), lambda p, qi, kj, ph, ikb, kvb, nk: (0, qi[p])),
                      pl.BlockSpec((nh, KPAD, tq), qmap),
                      pl.BlockSpec((nh, tk, KPAD), lambda p, qi, kj, ph, ikb, kvb, nk: (0, kvb[p], 0)),
                      pl.BlockSpec((nh, VROWS, tk), lambda p, qi, kj, ph, ikb, kvb, nk: (0, 0, kvb[p]))],
            out_specs=pl.BlockSpec((tq, nh * hd), lambda p, qi, kj, ph, ikb, kvb, nk: (qi[p], 0)),
            scratch_shapes=[pltpu.VMEM((S, tq), jnp.int16), pltpu.VMEM((S, tq), jnp.int16),
                            pltpu.VMEM((1, tq), jnp.int32), pltpu.VMEM((1, tq), jnp.int32),
                            pltpu.VMEM((1, tq), jnp.int32),
                            pltpu.VMEM((nh, 1, tq), F32), pltpu.VMEM((nh, VROWS, tq), F32),
                            pltpu.VMEM((2, tk, tq), F32), pltpu.VMEM((2, tk, tq), BF16),
                            pltpu.VMEM((tk, tq), F32)]),
        out_shape=jax.ShapeDtypeStruct((S, nh * hd), F32),
        compiler_params=_cparams(),
        name="dsa",
    )(*tab, iqT, ik, wT, qT, k, vT)


def kernel(x, mem, ffn1_norm, ffn1_w_in, ffn1_w_out, mix_norm, even_w_in, even_b_forget, fox_q_norm, fox_k_norm, dsa_q_norm, dsa_k_norm, odd_w_in, moba_q_norm, moba_k_norm, mix_w_out, mem_norm_x, mem_norm_m, mem_wq, mem_wkv, mem_q_norm, mem_k_norm, mem_wo, ffn2_norm, ffn2_w_in, ffn2_w_out):
    _, S, D = x.shape
    M = mem.shape[1]
    depth = ffn1_norm.shape[0]
    tm = min(512, S)
    ta = min(512, S)
    tdq = min(512, S)
    pos = jnp.arange(S, dtype=F32)
    inv = ROPE_THETA ** (-jnp.arange(0, ROT_DIM, 2, dtype=F32) / ROT_DIM)
    ang = inv[:, None] * pos[None, :]
    rope = (jnp.cos(ang), jnp.sin(ang))
    W8 = 8 * HEAD_DIM
    zcols = lambda n: jnp.zeros((D, n), F32)

    h = x[0]
    m0 = mem[0]
    for layer in range(depth):
        h = _ffn(h, ffn1_norm[layer], ffn1_w_in[layer].astype(BF16), ffn1_w_out[layer].astype(BF16), tm)
        if layer % 2 == 0:
            e = layer // 2
            o = np.cumsum([0, W8, W8, W8, FOX_HEADS, W8, W8, W8, IDX_HEADS * IDX_DIM, IDX_DIM, IDX_HEADS])
            w = even_w_in[e]
            seg = lambda n: w[:, o[n]:o[n + 1]]
            w = jnp.concatenate([seg(0), seg(1), seg(2), seg(4), seg(5), seg(6), seg(7),
                                 seg(8), zcols(KPAD - IDX_DIM),
                                 seg(3), seg(9), zcols(LANE - FOX_HEADS - IDX_HEADS)], axis=1).astype(BF16)
            plan = [("q", 0, 8, 0, 0, False, 0), ("k", W8, 8, 0, 1, False, 1), ("v", 2 * W8, 8, 0, None, False, 2),
                    ("q", 3 * W8, 8, 0, 2, True, 3), ("k", 4 * W8, 8, 0, 3, True, 4), ("v", 5 * W8, 8, 0, None, False, 5),
                    ("q", 6 * W8, 8, 0, None, True, 6), ("k", 7 * W8, 1, 0, None, True, 7)]
            gains = jnp.stack([fox_q_norm[e], fox_k_norm[e], dsa_q_norm[e], dsa_k_norm[e]])
            fqT, fk, fvT, dqT, dk, dvT, iqT, ik, wT = _heads_proj(
                h, mix_norm[layer], w, plan,
                [("q", 8), ("k", 8), ("v", 8), ("q", 8), ("k", 8), ("v", 8), ("q", 8), ("k", 1)],
                tm, gains=gains, rope=rope, b_forget=even_b_forget[e])
            fox = _attention("causal", fqT, fk, fvT, (), ta, ta)
            dsa = _dsa_attention(iqT, ik[0], wT, dqT, dk, dvT, tdq, ta, min(DSA_TOPK, S // 4))
            wo = mix_w_out[layer].astype(BF16)
            h = _outproj(h, [fox, dsa], [wo[:W8], wo[W8:]], tm)
        else:
            od = layer // 2
            w = odd_w_in[od].astype(BF16)
            plan = [("q", 0, 8, 0, 0, True, 0), ("q", W8, 8, 8, 0, True, 0),
                    ("k", 2 * W8, 8, 0, 1, True, 1), ("k", 3 * W8, 8, 8, 1, True, 1),
                    ("v", 4 * W8, 8, 0, None, False, 2), ("v", 5 * W8, 8, 8, None, False, 2)]
            gains = jnp.stack([moba_q_norm[od], moba_k_norm[od]])
            qT, k, vT, km = _heads_proj(h, mix_norm[layer], w, plan, [("q", 16), ("k", 16), ("v", 16)],
                                        tm, gains=gains, rope=rope, kmean=True)
            nbt = tm // MOBA_BLOCK
            km = jnp.transpose(km[..., :nbt], (1, 0, 3, 2)).reshape(MOBA_HEADS, S // MOBA_BLOCK, HEAD_DIM)
            km = jnp.pad(km, ((0, 0), (0, 0), (0, KPAD - HEAD_DIM))).astype(BF16)
            selb = _moba_gate(qT, km, ta)
            moba = _attention("moba", qT, k, vT, (selb,), ta, ta)
            h = _outproj(h, [moba], [mix_w_out[layer].astype(BF16)], tm)
        mw = MEM_HEADS * HEAD_DIM
        (mqT,) = _heads_proj(h, mem_norm_x[layer], mem_wq[layer].astype(BF16),
                             [("q", 0, MEM_HEADS, 0, 0, False, 0)], [("q", MEM_HEADS)], tm,
                             gains=mem_q_norm[layer][None])
        mk, mvT = _heads_proj(m0, mem_norm_m[layer], mem_wkv[layer].astype(BF16),
                              [("k", 0, MEM_HEADS, 0, 0, False, 0), ("v", mw, MEM_HEADS, 0, None, False, 1)],
                              [("k", MEM_HEADS), ("v", MEM_HEADS)], M, gains=mem_k_norm[layer][None])
        ma = _attention("none", mqT, mk, mvT, (), ta, M)
        h = _outproj(h, [ma], [mem_wo[layer].astype(BF16)], tm)
        h = _ffn(h, ffn2_norm[layer], ffn2_w_in[layer].astype(BF16), ffn2_w_out[layer].astype(BF16), tm)
    return h[None]
```

```python
import functools

import numpy as np
import jax
import jax.numpy as jnp
from jax import lax
from jax.experimental import pallas as pl
from jax.experimental.pallas import tpu as pltpu

HEAD_DIM = 64
ROT_DIM = HEAD_DIM // 4
ROPE_THETA = 500000.0
FOX_HEADS = 8
DSA_HEADS = 8
MOBA_HEADS = 16
IDX_HEADS = 8
IDX_DIM = 64
DSA_TOPK = 256
MOBA_BLOCK = 256
MOBA_TOPK = 3
MEM_HEADS = 4
RMS_EPS = 1e-6

NEG = -1e30
INT_MIN = -(2 ** 31)
LANE = 128
SUBLANE = 8
KPAD = 128
VROWS = HEAD_DIM + 16
ATTN_ROWS = 256
VMEM_LIMIT = 56 * 1024 * 1024
LOG2E = 1.4426950408889634
QSCALE = HEAD_DIM ** -0.5 * LOG2E

F32 = jnp.float32
BF16 = jnp.bfloat16


def _cparams(n_axes=1):
    return pltpu.CompilerParams(dimension_semantics=("arbitrary",) * n_axes, vmem_limit_bytes=VMEM_LIMIT)


def _resident(shape):
    nd = len(shape)
    return pl.BlockSpec(shape, lambda *_: (0,) * nd, pipeline_mode=pl.Buffered(1))


def _rms_bf16(x, g):
    ms = jnp.mean(x * x, axis=-1, keepdims=True)
    return (x * lax.rsqrt(ms + RMS_EPS) * g).astype(BF16)


def _ffn_kernel(x_ref, g_ref, wa_ref, wb_ref, wo_ref, o_ref, *, fc):
    x = x_ref[...]
    xn = _rms_bf16(x, g_ref[...])
    acc = jnp.zeros(x.shape, F32)
    for c in range(wa_ref.shape[1] // fc):
        a = jnp.dot(xn, wa_ref[:, c * fc:(c + 1) * fc], preferred_element_type=F32)
        b = jnp.dot(xn, wb_ref[:, c * fc:(c + 1) * fc], preferred_element_type=F32)
        gate = (a * jax.nn.sigmoid(a) * b).astype(BF16)
        acc = acc + jnp.dot(gate, wo_ref[c * fc:(c + 1) * fc, :], preferred_element_type=F32)
    o_ref[...] = x + 0.5 * acc


def _ffn(h, g, w_in, w_out, tm):
    S, D = h.shape
    Fh = w_out.shape[0]
    return pl.pallas_call(
        functools.partial(_ffn_kernel, fc=256),
        grid=(S // tm,),
        in_specs=[pl.BlockSpec((tm, D), lambda i: (i, 0)),
                  _resident((1, D)),
                  pl.BlockSpec((D, Fh), lambda i: (0, 0), pipeline_mode=pl.Buffered(1)),
                  pl.BlockSpec((D, Fh), lambda i: (0, 1), pipeline_mode=pl.Buffered(1)),
                  _resident((Fh, D))],
        out_specs=pl.BlockSpec((tm, D), lambda i: (i, 0)),
        out_shape=jax.ShapeDtypeStruct((S, D), F32),
        compiler_params=_cparams(),
        name="ffn",
    )(h, g.reshape(1, D), w_in, w_in, w_out)


def _outproj_kernel(*refs, n):
    r_ref, o_ref = refs[0], refs[-1]
    acc = r_ref[...]
    for k in range(n):
        acc = acc + jnp.dot(refs[1 + k][...], refs[1 + n + k][...], preferred_element_type=F32)
    o_ref[...] = acc


def _outproj(res, xs, ws, tm):
    S, D = res.shape
    n = len(xs)
    in_specs = [pl.BlockSpec((tm, D), lambda i: (i, 0))]
    in_specs += [pl.BlockSpec((tm, x.shape[1]), lambda i: (i, 0)) for x in xs]
    in_specs += [_resident(w.shape) for w in ws]
    return pl.pallas_call(
        functools.partial(_outproj_kernel, n=n),
        grid=(S // tm,),
        in_specs=in_specs,
        out_specs=pl.BlockSpec((tm, D), lambda i: (i, 0)),
        out_shape=jax.ShapeDtypeStruct((S, D), F32),
        compiler_params=_cparams(),
        name="outproj",
    )(res, *xs, *ws)


def _lanes(ref, idx, tm):
    return jnp.concatenate([ref[idx]] * (tm // LANE), axis=1)


def _bf16_split3(x):
    a = x.astype(BF16).astype(F32)
    b = (x - a).astype(BF16).astype(F32)
    return a, b, x - a - b


def _heads_kernel(*refs, plan, tm, n_gain, rope, fox, kmean):
    it = iter(refs)
    x_ref, g_ref, w_ref = next(it), next(it), next(it)
    gn_ref = next(it) if n_gain else None
    cos = sin = None
    if rope:
        cos, sin = next(it)[...], next(it)[...]
    bf_ref = next(it) if fox else None
    n_out = 1 + max(p[6] for p in plan)
    outs = [next(it) for _ in range(n_out)]
    wt_ref = next(it) if fox else None
    km_ref = next(it) if kmean else None
    carry_ref = next(it) if fox else None

    y = jnp.dot(_rms_bf16(x_ref[...], g_ref[...]), w_ref[...], preferred_element_type=F32)
    zeros_pad = jnp.zeros((KPAD - HEAD_DIM, tm), F32)
    half = ROT_DIM // 2

    fq_rows = fk_rows = None
    if fox:
        @pl.when(pl.program_id(0) == 0)
        def _():
            carry_ref[...] = jnp.zeros(carry_ref.shape, F32)

        gT = y[:, y.shape[1] - LANE:].T
        xg = gT[0:FOX_HEADS] + _lanes(bf_ref, slice(None), tm)
        lf = jnp.minimum(xg, 0.0) - jnp.log1p(jnp.exp(-jnp.abs(xg)))
        tri = jnp.where(lax.broadcasted_iota(jnp.int32, (tm, tm), 0)
                        <= lax.broadcasted_iota(jnp.int32, (tm, tm), 1), 1.0, 0.0).astype(BF16)
        cs = sum(jnp.dot(part.astype(BF16), tri, preferred_element_type=F32) for part in _bf16_split3(lf))
        fcum = (carry_ref[...] + cs) * LOG2E
        carry_ref[...] = carry_ref[...] + jnp.sum(lf, axis=1, keepdims=True)
        wt_ref[...] = gT[FOX_HEADS:FOX_HEADS + IDX_HEADS] * (IDX_HEADS ** -0.5 * IDX_DIM ** -0.5)
        rid = lax.broadcasted_iota(jnp.int32, (SUBLANE, tm), 0)
        fq_rows, fk_rows = [], []
        for h in range(FOX_HEADS):
            f1, f2, f3 = _bf16_split3(fcum[h:h + 1])
            sel = lambda a, b, c, one: jnp.where(rid == 0, a, jnp.where(rid == 1, b, jnp.where(rid == 2, c, one)))
            ones_lo = jnp.where(rid < 6, 1.0, 0.0)
            fq_rows.append(sel(f1, f2, f3, ones_lo))
            fk_rows.append(jnp.where(rid < 3, 1.0, jnp.where(rid == 3, -f1, jnp.where(rid == 4, -f2,
                           jnp.where(rid == 5, -f3, 0.0)))))

    for kind, col, nheads, head0, gain, use_rope, out in plan:
        width = -(-nheads * HEAD_DIM // LANE) * LANE
        segT = y[:, col:col + width].T
        o_ref = outs[out]
        for hh in range(nheads):
            r = segT[hh * HEAD_DIM:(hh + 1) * HEAD_DIM]
            if gain is not None:
                r = r * lax.rsqrt(jnp.mean(r * r, axis=0, keepdims=True) + RMS_EPS) * _lanes(gn_ref, gain, tm)
            if use_rope:
                x1, x2 = r[:half], r[half:ROT_DIM]
                r = jnp.concatenate([x1 * cos - x2 * sin, x2 * cos + x1 * sin, r[ROT_DIM:]], axis=0)
            h = head0 + hh
            if kind == "v":
                o_ref[h] = jnp.concatenate([r, jnp.ones((VROWS - HEAD_DIM, tm), F32)], axis=0).astype(BF16)
                continue
            if kind == "q":
                r = r * QSCALE
            extra = None
            if fox and out == 0:
                extra = fq_rows[h]
            elif fox and out == 1:
                extra = fk_rows[h]
            if extra is None:
                padded = jnp.concatenate([r, zeros_pad], axis=0)
            else:
                padded = jnp.concatenate([r, extra, zeros_pad[SUBLANE:]], axis=0)
            if kind == "q":
                o_ref[h] = padded.astype(BF16)
            else:
                o_ref[h] = padded.T.astype(BF16)
                if kmean:
                    lane = lax.broadcasted_iota(jnp.int32, (HEAD_DIM, LANE), 1)
                    tile = jnp.zeros((HEAD_DIM, LANE), F32)
                    for b in range(tm // MOBA_BLOCK):
                        mb = jnp.sum(r[:, b * MOBA_BLOCK:(b + 1) * MOBA_BLOCK], axis=1, keepdims=True)
                        tile = jnp.where(lane == b, mb * (1.0 / MOBA_BLOCK), tile)
                    km_ref[0, h] = tile


def _heads_proj(x, g, w, plan, out_heads, tm, gains=None, rope=None, b_forget=None, kmean=False):
    S, D = x.shape
    N = w.shape[1]
    fox = b_forget is not None
    in_specs = [pl.BlockSpec((tm, D), lambda i: (i, 0)), _resident((1, D)), _resident((D, N))]
    args = [x, g.reshape(1, D), w]
    if gains is not None:
        gt = jnp.broadcast_to(gains[:, :, None], gains.shape + (LANE,)).astype(F32)
        in_specs.append(_resident(gt.shape))
        args.append(gt)
    if rope is not None:
        in_specs += [pl.BlockSpec((ROT_DIM // 2, tm), lambda i: (0, i))] * 2
        args += list(rope)
    if fox:
        in_specs.append(_resident((FOX_HEADS, LANE)))
        args.append(jnp.broadcast_to(b_forget[:, None], (FOX_HEADS, LANE)).astype(F32))
    out_specs, out_shapes = [], []
    for kind, nh in out_heads:
        if kind == "q":
            out_specs.append(pl.BlockSpec((nh, KPAD, tm), lambda i: (0, 0, i)))
            out_shapes.append(jax.ShapeDtypeStruct((nh, KPAD, S), BF16))
        elif kind == "k":
            out_specs.append(pl.BlockSpec((nh, tm, KPAD), lambda i: (0, i, 0)))
            out_shapes.append(jax.ShapeDtypeStruct((nh, S, KPAD), BF16))
        else:
            out_specs.append(pl.BlockSpec((nh, VROWS, tm), lambda i: (0, 0, i)))
            out_shapes.append(jax.ShapeDtypeStruct((nh, VROWS, S), BF16))
    if fox:
        out_specs.append(pl.BlockSpec((IDX_HEADS, tm), lambda i: (0, i)))
        out_shapes.append(jax.ShapeDtypeStruct((IDX_HEADS, S), F32))
    if kmean:
        nhk = out_heads[1][1]
        out_specs.append(pl.BlockSpec((1, nhk, HEAD_DIM, LANE), lambda i: (i, 0, 0, 0)))
        out_shapes.append(jax.ShapeDtypeStruct((S // tm, nhk, HEAD_DIM, LANE), F32))
    return pl.pallas_call(
        functools.partial(_heads_kernel, plan=tuple(plan), tm=tm, n_gain=0 if gains is None else gains.shape[0],
                          rope=rope is not None, fox=fox, kmean=kmean),
        grid=(S // tm,),
        in_specs=in_specs,
        out_specs=out_specs,
        out_shape=out_shapes,
        scratch_shapes=[pltpu.VMEM((FOX_HEADS, 1), F32)] if fox else [],
        compiler_params=_cparams(),
        name="heads_proj",
    )(*args)


def _heads_update(units, qT_ref, k_ref, vT_ref, s_ref, p_ref, m_ref, acc_ref, biased):
    _, tk, tqu = s_ref.shape
    rc = min(ATTN_ROWS, tk)
    n = len(units)
    m_new = alpha = None
    for t in range(n + 1):
        mx = None
        if t < n:
            h, q0, dg = units[t]
            qT = qT_ref[h, :, q0:q0 + tqu]
        if t >= 1:
            hb, qb, _ = units[t - 1]
            m_b, alpha_b = m_new, alpha
        for r0 in range(0, tk, rc):
            if t < n:
                s = biased(jnp.dot(k_ref[h, r0:r0 + rc, :], qT, preferred_element_type=F32), h, q0, r0, dg)
                s_ref[t % 2, r0:r0 + rc, :] = s
                cm = jnp.max(s.reshape(rc // SUBLANE, SUBLANE, tqu), axis=0)
                mx = cm if mx is None else jnp.maximum(mx, cm)
            if t >= 1:
                p_ref[(t - 1) % 2, r0:r0 + rc, :] = jnp.exp2(s_ref[(t - 1) % 2, r0:r0 + rc, :] - m_b).astype(BF16)
        if t >= 1:
            pv = jnp.dot(vT_ref[hb], p_ref[(t - 1) % 2], preferred_element_type=F32)
            acc_ref[hb, :, qb:qb + tqu] = alpha_b * acc_ref[hb, :, qb:qb + tqu] + pv
        if t < n:
            m_prev = m_ref[h, :, q0:q0 + tqu]
            m_new = jnp.maximum(m_prev, jnp.max(mx, axis=0, keepdims=True))
            alpha = jnp.exp2(m_prev - m_new)
            m_ref[h, :, q0:q0 + tqu] = m_new


def _attn_init(m_ref, acc_ref):
    m_ref[...] = jnp.full(m_ref.shape, NEG, F32)
    acc_ref[...] = jnp.zeros(acc_ref.shape, F32)


def _attn_finish(o_ref, acc_ref, nh):
    hd = HEAD_DIM
    out = lambda h: acc_ref[h, :hd, :] / acc_ref[h, hd:hd + 1, :]
    for h in range(0, nh, 2):
        o_ref[:, h * hd:(h + 2) * hd] = jnp.concatenate([out(h), out(h + 1)], axis=0).T.astype(o_ref.dtype)


def _attn_kernel(qi_ref, kj_ref, *refs, mode, nh, tq, tk, nsub):
    if mode == "moba":
        qT_ref, k_ref, vT_ref, sb_ref, o_ref, m_ref, acc_ref, s_ref, p_ref, cb_ref = refs
    elif mode == "causal":
        qT_ref, k_ref, vT_ref, o_ref, m_ref, acc_ref, s_ref, p_ref, cb_ref = refs
    else:
        qT_ref, k_ref, vT_ref, o_ref, m_ref, acc_ref, s_ref, p_ref = refs
    p = pl.program_id(0)
    i = qi_ref[p]
    j = kj_ref[p]
    causal = mode != "none"
    tqu = tq // nsub

    if causal:
        @pl.when(p == 0)
        def _():
            row = lax.broadcasted_iota(jnp.int32, (tk, tqu), 0)
            col = lax.broadcasted_iota(jnp.int32, (tk, tqu), 1)
            cb_ref[...] = jnp.where(row <= col, 0.0, NEG)

    @pl.when(j == 0)
    def _():
        _attn_init(m_ref, acc_ref)

    def biased(s, h, q0, r0, diag):
        if mode == "moba":
            s = s + sb_ref[h, pl.ds(j * (tk // MOBA_BLOCK) + r0 // MOBA_BLOCK, 1), q0:q0 + tqu]
        if diag:
            s = s + cb_ref[r0:r0 + s.shape[0], :]
        return s

    def process(first):
        units = [(h, q * tqu, q == first) for h in range(nh) for q in range(max(first, 0), nsub)]
        _heads_update(units, qT_ref, k_ref, vT_ref, s_ref, p_ref, m_ref, acc_ref, biased)

    if causal:
        d = j - nsub * i
        pl.when(d < 0)(lambda: process(-1))
        for q in range(nsub):
            pl.when(d == q)(functools.partial(process, q))
        last = d == nsub - 1
    else:
        process(-1)
        last = j == 0

    @pl.when(last)
    def _():
        _attn_finish(o_ref, acc_ref, nh)


def _attention(mode, qT, k, vT, extra, tq, tk, nsub):
    nh, _, S = qT.shape
    hd = HEAD_DIM
    nq = S // tq
    if mode == "none":
        pairs = [(i, 0) for i in range(nq)]
    else:
        assert tq == nsub * tk
        pairs = [(i, j) for i in range(nq) for j in range(nsub * (i + 1))]
    qi = jnp.asarray(np.array([a for a, _ in pairs], np.int32))
    kj = jnp.asarray(np.array([b for _, b in pairs], np.int32))
    in_specs = [pl.BlockSpec((nh, KPAD, tq), lambda p, qi, kj: (0, 0, qi[p])),
                pl.BlockSpec((nh, tk, KPAD), lambda p, qi, kj: (0, kj[p], 0)),
                pl.BlockSpec((nh, VROWS, tk), lambda p, qi, kj: (0, 0, kj[p]))]
    if mode == "moba":
        nb = extra[0].shape[1]
        in_specs += [pl.BlockSpec((nh, nb, tq), lambda p, qi, kj: (0, 0, qi[p]))]
    tqu = tq // nsub
    scratch = [pltpu.VMEM((nh, 1, tq), F32), pltpu.VMEM((nh, VROWS, tq), F32),
               pltpu.VMEM((2, tk, tqu), F32), pltpu.VMEM((2, tk, tqu), BF16)]
    if mode != "none":
        scratch.append(pltpu.VMEM((tk, tqu), F32))
    return pl.pallas_call(
        functools.partial(_attn_kernel, mode=mode, nh=nh, tq=tq, tk=tk, nsub=nsub),
        grid_spec=pltpu.PrefetchScalarGridSpec(
            num_scalar_prefetch=2,
            grid=(len(pairs),),
            in_specs=in_specs,
            out_specs=pl.BlockSpec((tq, nh * hd), lambda p, qi, kj: (qi[p], 0)),
            scratch_shapes=scratch),
        out_shape=jax.ShapeDtypeStruct((S, nh * hd), BF16),
        compiler_params=_cparams(),
        name="attn_" + mode,
    )(qi, kj, qT, k, vT, *extra)


def _gate_kernel(qT_ref, km_ref, o_ref, *, nh, nb, tq):
    i = pl.program_id(0)
    n = lax.broadcasted_iota(jnp.int32, (nb, tq), 0)
    own = (i * tq + lax.broadcasted_iota(jnp.int32, (nb, tq), 1)) // MOBA_BLOCK
    ninf = jnp.float32(-jnp.inf)

    def body(h, carry):
        g = jnp.dot(km_ref[h], qT_ref[h], preferred_element_type=F32)
        g = jnp.where(n < own, g, ninf)
        keep = n == own
        for _ in range(MOBA_TOPK):
            mx = jnp.max(g, axis=0, keepdims=True)
            first = jnp.min(jnp.where((g == mx) & (mx > ninf), n, nb), axis=0, keepdims=True)
            pick = n == first
            keep = keep | pick
            g = jnp.where(pick, ninf, g)
        o_ref[h] = jnp.where(keep, 0.0, NEG)
        return carry

    lax.fori_loop(0, nh, body, 0)


def _moba_gate(qT, kmean, tq):
    nh, _, S = qT.shape
    nb = kmean.shape[1]
    return pl.pallas_call(
        functools.partial(_gate_kernel, nh=nh, nb=nb, tq=tq),
        grid=(S // tq,),
        in_specs=[pl.BlockSpec((nh, KPAD, tq), lambda i: (0, 0, i)), _resident((nh, nb, KPAD))],
        out_specs=pl.BlockSpec((nh, nb, tq), lambda i: (0, 0, i)),
        out_shape=jax.ShapeDtypeStruct((nh, nb, S), F32),
        compiler_params=_cparams(),
        name="moba_gate",
    )(qT, kmean)


def _dsa_kernel(qi_ref, kj_ref, ph_ref, ikb_ref, kvb_ref, nk_ref,
                iqT_ref, ik_ref, w_ref, qT_ref, k_ref, vT_ref, o_ref,
                hi_ref, lo_ref, thi_ref, tlo_ref, cut_ref, m_ref, acc_ref, s_ref, p_ref, bias_ref,
                *, nih, nh, tq, tk, topk, seq, chunk):
    p = pl.program_id(0)
    i = qi_ref[p]
    j = kj_ref[p]
    phase = ph_ref[p]
    nkb = nk_ref[p]
    I16, I32 = jnp.int16, jnp.int32
    lo_min = -(2 ** 15)
    one, zero = jnp.ones((), I16), jnp.zeros((), I16)
    row16 = lambda x: x.astype(I16)

    @pl.when(phase == 0)
    def _scores():
        ik = ik_ref[...]
        sc = jnp.zeros((tk, tq), F32)
        for h in range(nih):
            d = jnp.dot(ik, iqT_ref[h], preferred_element_type=F32)
            sc = sc + w_ref[h:h + 1, :] * jnp.maximum(d, 0.0)
        b = lax.bitcast_convert_type(sc, I32)
        key = b ^ ((b >> 31) & 0x7FFFFFFF)
        row = j * tk + lax.broadcasted_iota(I32, (tk, tq), 0)
        col = i * tq + lax.broadcasted_iota(I32, (tk, tq), 1)
        key = jnp.where(row <= col, key, INT_MIN)
        rows = pl.ds(pl.multiple_of(j * tk, tk), tk)
        hi_ref[rows, :] = (key >> 16).astype(I16)
        lo_ref[rows, :] = ((key & 0xFFFF) + lo_min).astype(I16)

    def count(hit):
        def body(c, acc):
            r0 = pl.multiple_of(c * chunk, chunk)
            h = hit(hi_ref[pl.ds(r0, chunk), :], lo_ref[pl.ds(r0, chunk), :], r0)
            part = h[0:2 * SUBLANE]
            for r in range(2 * SUBLANE, chunk, 2 * SUBLANE):
                part = part + h[r:r + 2 * SUBLANE]
            return acc + part.astype(I32)
        acc = lax.fori_loop(0, nkb * (tk // chunk), body, jnp.zeros((2 * SUBLANE, tq), I32))
        return jnp.sum(acc, axis=0, keepdims=True)

    def kth_largest(pick, k):
        c0 = count(lambda hi, lo, r0: jnp.where(pick(hi, lo) >= zero, one, zero))
        t = jnp.where(c0 >= k, 0, lo_min)

        def bit_body(it, t):
            cand = t | jnp.left_shift(jnp.int32(1), 14 - it)
            c16 = row16(cand)
            cnt = count(lambda hi, lo, r0: jnp.where(pick(hi, lo) >= c16, one, zero))
            return jnp.where(cnt >= k, cand, t)

        return lax.fori_loop(0, 15, bit_body, t)

    @pl.when((phase == 1) & (j == 0))
    def _select():
        thi = kth_largest(lambda hi, lo: hi, topk)
        thi16 = row16(thi)
        above = count(lambda hi, lo, r0: jnp.where(hi > thi16, one, zero))
        need = topk - above

        def mark(c, carry):
            rows = pl.ds(pl.multiple_of(c * chunk, chunk), chunk)
            lo_ref[rows, :] = jnp.where(hi_ref[rows, :] == thi16, lo_ref[rows, :], jnp.full((), lo_min, I16))
            return carry

        lax.fori_loop(0, nkb * (tk // chunk), mark, 0)
        tlo = kth_largest(lambda hi, lo: lo, need)
        tlo = jnp.where((thi == lo_min) & (tlo == lo_min), lo_min + 1, tlo)
        tlo16 = row16(tlo)
        thi_ref[...] = thi
        tlo_ref[...] = tlo
        cut_ref[...] = jnp.full((1, tq), seq, I32)
        in_bucket = lambda hi, x: jnp.where(hi == thi16, x, zero)
        cge = above + count(lambda hi, lo, r0: in_bucket(hi, jnp.where(lo >= tlo16, one, zero)))

        @pl.when(jnp.max(cge) > topk)
        def _ties():
            need_tie = need - count(lambda hi, lo, r0: jnp.where(lo > tlo16, one, zero))
            nbits = max(1, int(seq - 1).bit_length())

            def pos_body(it, x):
                cand = x + jnp.left_shift(jnp.int32(1), nbits - 1 - it)
                before = lambda r0: jnp.where(r0 + lax.broadcasted_iota(I32, (chunk, tq), 0) < cand, 1, 0).astype(I16)
                c = count(lambda hi, lo, r0: in_bucket(hi, jnp.where(lo == tlo16, before(r0), zero)))
                return jnp.where(c < need_tie, cand, x)

            x = lax.fori_loop(0, nbits, pos_body, jnp.zeros((1, tq), I32))
            cut_ref[...] = jnp.where(cge > topk, x, seq)

        _attn_init(m_ref, acc_ref)

    @pl.when(phase == 1)
    def _attend():
        rows = pl.ds(pl.multiple_of(j * tk, tk), tk)
        hi, lo = hi_ref[rows, :], lo_ref[rows, :]
        thi16, tlo16 = row16(thi_ref[...]), row16(tlo_ref[...])
        kept = jnp.where(j * tk + lax.broadcasted_iota(I32, (tk, tq), 0) <= cut_ref[...], 1, 0).astype(I16)
        tie = jnp.where(lo == tlo16, kept, zero)
        sel = jnp.where(hi > thi16, one, jnp.where(hi == thi16, jnp.where(lo > tlo16, one, tie), zero))
        bias_ref[...] = jnp.where(sel.astype(I32) > 0, 0.0, NEG)

        def biased(s, h, q0, r0, diag):
            return s + bias_ref[r0:r0 + s.shape[0], :]

        units = [(h, 0, False) for h in range(nh)]
        _heads_update(units, qT_ref, k_ref, vT_ref, s_ref, p_ref, m_ref, acc_ref, biased)

        @pl.when(j == nkb - 1)
        def _():
            _attn_finish(o_ref, acc_ref, nh)


def _dsa_attention(iqT, ik, wT, qT, k, vT, tq, tk, topk):
    nh, _, S = qT.shape
    hd = HEAD_DIM
    nih = iqT.shape[0]
    steps = []
    for i in range(S // tq):
        nkb = (i * tq + tq - 1) // tk + 1
        steps += [(i, j, 0, j, 0, nkb) for j in range(nkb)]
        steps += [(i, j, 1, nkb - 1, j, nkb) for j in range(nkb)]
    tab = [jnp.asarray(np.array([s[c] for s in steps], np.int32)) for c in range(6)]
    chunk = min(tk, 512)
    qmap = lambda p, qi, kj, ph, ikb, kvb, nk: (0, 0, qi[p])
    return pl.pallas_call(
        functools.partial(_dsa_kernel, nih=nih, nh=nh, tq=tq, tk=tk, topk=topk, seq=S, chunk=chunk),
        grid_spec=pltpu.PrefetchScalarGridSpec(
            num_scalar_prefetch=6,
            grid=(len(steps),),
            in_specs=[pl.BlockSpec((nih, KPAD, tq), qmap),
                      pl.BlockSpec((tk, KPAD), lambda p, qi, kj, ph, ikb, kvb, nk: (ikb[p], 0)),
                      pl.BlockSpec((nih, tq), lambda p, qi, kj, ph, ikb, kvb, nk: (0, qi[p])),
                      pl.BlockSpec((nh, KPAD, tq), qmap),
                      pl.BlockSpec((nh, tk, KPAD), lambda p, qi, kj, ph, ikb, kvb, nk: (0, kvb[p], 0)),
                      pl.BlockSpec((nh, VROWS, tk), lambda p, qi, kj, ph, ikb, kvb, nk: (0, 0, kvb[p]))],
            out_specs=pl.BlockSpec((tq, nh * hd), lambda p, qi, kj, ph, ikb, kvb, nk: (qi[p], 0)),
            scratch_shapes=[pltpu.VMEM((S, tq), jnp.int16), pltpu.VMEM((S, tq), jnp.int16),
                            pltpu.VMEM((1, tq), jnp.int32), pltpu.VMEM((1, tq), jnp.int32),
                            pltpu.VMEM((1, tq), jnp.int32),
                            pltpu.VMEM((nh, 1, tq), F32), pltpu.VMEM((nh, VROWS, tq), F32),
                            pltpu.VMEM((2, tk, tq), F32), pltpu.VMEM((2, tk, tq), BF16),
                            pltpu.VMEM((tk, tq), F32)]),
        out_shape=jax.ShapeDtypeStruct((S, nh * hd), BF16),
        compiler_params=_cparams(),
        name="dsa",
    )(*tab, iqT, ik, wT, qT, k, vT)


def kernel(x, mem, ffn1_norm, ffn1_w_in, ffn1_w_out, mix_norm, even_w_in, even_b_forget, fox_q_norm, fox_k_norm, dsa_q_norm, dsa_k_norm, odd_w_in, moba_q_norm, moba_k_norm, mix_w_out, mem_norm_x, mem_norm_m, mem_wq, mem_wkv, mem_q_norm, mem_k_norm, mem_wo, ffn2_norm, ffn2_w_in, ffn2_w_out):
    _, S, D = x.shape
    M = mem.shape[1]
    depth = ffn1_norm.shape[0]
    tm = min(512, S)
    ta = min(512, S)
    nsub = 2 if S % (2 * ta) == 0 else 1
    tdq = min(512, S)
    pos = jnp.arange(S, dtype=F32)
    inv = ROPE_THETA ** (-jnp.arange(0, ROT_DIM, 2, dtype=F32) / ROT_DIM)
    ang = inv[:, None] * pos[None, :]
    rope = (jnp.cos(ang), jnp.sin(ang))
    W8 = 8 * HEAD_DIM
    zcols = lambda n: jnp.zeros((D, n), F32)

    h = x[0]
    m0 = mem[0]
    for layer in range(depth):
        h = _ffn(h, ffn1_norm[layer], ffn1_w_in[layer].astype(BF16), ffn1_w_out[layer].astype(BF16), tm)
        if layer % 2 == 0:
            e = layer // 2
            o = np.cumsum([0, W8, W8, W8, FOX_HEADS, W8, W8, W8, IDX_HEADS * IDX_DIM, IDX_DIM, IDX_HEADS])
            w = even_w_in[e]
            seg = lambda n: w[:, o[n]:o[n + 1]]
            w = jnp.concatenate([seg(0), seg(1), seg(2), seg(4), seg(5), seg(6), seg(7),
                                 seg(8), zcols(KPAD - IDX_DIM),
                                 seg(3), seg(9), zcols(LANE - FOX_HEADS - IDX_HEADS)], axis=1).astype(BF16)
            plan = [("q", 0, 8, 0, 0, False, 0), ("k", W8, 8, 0, 1, False, 1), ("v", 2 * W8, 8, 0, None, False, 2),
                    ("q", 3 * W8, 8, 0, 2, True, 3), ("k", 4 * W8, 8, 0, 3, True, 4), ("v", 5 * W8, 8, 0, None, False, 5),
                    ("q", 6 * W8, 8, 0, None, True, 6), ("k", 7 * W8, 1, 0, None, True, 7)]
            gains = jnp.stack([fox_q_norm[e], fox_k_norm[e], dsa_q_norm[e], dsa_k_norm[e]])
            fqT, fk, fvT, dqT, dk, dvT, iqT, ik, wT = _heads_proj(
                h, mix_norm[layer], w, plan,
                [("q", 8), ("k", 8), ("v", 8), ("q", 8), ("k", 8), ("v", 8), ("q", 8), ("k", 1)],
                tm, gains=gains, rope=rope, b_forget=even_b_forget[e])
            fox = _attention("causal", fqT, fk, fvT, (), nsub * ta, ta, nsub)
            dsa = _dsa_attention(iqT, ik[0], wT, dqT, dk, dvT, tdq, ta, min(DSA_TOPK, S // 4))
            wo = mix_w_out[layer].astype(BF16)
            h = _outproj(h, [fox, dsa], [wo[:W8], wo[W8:]], tm)
        else:
            od = layer // 2
            w = odd_w_in[od].astype(BF16)
            plan = [("q", 0, 8, 0, 0, True, 0), ("q", W8, 8, 8, 0, True, 0),
                    ("k", 2 * W8, 8, 0, 1, True, 1), ("k", 3 * W8, 8, 8, 1, True, 1),
                    ("v", 4 * W8, 8, 0, None, False, 2), ("v", 5 * W8, 8, 8, None, False, 2)]
            gains = jnp.stack([moba_q_norm[od], moba_k_norm[od]])
            qT, k, vT, km = _heads_proj(h, mix_norm[layer], w, plan, [("q", 16), ("k", 16), ("v", 16)],
                                        tm, gains=gains, rope=rope, kmean=True)
            nbt = tm // MOBA_BLOCK
            km = jnp.transpose(km[..., :nbt], (1, 0, 3, 2)).reshape(MOBA_HEADS, S // MOBA_BLOCK, HEAD_DIM)
            km = jnp.pad(km, ((0, 0), (0, 0), (0, KPAD - HEAD_DIM))).astype(BF16)
            selb = _moba_gate(qT, km, ta)
            moba = _attention("moba", qT, k, vT, (selb,), nsub * ta, ta, nsub)
            h = _outproj(h, [moba], [mix_w_out[layer].astype(BF16)], tm)
        mw = MEM_HEADS * HEAD_DIM
        (mqT,) = _heads_proj(h, mem_norm_x[layer], mem_wq[layer].astype(BF16),
                             [("q", 0, MEM_HEADS, 0, 0, False, 0)], [("q", MEM_HEADS)], tm,
                             gains=mem_q_norm[layer][None])
        mk, mvT = _heads_proj(m0, mem_norm_m[layer], mem_wkv[layer].astype(BF16),
                              [("k", 0, MEM_HEADS, 0, 0, False, 0), ("v", mw, MEM_HEADS, 0, None, False, 1)],
                              [("k", MEM_HEADS), ("v", MEM_HEADS)], M, gains=mem_k_norm[layer][None])
        ma = _attention("none", mqT, mk, mvT, (), nsub * ta, M, nsub)
        h = _outproj(h, [ma], [mem_wo[layer].astype(BF16)], tm)
        h = _ffn(h, ffn2_norm[layer], ffn2_w_in[layer].astype(BF16), ffn2_w_out[layer].astype(BF16), tm)
    return h[None]
```

```python
import functools

import numpy as np
import jax
import jax.numpy as jnp
from jax import lax
from jax.experimental import pallas as pl
from jax.experimental.pallas import tpu as pltpu

HEAD_DIM = 64
ROT_DIM = HEAD_DIM // 4
ROPE_THETA = 500000.0
FOX_HEADS = 8
DSA_HEADS = 8
MOBA_HEADS = 16
IDX_HEADS = 8
IDX_DIM = 64
DSA_TOPK = 256
MOBA_BLOCK = 256
MOBA_TOPK = 3
MEM_HEADS = 4
RMS_EPS = 1e-6

NEG = -1e30
INT_MIN = -(2 ** 31)
LANE = 128
SUBLANE = 8
KPAD = 128
VROWS = HEAD_DIM + 16
ATTN_ROWS = 512
VMEM_LIMIT = 56 * 1024 * 1024
LOG2E = 1.4426950408889634
QSCALE = HEAD_DIM ** -0.5 * LOG2E

F32 = jnp.float32
BF16 = jnp.bfloat16


def _cparams(n_axes=1):
    return pltpu.CompilerParams(dimension_semantics=("arbitrary",) * n_axes, vmem_limit_bytes=VMEM_LIMIT)


def _resident(shape):
    nd = len(shape)
    return pl.BlockSpec(shape, lambda *_: (0,) * nd, pipeline_mode=pl.Buffered(1))


def _rms_bf16(x, g):
    ms = jnp.mean(x * x, axis=-1, keepdims=True)
    return (x * lax.rsqrt(ms + RMS_EPS) * g).astype(BF16)


def _ffn_kernel(x_ref, g_ref, wa_ref, wb_ref, wo_ref, o_ref, *, fc):
    x = x_ref[...]
    xn = _rms_bf16(x, g_ref[...])
    acc = jnp.zeros(x.shape, F32)
    for c in range(wa_ref.shape[1] // fc):
        a = jnp.dot(xn, wa_ref[:, c * fc:(c + 1) * fc], preferred_element_type=F32)
        b = jnp.dot(xn, wb_ref[:, c * fc:(c + 1) * fc], preferred_element_type=F32)
        gate = (a * jax.nn.sigmoid(a) * b).astype(BF16)
        acc = acc + jnp.dot(gate, wo_ref[c * fc:(c + 1) * fc, :], preferred_element_type=F32)
    o_ref[...] = x + 0.5 * acc


def _ffn(h, g, w_in, w_out, tm):
    S, D = h.shape
    Fh = w_out.shape[0]
    return pl.pallas_call(
        functools.partial(_ffn_kernel, fc=256),
        grid=(S // tm,),
        in_specs=[pl.BlockSpec((tm, D), lambda i: (i, 0)),
                  _resident((1, D)),
                  pl.BlockSpec((D, Fh), lambda i: (0, 0), pipeline_mode=pl.Buffered(1)),
                  pl.BlockSpec((D, Fh), lambda i: (0, 1), pipeline_mode=pl.Buffered(1)),
                  _resident((Fh, D))],
        out_specs=pl.BlockSpec((tm, D), lambda i: (i, 0)),
        out_shape=jax.ShapeDtypeStruct((S, D), F32),
        compiler_params=_cparams(),
        name="ffn",
    )(h, g.reshape(1, D), w_in, w_in, w_out)


def _outproj_kernel(*refs, n):
    r_ref, o_ref = refs[0], refs[-1]
    acc = r_ref[...]
    for k in range(n):
        acc = acc + jnp.dot(refs[1 + k][...], refs[1 + n + k][...], preferred_element_type=F32)
    o_ref[...] = acc


def _outproj(res, xs, ws, tm):
    S, D = res.shape
    n = len(xs)
    in_specs = [pl.BlockSpec((tm, D), lambda i: (i, 0))]
    in_specs += [pl.BlockSpec((tm, x.shape[1]), lambda i: (i, 0)) for x in xs]
    in_specs += [_resident(w.shape) for w in ws]
    return pl.pallas_call(
        functools.partial(_outproj_kernel, n=n),
        grid=(S // tm,),
        in_specs=in_specs,
        out_specs=pl.BlockSpec((tm, D), lambda i: (i, 0)),
        out_shape=jax.ShapeDtypeStruct((S, D), F32),
        compiler_params=_cparams(),
        name="outproj",
    )(res, *xs, *ws)


def _lanes(ref, idx, tm):
    return jnp.concatenate([ref[idx]] * (tm // LANE), axis=1)


def _bf16_split3(x):
    a = x.astype(BF16).astype(F32)
    b = (x - a).astype(BF16).astype(F32)
    return a, b, x - a - b


def _heads_kernel(*refs, plan, tm, n_gain, rope, fox, kmean):
    it = iter(refs)
    x_ref, g_ref, w_ref = next(it), next(it), next(it)
    gn_ref = next(it) if n_gain else None
    cos = sin = None
    if rope:
        cos, sin = next(it)[...], next(it)[...]
    bf_ref = next(it) if fox else None
    n_out = 1 + max(p[6] for p in plan)
    outs = [next(it) for _ in range(n_out)]
    wt_ref = next(it) if fox else None
    km_ref = next(it) if kmean else None
    carry_ref = next(it) if fox else None

    y = jnp.dot(_rms_bf16(x_ref[...], g_ref[...]), w_ref[...], preferred_element_type=F32)
    zeros_pad = jnp.zeros((KPAD - HEAD_DIM, tm), F32)
    half = ROT_DIM // 2

    fq_rows = fk_rows = None
    if fox:
        @pl.when(pl.program_id(0) == 0)
        def _():
            carry_ref[...] = jnp.zeros(carry_ref.shape, F32)

        gT = y[:, y.shape[1] - LANE:].T
        xg = gT[0:FOX_HEADS] + _lanes(bf_ref, slice(None), tm)
        lf = jnp.minimum(xg, 0.0) - jnp.log1p(jnp.exp(-jnp.abs(xg)))
        tri = jnp.where(lax.broadcasted_iota(jnp.int32, (tm, tm), 0)
                        <= lax.broadcasted_iota(jnp.int32, (tm, tm), 1), 1.0, 0.0).astype(BF16)
        cs = sum(jnp.dot(part.astype(BF16), tri, preferred_element_type=F32) for part in _bf16_split3(lf))
        fcum = (carry_ref[...] + cs) * LOG2E
        carry_ref[...] = carry_ref[...] + jnp.sum(lf, axis=1, keepdims=True)
        wt_ref[...] = gT[FOX_HEADS:FOX_HEADS + IDX_HEADS] * (IDX_HEADS ** -0.5 * IDX_DIM ** -0.5)
        rid = lax.broadcasted_iota(jnp.int32, (SUBLANE, tm), 0)
        fq_rows, fk_rows = [], []
        for h in range(FOX_HEADS):
            f1, f2, f3 = _bf16_split3(fcum[h:h + 1])
            sel = lambda a, b, c, one: jnp.where(rid == 0, a, jnp.where(rid == 1, b, jnp.where(rid == 2, c, one)))
            ones_lo = jnp.where(rid < 6, 1.0, 0.0)
            fq_rows.append(sel(f1, f2, f3, ones_lo))
            fk_rows.append(jnp.where(rid < 3, 1.0, jnp.where(rid == 3, -f1, jnp.where(rid == 4, -f2,
                           jnp.where(rid == 5, -f3, 0.0)))))

    for kind, col, nheads, head0, gain, use_rope, out in plan:
        width = -(-nheads * HEAD_DIM // LANE) * LANE
        segT = y[:, col:col + width].T
        o_ref = outs[out]
        for hh in range(nheads):
            r = segT[hh * HEAD_DIM:(hh + 1) * HEAD_DIM]
            if gain is not None:
                r = r * lax.rsqrt(jnp.mean(r * r, axis=0, keepdims=True) + RMS_EPS) * _lanes(gn_ref, gain, tm)
            if use_rope:
                x1, x2 = r[:half], r[half:ROT_DIM]
                r = jnp.concatenate([x1 * cos - x2 * sin, x2 * cos + x1 * sin, r[ROT_DIM:]], axis=0)
            h = head0 + hh
            if kind == "v":
                o_ref[h] = jnp.concatenate([r, jnp.ones((VROWS - HEAD_DIM, tm), F32)], axis=0).astype(BF16)
                continue
            if kind == "q":
                r = r * QSCALE
            extra = None
            if fox and out == 0:
                extra = fq_rows[h]
            elif fox and out == 1:
                extra = fk_rows[h]
            if extra is None:
                padded = jnp.concatenate([r, zeros_pad], axis=0)
            else:
                padded = jnp.concatenate([r, extra, zeros_pad[SUBLANE:]], axis=0)
            if kind == "q":
                o_ref[h] = padded.astype(BF16)
            else:
                o_ref[h] = padded.T.astype(BF16)
                if kmean:
                    lane = lax.broadcasted_iota(jnp.int32, (HEAD_DIM, LANE), 1)
                    tile = jnp.zeros((HEAD_DIM, LANE), F32)
                    for b in range(tm // MOBA_BLOCK):
                        mb = jnp.sum(r[:, b * MOBA_BLOCK:(b + 1) * MOBA_BLOCK], axis=1, keepdims=True)
                        tile = jnp.where(lane == b, mb * (1.0 / MOBA_BLOCK), tile)
                    km_ref[0, h] = tile


def _heads_proj(x, g, w, plan, out_heads, tm, gains=None, rope=None, b_forget=None, kmean=False):
    S, D = x.shape
    N = w.shape[1]
    fox = b_forget is not None
    in_specs = [pl.BlockSpec((tm, D), lambda i: (i, 0)), _resident((1, D)), _resident((D, N))]
    args = [x, g.reshape(1, D), w]
    if gains is not None:
        gt = jnp.broadcast_to(gains[:, :, None], gains.shape + (LANE,)).astype(F32)
        in_specs.append(_resident(gt.shape))
        args.append(gt)
    if rope is not None:
        in_specs += [pl.BlockSpec((ROT_DIM // 2, tm), lambda i: (0, i))] * 2
        args += list(rope)
    if fox:
        in_specs.append(_resident((FOX_HEADS, LANE)))
        args.append(jnp.broadcast_to(b_forget[:, None], (FOX_HEADS, LANE)).astype(F32))
    out_specs, out_shapes = [], []
    for kind, nh in out_heads:
        if kind == "q":
            out_specs.append(pl.BlockSpec((nh, KPAD, tm), lambda i: (0, 0, i)))
            out_shapes.append(jax.ShapeDtypeStruct((nh, KPAD, S), BF16))
        elif kind == "k":
            out_specs.append(pl.BlockSpec((nh, tm, KPAD), lambda i: (0, i, 0)))
            out_shapes.append(jax.ShapeDtypeStruct((nh, S, KPAD), BF16))
        else:
            out_specs.append(pl.BlockSpec((nh, VROWS, tm), lambda i: (0, 0, i)))
            out_shapes.append(jax.ShapeDtypeStruct((nh, VROWS, S), BF16))
    if fox:
        out_specs.append(pl.BlockSpec((IDX_HEADS, tm), lambda i: (0, i)))
        out_shapes.append(jax.ShapeDtypeStruct((IDX_HEADS, S), F32))
    if kmean:
        nhk = out_heads[1][1]
        out_specs.append(pl.BlockSpec((1, nhk, HEAD_DIM, LANE), lambda i: (i, 0, 0, 0)))
        out_shapes.append(jax.ShapeDtypeStruct((S // tm, nhk, HEAD_DIM, LANE), F32))
    return pl.pallas_call(
        functools.partial(_heads_kernel, plan=tuple(plan), tm=tm, n_gain=0 if gains is None else gains.shape[0],
                          rope=rope is not None, fox=fox, kmean=kmean),
        grid=(S // tm,),
        in_specs=in_specs,
        out_specs=out_specs,
        out_shape=out_shapes,
        scratch_shapes=[pltpu.VMEM((FOX_HEADS, 1), F32)] if fox else [],
        compiler_params=_cparams(),
        name="heads_proj",
    )(*args)


def _heads_update(units, qT_ref, k_ref, vT_ref, s_ref, p_ref, m_ref, acc_ref, biased):
    _, tk, tqu = s_ref.shape
    rc = min(ATTN_ROWS, tk)
    n = len(units)
    m_new = alpha = None
    for t in range(n + 1):
        mx = None
        if t < n:
            h, q0, dg = units[t]
            qT = qT_ref[h, :, q0:q0 + tqu]
        if t >= 1:
            hb, qb, _ = units[t - 1]
            m_b, alpha_b = m_new, alpha
        for r0 in range(0, tk, rc):
            if t < n:
                s = biased(jnp.dot(k_ref[h, r0:r0 + rc, :], qT, preferred_element_type=F32), h, q0, r0, dg)
                s_ref[t % 2, r0:r0 + rc, :] = s
                cm = jnp.max(s.reshape(rc // SUBLANE, SUBLANE, tqu), axis=0)
                mx = cm if mx is None else jnp.maximum(mx, cm)
            if t >= 1:
                p_ref[(t - 1) % 2, r0:r0 + rc, :] = jnp.exp2(s_ref[(t - 1) % 2, r0:r0 + rc, :] - m_b).astype(BF16)
        if t >= 1:
            pv = jnp.dot(vT_ref[hb], p_ref[(t - 1) % 2], preferred_element_type=F32)
            acc_ref[hb, :, qb:qb + tqu] = alpha_b * acc_ref[hb, :, qb:qb + tqu] + pv
        if t < n:
            m_prev = m_ref[h, :, q0:q0 + tqu]
            m_new = jnp.maximum(m_prev, jnp.max(mx, axis=0, keepdims=True))
            alpha = jnp.exp2(m_prev - m_new)
            m_ref[h, :, q0:q0 + tqu] = m_new


def _attn_init(m_ref, acc_ref):
    m_ref[...] = jnp.full(m_ref.shape, NEG, F32)
    acc_ref[...] = jnp.zeros(acc_ref.shape, F32)


def _attn_finish(o_ref, acc_ref, nh):
    hd = HEAD_DIM
    out = lambda h: acc_ref[h, :hd, :] / acc_ref[h, hd:hd + 1, :]
    for h in range(0, nh, 2):
        o_ref[:, h * hd:(h + 2) * hd] = jnp.concatenate([out(h), out(h + 1)], axis=0).T.astype(o_ref.dtype)


def _attn_kernel(qi_ref, kj_ref, *refs, mode, nh, tq, tk, nsub):
    if mode == "moba":
        qT_ref, k_ref, vT_ref, sb_ref, o_ref, m_ref, acc_ref, s_ref, p_ref, cb_ref = refs
    elif mode == "causal":
        qT_ref, k_ref, vT_ref, o_ref, m_ref, acc_ref, s_ref, p_ref, cb_ref = refs
    else:
        qT_ref, k_ref, vT_ref, o_ref, m_ref, acc_ref, s_ref, p_ref = refs
    p = pl.program_id(0)
    i = qi_ref[p]
    j = kj_ref[p]
    causal = mode != "none"
    tqu = tq // nsub

    if causal:
        @pl.when(p == 0)
        def _():
            row = lax.broadcasted_iota(jnp.int32, (tk, tqu), 0)
            col = lax.broadcasted_iota(jnp.int32, (tk, tqu), 1)
            cb_ref[...] = jnp.where(row <= col, 0.0, NEG)

    @pl.when(j == 0)
    def _():
        _attn_init(m_ref, acc_ref)

    def biased(s, h, q0, r0, diag):
        if mode == "moba":
            b0 = j * (tk // MOBA_BLOCK) + r0 // MOBA_BLOCK
            s = jnp.concatenate([s[u * MOBA_BLOCK:(u + 1) * MOBA_BLOCK] + sb_ref[h, pl.ds(b0 + u, 1), q0:q0 + tqu]
                                 for u in range(s.shape[0] // MOBA_BLOCK)], axis=0)
        if diag:
            s = s + cb_ref[r0:r0 + s.shape[0], :]
        return s

    def process(first):
        units = [(h, q * tqu, q == first) for h in range(nh) for q in range(max(first, 0), nsub)]
        _heads_update(units, qT_ref, k_ref, vT_ref, s_ref, p_ref, m_ref, acc_ref, biased)

    if causal:
        d = j - nsub * i
        pl.when(d < 0)(lambda: process(-1))
        for q in range(nsub):
            pl.when(d == q)(functools.partial(process, q))
        last = d == nsub - 1
    else:
        process(-1)
        last = j == 0

    @pl.when(last)
    def _():
        _attn_finish(o_ref, acc_ref, nh)


def _attention(mode, qT, k, vT, extra, tq, tk, nsub):
    nh, _, S = qT.shape
    hd = HEAD_DIM
    nq = S // tq
    if mode == "none":
        pairs = [(i, 0) for i in range(nq)]
    else:
        assert tq == nsub * tk
        pairs = [(i, j) for i in range(nq) for j in range(nsub * (i + 1))]
    qi = jnp.asarray(np.array([a for a, _ in pairs], np.int32))
    kj = jnp.asarray(np.array([b for _, b in pairs], np.int32))
    in_specs = [pl.BlockSpec((nh, KPAD, tq), lambda p, qi, kj: (0, 0, qi[p])),
                pl.BlockSpec((nh, tk, KPAD), lambda p, qi, kj: (0, kj[p], 0)),
                pl.BlockSpec((nh, VROWS, tk), lambda p, qi, kj: (0, 0, kj[p]))]
    if mode == "moba":
        nb = extra[0].shape[1]
        in_specs += [pl.BlockSpec((nh, nb, tq), lambda p, qi, kj: (0, 0, qi[p]))]
    tqu = tq // nsub
    scratch = [pltpu.VMEM((nh, 1, tq), F32), pltpu.VMEM((nh, VROWS, tq), F32),
               pltpu.VMEM((2, tk, tqu), F32), pltpu.VMEM((2, tk, tqu), BF16)]
    if mode != "none":
        scratch.append(pltpu.VMEM((tk, tqu), F32))
    return pl.pallas_call(
        functools.partial(_attn_kernel, mode=mode, nh=nh, tq=tq, tk=tk, nsub=nsub),
        grid_spec=pltpu.PrefetchScalarGridSpec(
            num_scalar_prefetch=2,
            grid=(len(pairs),),
            in_specs=in_specs,
            out_specs=pl.BlockSpec((tq, nh * hd), lambda p, qi, kj: (qi[p], 0)),
            scratch_shapes=scratch),
        out_shape=jax.ShapeDtypeStruct((S, nh * hd), BF16),
        compiler_params=_cparams(),
        name="attn_" + mode,
    )(qi, kj, qT, k, vT, *extra)


def _gate_kernel(qT_ref, km_ref, o_ref, *, nh, nb, tq):
    i = pl.program_id(0)
    n = lax.broadcasted_iota(jnp.int32, (nb, tq), 0)
    own = (i * tq + lax.broadcasted_iota(jnp.int32, (nb, tq), 1)) // MOBA_BLOCK
    ninf = jnp.float32(-jnp.inf)

    def body(h, carry):
        g = jnp.dot(km_ref[h], qT_ref[h], preferred_element_type=F32)
        g = jnp.where(n < own, g, ninf)
        keep = n == own
        for _ in range(MOBA_TOPK):
            mx = jnp.max(g, axis=0, keepdims=True)
            first = jnp.min(jnp.where((g == mx) & (mx > ninf), n, nb), axis=0, keepdims=True)
            pick = n == first
            keep = keep | pick
            g = jnp.where(pick, ninf, g)
        o_ref[h] = jnp.where(keep, 0.0, NEG)
        return carry

    lax.fori_loop(0, nh, body, 0)


def _moba_gate(qT, kmean, tq):
    nh, _, S = qT.shape
    nb = kmean.shape[1]
    return pl.pallas_call(
        functools.partial(_gate_kernel, nh=nh, nb=nb, tq=tq),
        grid=(S // tq,),
        in_specs=[pl.BlockSpec((nh, KPAD, tq), lambda i: (0, 0, i)), _resident((nh, nb, KPAD))],
        out_specs=pl.BlockSpec((nh, nb, tq), lambda i: (0, 0, i)),
        out_shape=jax.ShapeDtypeStruct((nh, nb, S), F32),
        compiler_params=_cparams(),
        name="moba_gate",
    )(qT, kmean)


def _dsa_kernel(qi_ref, kj_ref, ph_ref, ikb_ref, kvb_ref, nk_ref,
                iqT_ref, ik_ref, w_ref, qT_ref, k_ref, vT_ref, o_ref,
                hi_ref, lo_ref, thi_ref, tlo_ref, cut_ref, m_ref, acc_ref, s_ref, p_ref, bias_ref,
                *, nih, nh, tq, tk, topk, seq, chunk):
    p = pl.program_id(0)
    i = qi_ref[p]
    j = kj_ref[p]
    phase = ph_ref[p]
    nkb = nk_ref[p]
    I16, I32 = jnp.int16, jnp.int32
    lo_min = -(2 ** 15)
    one, zero = jnp.ones((), I16), jnp.zeros((), I16)
    row16 = lambda x: x.astype(I16)

    @pl.when(phase == 0)
    def _scores():
        ik = ik_ref[...]
        sc = jnp.zeros((tk, tq), F32)
        for h in range(nih):
            d = jnp.dot(ik, iqT_ref[h], preferred_element_type=F32)
            sc = sc + w_ref[h:h + 1, :] * jnp.maximum(d, 0.0)
        b = lax.bitcast_convert_type(sc, I32)
        key = b ^ ((b >> 31) & 0x7FFFFFFF)
        row = j * tk + lax.broadcasted_iota(I32, (tk, tq), 0)
        col = i * tq + lax.broadcasted_iota(I32, (tk, tq), 1)
        key = jnp.where(row <= col, key, INT_MIN)
        rows = pl.ds(pl.multiple_of(j * tk, tk), tk)
        hi_ref[rows, :] = (key >> 16).astype(I16)
        lo_ref[rows, :] = ((key & 0xFFFF) + lo_min).astype(I16)

    def count(hit):
        def body(c, acc):
            r0 = pl.multiple_of(c * chunk, chunk)
            h = hit(hi_ref[pl.ds(r0, chunk), :], lo_ref[pl.ds(r0, chunk), :], r0)
            part = h[0:2 * SUBLANE]
            for r in range(2 * SUBLANE, chunk, 2 * SUBLANE):
                part = part + h[r:r + 2 * SUBLANE]
            return acc + part.astype(I32)
        acc = lax.fori_loop(0, nkb * (tk // chunk), body, jnp.zeros((2 * SUBLANE, tq), I32))
        return jnp.sum(acc, axis=0, keepdims=True)

    def kth_largest(pick, k):
        c0 = count(lambda hi, lo, r0: jnp.where(pick(hi, lo) >= zero, one, zero))
        t = jnp.where(c0 >= k, 0, lo_min)

        def bit_body(it, t):
            cand = t | jnp.left_shift(jnp.int32(1), 14 - it)
            c16 = row16(cand)
            cnt = count(lambda hi, lo, r0: jnp.where(pick(hi, lo) >= c16, one, zero))
            return jnp.where(cnt >= k, cand, t)

        return lax.fori_loop(0, 15, bit_body, t)

    @pl.when((phase == 1) & (j == 0))
    def _select():
        thi = kth_largest(lambda hi, lo: hi, topk)
        thi16 = row16(thi)
        above = count(lambda hi, lo, r0: jnp.where(hi > thi16, one, zero))
        need = topk - above

        def mark(c, carry):
            rows = pl.ds(pl.multiple_of(c * chunk, chunk), chunk)
            lo_ref[rows, :] = jnp.where(hi_ref[rows, :] == thi16, lo_ref[rows, :], jnp.full((), lo_min, I16))
            return carry

        lax.fori_loop(0, nkb * (tk // chunk), mark, 0)
        tlo = kth_largest(lambda hi, lo: lo, need)
        tlo = jnp.where((thi == lo_min) & (tlo == lo_min), lo_min + 1, tlo)
        tlo16 = row16(tlo)
        thi_ref[...] = thi
        tlo_ref[...] = tlo
        cut_ref[...] = jnp.full((1, tq), seq, I32)
        in_bucket = lambda hi, x: jnp.where(hi == thi16, x, zero)
        cge = above + count(lambda hi, lo, r0: in_bucket(hi, jnp.where(lo >= tlo16, one, zero)))

        @pl.when(jnp.max(cge) > topk)
        def _ties():
            need_tie = need - count(lambda hi, lo, r0: jnp.where(lo > tlo16, one, zero))
            nbits = max(1, int(seq - 1).bit_length())

            def pos_body(it, x):
                cand = x + jnp.left_shift(jnp.int32(1), nbits - 1 - it)
                before = lambda r0: jnp.where(r0 + lax.broadcasted_iota(I32, (chunk, tq), 0) < cand, 1, 0).astype(I16)
                c = count(lambda hi, lo, r0: in_bucket(hi, jnp.where(lo == tlo16, before(r0), zero)))
                return jnp.where(c < need_tie, cand, x)

            x = lax.fori_loop(0, nbits, pos_body, jnp.zeros((1, tq), I32))
            cut_ref[...] = jnp.where(cge > topk, x, seq)

        _attn_init(m_ref, acc_ref)

    @pl.when(phase == 1)
    def _attend():
        rows = pl.ds(pl.multiple_of(j * tk, tk), tk)
        hi, lo = hi_ref[rows, :], lo_ref[rows, :]
        thi16, tlo16 = row16(thi_ref[...]), row16(tlo_ref[...])
        kept = jnp.where(j * tk + lax.broadcasted_iota(I32, (tk, tq), 0) <= cut_ref[...], 1, 0).astype(I16)
        tie = jnp.where(lo == tlo16, kept, zero)
        sel = jnp.where(hi > thi16, one, jnp.where(hi == thi16, jnp.where(lo > tlo16, one, tie), zero))
        bias_ref[...] = jnp.where(sel.astype(I32) > 0, 0.0, NEG)

        def biased(s, h, q0, r0, diag):
            return s + bias_ref[r0:r0 + s.shape[0], :]

        units = [(h, 0, False) for h in range(nh)]
        _heads_update(units, qT_ref, k_ref, vT_ref, s_ref, p_ref, m_ref, acc_ref, biased)

        @pl.when(j == nkb - 1)
        def _():
            _attn_finish(o_ref, acc_ref, nh)


def _dsa_attention(iqT, ik, wT, qT, k, vT, tq, tk, topk):
    nh, _, S = qT.shape
    hd = HEAD_DIM
    nih = iqT.shape[0]
    steps = []
    for i in range(S // tq):
        nkb = (i * tq + tq - 1) // tk + 1
        steps += [(i, j, 0, j, 0, nkb) for j in range(nkb)]
        steps += [(i, j, 1, nkb - 1, j, nkb) for j in range(nkb)]
    tab = [jnp.asarray(np.array([s[c] for s in steps], np.int32)) for c in range(6)]
    chunk = min(tk, 512)
    qmap = lambda p, qi, kj, ph, ikb, kvb, nk: (0, 0, qi[p])
    return pl.pallas_call(
        functools.partial(_dsa_kernel, nih=nih, nh=nh, tq=tq, tk=tk, topk=topk, seq=S, chunk=chunk),
        grid_spec=pltpu.PrefetchScalarGridSpec(
            num_scalar_prefetch=6,
            grid=(len(steps),),
            in_specs=[pl.BlockSpec((nih, KPAD, tq), qmap),
                      pl.BlockSpec((tk, KPAD), lambda p, qi, kj, ph, ikb, kvb, nk: (ikb[p], 0)),
                      pl.BlockSpec((nih, tq), lambda p, qi, kj, ph, ikb, kvb, nk: (0, qi[p])),
                      pl.BlockSpec((nh, KPAD, tq), qmap),
                      pl.BlockSpec((nh, tk, KPAD), lambda p, qi, kj, ph, ikb, kvb, nk: (0, kvb[p], 0)),
                      pl.BlockSpec((nh, VROWS, tk), lambda p, qi, kj, ph, ikb, kvb, nk: (0, 0, kvb[p]))],
            out_specs=pl.BlockSpec((tq, nh * hd), lambda p, qi, kj, ph, ikb, kvb, nk: (qi[p], 0)),
            scratch_shapes=[pltpu.VMEM((S, tq), jnp.int16), pltpu.VMEM((S, tq), jnp.int16),
                            pltpu.VMEM((1, tq), jnp.int32), pltpu.VMEM((1, tq), jnp.int32),
                            pltpu.VMEM((1, tq), jnp.int32),
                            pltpu.VMEM((nh, 1, tq), F32), pltpu.VMEM((nh, VROWS, tq), F32),
                            pltpu.VMEM((2, tk, tq), F32), pltpu.VMEM((2, tk, tq), BF16),
                            pltpu.VMEM((tk, tq), F32)]),
        out_shape=jax.ShapeDtypeStruct((S, nh * hd), BF16),
        compiler_params=_cparams(),
        name="dsa",
    )(*tab, iqT, ik, wT, qT, k, vT)


def kernel(x, mem, ffn1_norm, ffn1_w_in, ffn1_w_out, mix_norm, even_w_in, even_b_forget, fox_q_norm, fox_k_norm, dsa_q_norm, dsa_k_norm, odd_w_in, moba_q_norm, moba_k_norm, mix_w_out, mem_norm_x, mem_norm_m, mem_wq, mem_wkv, mem_q_norm, mem_k_norm, mem_wo, ffn2_norm, ffn2_w_in, ffn2_w_out):
    _, S, D = x.shape
    M = mem.shape[1]
    depth = ffn1_norm.shape[0]
    tm = min(512, S)
    ta = min(512, S)
    nsub = 2 if S % (2 * ta) == 0 else 1
    tdq = min(512, S)
    pos = jnp.arange(S, dtype=F32)
    inv = ROPE_THETA ** (-jnp.arange(0, ROT_DIM, 2, dtype=F32) / ROT_DIM)
    ang = inv[:, None] * pos[None, :]
    rope = (jnp.cos(ang), jnp.sin(ang))
    W8 = 8 * HEAD_DIM
    zcols = lambda n: jnp.zeros((D, n), F32)

    h = x[0]
    m0 = mem[0]
    for layer in range(depth):
        h = _ffn(h, ffn1_norm[layer], ffn1_w_in[layer].astype(BF16), ffn1_w_out[layer].astype(BF16), tm)
        if layer % 2 == 0:
            e = layer // 2
            o = np.cumsum([0, W8, W8, W8, FOX_HEADS, W8, W8, W8, IDX_HEADS * IDX_DIM, IDX_DIM, IDX_HEADS])
            w = even_w_in[e]
            seg = lambda n: w[:, o[n]:o[n + 1]]
            w = jnp.concatenate([seg(0), seg(1), seg(2), seg(4), seg(5), seg(6), seg(7),
                                 seg(8), zcols(KPAD - IDX_DIM),
                                 seg(3), seg(9), zcols(LANE - FOX_HEADS - IDX_HEADS)], axis=1).astype(BF16)
            plan = [("q", 0, 8, 0, 0, False, 0), ("k", W8, 8, 0, 1, False, 1), ("v", 2 * W8, 8, 0, None, False, 2),
                    ("q", 3 * W8, 8, 0, 2, True, 3), ("k", 4 * W8, 8, 0, 3, True, 4), ("v", 5 * W8, 8, 0, None, False, 5),
                    ("q", 6 * W8, 8, 0, None, True, 6), ("k", 7 * W8, 1, 0, None, True, 7)]
            gains = jnp.stack([fox_q_norm[e], fox_k_norm[e], dsa_q_norm[e], dsa_k_norm[e]])
            fqT, fk, fvT, dqT, dk, dvT, iqT, ik, wT = _heads_proj(
                h, mix_norm[layer], w, plan,
                [("q", 8), ("k", 8), ("v", 8), ("q", 8), ("k", 8), ("v", 8), ("q", 8), ("k", 1)],
                tm, gains=gains, rope=rope, b_forget=even_b_forget[e])
            fox = _attention("causal", fqT, fk, fvT, (), nsub * ta, ta, nsub)
            dsa = _dsa_attention(iqT, ik[0], wT, dqT, dk, dvT, tdq, ta, min(DSA_TOPK, S // 4))
            wo = mix_w_out[layer].astype(BF16)
            h = _outproj(h, [fox, dsa], [wo[:W8], wo[W8:]], tm)
        else:
            od = layer // 2
            w = odd_w_in[od].astype(BF16)
            plan = [("q", 0, 8, 0, 0, True, 0), ("q", W8, 8, 8, 0, True, 0),
                    ("k", 2 * W8, 8, 0, 1, True, 1), ("k", 3 * W8, 8, 8, 1, True, 1),
                    ("v", 4 * W8, 8, 0, None, False, 2), ("v", 5 * W8, 8, 8, None, False, 2)]
            gains = jnp.stack([moba_q_norm[od], moba_k_norm[od]])
            qT, k, vT, km = _heads_proj(h, mix_norm[layer], w, plan, [("q", 16), ("k", 16), ("v", 16)],
                                        tm, gains=gains, rope=rope, kmean=True)
            nbt = tm // MOBA_BLOCK
            km = jnp.transpose(km[..., :nbt], (1, 0, 3, 2)).reshape(MOBA_HEADS, S // MOBA_BLOCK, HEAD_DIM)
            km = jnp.pad(km, ((0, 0), (0, 0), (0, KPAD - HEAD_DIM))).astype(BF16)
            selb = _moba_gate(qT, km, ta)
            moba = _attention("moba", qT, k, vT, (selb,), nsub * ta, ta, nsub)
            h = _outproj(h, [moba], [mix_w_out[layer].astype(BF16)], tm)
        mw = MEM_HEADS * HEAD_DIM
        (mqT,) = _heads_proj(h, mem_norm_x[layer], mem_wq[layer].astype(BF16),
                             [("q", 0, MEM_HEADS, 0, 0, False, 0)], [("q", MEM_HEADS)], tm,
                             gains=mem_q_norm[layer][None])
        mk, mvT = _heads_proj(m0, mem_norm_m[layer], mem_wkv[layer].astype(BF16),
                              [("k", 0, MEM_HEADS, 0, 0, False, 0), ("v", mw, MEM_HEADS, 0, None, False, 1)],
                              [("k", MEM_HEADS), ("v", MEM_HEADS)], M, gains=mem_k_norm[layer][None])
        ma = _attention("none", mqT, mk, mvT, (), nsub * ta, M, nsub)
        h = _outproj(h, [ma], [mem_wo[layer].astype(BF16)], tm)
        h = _ffn(h, ffn2_norm[layer], ffn2_w_in[layer].astype(BF16), ffn2_w_out[layer].astype(BF16), tm)
    return h[None]
```

```python
import functools

import numpy as np
import jax
import jax.numpy as jnp
from jax import lax
from jax.experimental import pallas as pl
from jax.experimental.pallas import tpu as pltpu

HEAD_DIM = 64
ROT_DIM = HEAD_DIM // 4
ROPE_THETA = 500000.0
FOX_HEADS = 8
DSA_HEADS = 8
MOBA_HEADS = 16
IDX_HEADS = 8
IDX_DIM = 64
DSA_TOPK = 256
MOBA_BLOCK = 256
MOBA_TOPK = 3
MEM_HEADS = 4
RMS_EPS = 1e-6

NEG = -1e30
INT_MIN = -(2 ** 31)
LANE = 128
SUBLANE = 8
KPAD = 128
VROWS = HEAD_DIM + 16
ATTN_ROWS = 256
VMEM_LIMIT = 56 * 1024 * 1024
LOG2E = 1.4426950408889634
QSCALE = HEAD_DIM ** -0.5 * LOG2E

F32 = jnp.float32
BF16 = jnp.bfloat16


def _cparams(n_axes=1):
    return pltpu.CompilerParams(dimension_semantics=("arbitrary",) * n_axes, vmem_limit_bytes=VMEM_LIMIT)


def _resident(shape):
    nd = len(shape)
    return pl.BlockSpec(shape, lambda *_: (0,) * nd, pipeline_mode=pl.Buffered(1))


def _rms_bf16(x, g):
    ms = jnp.mean(x * x, axis=-1, keepdims=True)
    return (x * lax.rsqrt(ms + RMS_EPS) * g).astype(BF16)


def _ffn_kernel(x_ref, g_ref, wa_ref, wb_ref, wo_ref, o_ref, *, fc):
    x = x_ref[...]
    xn = _rms_bf16(x, g_ref[...])
    acc = jnp.zeros(x.shape, F32)
    for c in range(wa_ref.shape[1] // fc):
        a = jnp.dot(xn, wa_ref[:, c * fc:(c + 1) * fc], preferred_element_type=F32)
        b = jnp.dot(xn, wb_ref[:, c * fc:(c + 1) * fc], preferred_element_type=F32)
        gate = (a * jax.nn.sigmoid(a) * b).astype(BF16)
        acc = acc + jnp.dot(gate, wo_ref[c * fc:(c + 1) * fc, :], preferred_element_type=F32)
    o_ref[...] = x + 0.5 * acc


def _ffn(h, g, w_in, w_out, tm):
    S, D = h.shape
    Fh = w_out.shape[0]
    return pl.pallas_call(
        functools.partial(_ffn_kernel, fc=256),
        grid=(S // tm,),
        in_specs=[pl.BlockSpec((tm, D), lambda i: (i, 0)),
                  _resident((1, D)),
                  pl.BlockSpec((D, Fh), lambda i: (0, 0), pipeline_mode=pl.Buffered(1)),
                  pl.BlockSpec((D, Fh), lambda i: (0, 1), pipeline_mode=pl.Buffered(1)),
                  _resident((Fh, D))],
        out_specs=pl.BlockSpec((tm, D), lambda i: (i, 0)),
        out_shape=jax.ShapeDtypeStruct((S, D), F32),
        compiler_params=_cparams(),
        name="ffn",
    )(h, g.reshape(1, D), w_in, w_in, w_out)


def _outproj_kernel(*refs, n):
    r_ref, o_ref = refs[0], refs[-1]
    acc = r_ref[...]
    for k in range(n):
        acc = acc + jnp.dot(refs[1 + k][...], refs[1 + n + k][...], preferred_element_type=F32)
    o_ref[...] = acc


def _outproj(res, xs, ws, tm):
    S, D = res.shape
    n = len(xs)
    in_specs = [pl.BlockSpec((tm, D), lambda i: (i, 0))]
    in_specs += [pl.BlockSpec((tm, x.shape[1]), lambda i: (i, 0)) for x in xs]
    in_specs += [_resident(w.shape) for w in ws]
    return pl.pallas_call(
        functools.partial(_outproj_kernel, n=n),
        grid=(S // tm,),
        in_specs=in_specs,
        out_specs=pl.BlockSpec((tm, D), lambda i: (i, 0)),
        out_shape=jax.ShapeDtypeStruct((S, D), F32),
        compiler_params=_cparams(),
        name="outproj",
    )(res, *xs, *ws)


def _lanes(ref, idx, tm):
    return jnp.concatenate([ref[idx]] * (tm // LANE), axis=1)


def _bf16_split3(x):
    a = x.astype(BF16).astype(F32)
    b = (x - a).astype(BF16).astype(F32)
    return a, b, x - a - b


def _heads_kernel(*refs, plan, tm, n_gain, rope, fox, kmean):
    it = iter(refs)
    x_ref, g_ref, w_ref = next(it), next(it), next(it)
    gn_ref = next(it) if n_gain else None
    cos = sin = None
    if rope:
        cos, sin = next(it)[...], next(it)[...]
    bf_ref = next(it) if fox else None
    n_out = 1 + max(p[6] for p in plan)
    outs = [next(it) for _ in range(n_out)]
    wt_ref = next(it) if fox else None
    km_ref = next(it) if kmean else None
    carry_ref = next(it) if fox else None

    xn = _rms_bf16(x_ref[...], g_ref[...])
    proj_t = lambda col, width: jnp.dot(xn, w_ref[:, col:col + width], preferred_element_type=F32).T
    zeros_pad = jnp.zeros((KPAD - HEAD_DIM, tm), F32)
    half = ROT_DIM // 2

    fq_rows = fk_rows = None
    if fox:
        @pl.when(pl.program_id(0) == 0)
        def _():
            carry_ref[...] = jnp.zeros(carry_ref.shape, F32)

        gT = proj_t(w_ref.shape[1] - LANE, LANE)
        xg = gT[0:FOX_HEADS] + _lanes(bf_ref, slice(None), tm)
        lf = jnp.minimum(xg, 0.0) - jnp.log1p(jnp.exp(-jnp.abs(xg)))
        tri = jnp.where(lax.broadcasted_iota(jnp.int32, (tm, tm), 0)
                        <= lax.broadcasted_iota(jnp.int32, (tm, tm), 1), 1.0, 0.0).astype(BF16)
        cs = sum(jnp.dot(part.astype(BF16), tri, preferred_element_type=F32) for part in _bf16_split3(lf))
        fcum = (carry_ref[...] + cs) * LOG2E
        carry_ref[...] = carry_ref[...] + jnp.sum(lf, axis=1, keepdims=True)
        wt_ref[...] = gT[FOX_HEADS:FOX_HEADS + IDX_HEADS] * (IDX_HEADS ** -0.5 * IDX_DIM ** -0.5)
        rid = lax.broadcasted_iota(jnp.int32, (SUBLANE, tm), 0)
        fq_rows, fk_rows = [], []
        for h in range(FOX_HEADS):
            f1, f2, f3 = _bf16_split3(fcum[h:h + 1])
            sel = lambda a, b, c, one: jnp.where(rid == 0, a, jnp.where(rid == 1, b, jnp.where(rid == 2, c, one)))
            ones_lo = jnp.where(rid < 6, 1.0, 0.0)
            fq_rows.append(sel(f1, f2, f3, ones_lo))
            fk_rows.append(jnp.where(rid < 3, 1.0, jnp.where(rid == 3, -f1, jnp.where(rid == 4, -f2,
                           jnp.where(rid == 5, -f3, 0.0)))))

    for kind, col, nheads, head0, gain, use_rope, out in plan:
        width = -(-nheads * HEAD_DIM // LANE) * LANE
        segT = proj_t(col, width)
        o_ref = outs[out]
        for hh in range(nheads):
            r = segT[hh * HEAD_DIM:(hh + 1) * HEAD_DIM]
            if gain is not None:
                r = r * lax.rsqrt(jnp.mean(r * r, axis=0, keepdims=True) + RMS_EPS) * _lanes(gn_ref, gain, tm)
            if use_rope:
                x1, x2 = r[:half], r[half:ROT_DIM]
                r = jnp.concatenate([x1 * cos - x2 * sin, x2 * cos + x1 * sin, r[ROT_DIM:]], axis=0)
            h = head0 + hh
            if kind == "v":
                o_ref[h] = jnp.concatenate([r, jnp.ones((VROWS - HEAD_DIM, tm), F32)], axis=0).astype(BF16)
                continue
            if kind == "q":
                r = r * QSCALE
            extra = None
            if fox and out == 0:
                extra = fq_rows[h]
            elif fox and out == 1:
                extra = fk_rows[h]
            if extra is None:
                padded = jnp.concatenate([r, zeros_pad], axis=0)
            else:
                padded = jnp.concatenate([r, extra, zeros_pad[SUBLANE:]], axis=0)
            if kind == "q":
                o_ref[h] = padded.astype(BF16)
            else:
                o_ref[h] = padded.T.astype(BF16)
                if kmean:
                    lane = lax.broadcasted_iota(jnp.int32, (HEAD_DIM, LANE), 1)
                    tile = jnp.zeros((HEAD_DIM, LANE), F32)
                    for b in range(tm // MOBA_BLOCK):
                        mb = jnp.sum(r[:, b * MOBA_BLOCK:(b + 1) * MOBA_BLOCK], axis=1, keepdims=True)
                        tile = jnp.where(lane == b, mb * (1.0 / MOBA_BLOCK), tile)
                    km_ref[0, h] = tile


def _heads_proj(x, g, w, plan, out_heads, tm, gains=None, rope=None, b_forget=None, kmean=False):
    S, D = x.shape
    N = w.shape[1]
    fox = b_forget is not None
    in_specs = [pl.BlockSpec((tm, D), lambda i: (i, 0)), _resident((1, D)), _resident((D, N))]
    args = [x, g.reshape(1, D), w]
    if gains is not None:
        gt = jnp.broadcast_to(gains[:, :, None], gains.shape + (LANE,)).astype(F32)
        in_specs.append(_resident(gt.shape))
        args.append(gt)
    if rope is not None:
        in_specs += [pl.BlockSpec((ROT_DIM // 2, tm), lambda i: (0, i))] * 2
        args += list(rope)
    if fox:
        in_specs.append(_resident((FOX_HEADS, LANE)))
        args.append(jnp.broadcast_to(b_forget[:, None], (FOX_HEADS, LANE)).astype(F32))
    out_specs, out_shapes = [], []
    for kind, nh in out_heads:
        if kind == "q":
            out_specs.append(pl.BlockSpec((nh, KPAD, tm), lambda i: (0, 0, i)))
            out_shapes.append(jax.ShapeDtypeStruct((nh, KPAD, S), BF16))
        elif kind == "k":
            out_specs.append(pl.BlockSpec((nh, tm, KPAD), lambda i: (0, i, 0)))
            out_shapes.append(jax.ShapeDtypeStruct((nh, S, KPAD), BF16))
        else:
            out_specs.append(pl.BlockSpec((nh, VROWS, tm), lambda i: (0, 0, i)))
            out_shapes.append(jax.ShapeDtypeStruct((nh, VROWS, S), BF16))
    if fox:
        out_specs.append(pl.BlockSpec((IDX_HEADS, tm), lambda i: (0, i)))
        out_shapes.append(jax.ShapeDtypeStruct((IDX_HEADS, S), F32))
    if kmean:
        nhk = out_heads[1][1]
        out_specs.append(pl.BlockSpec((1, nhk, HEAD_DIM, LANE), lambda i: (i, 0, 0, 0)))
        out_shapes.append(jax.ShapeDtypeStruct((S // tm, nhk, HEAD_DIM, LANE), F32))
    return pl.pallas_call(
        functools.partial(_heads_kernel, plan=tuple(plan), tm=tm, n_gain=0 if gains is None else gains.shape[0],
                          rope=rope is not None, fox=fox, kmean=kmean),
        grid=(S // tm,),
        in_specs=in_specs,
        out_specs=out_specs,
        out_shape=out_shapes,
        scratch_shapes=[pltpu.VMEM((FOX_HEADS, 1), F32)] if fox else [],
        compiler_params=_cparams(),
        name="heads_proj",
    )(*args)


def _heads_update(units, qT_ref, k_ref, vT_ref, s_ref, p_ref, m_ref, acc_ref, biased):
    _, tk, tqu = s_ref.shape
    rc = min(ATTN_ROWS, tk)
    n = len(units)
    m_new = alpha = None
    for t in range(n + 1):
        mx = None
        if t < n:
            h, q0, dg = units[t]
            qT = qT_ref[h, :, q0:q0 + tqu]
        if t >= 1:
            hb, qb, _ = units[t - 1]
            m_b, alpha_b = m_new, alpha
        for r0 in range(0, tk, rc):
            if t < n:
                s = biased(jnp.dot(k_ref[h, r0:r0 + rc, :], qT, preferred_element_type=F32), h, q0, r0, dg)
                s_ref[t % 2, r0:r0 + rc, :] = s
                cm = jnp.max(s.reshape(rc // SUBLANE, SUBLANE, tqu), axis=0)
                mx = cm if mx is None else jnp.maximum(mx, cm)
            if t >= 1:
                p_ref[(t - 1) % 2, r0:r0 + rc, :] = jnp.exp2(s_ref[(t - 1) % 2, r0:r0 + rc, :] - m_b).astype(BF16)
        if t >= 1:
            pv = jnp.dot(vT_ref[hb], p_ref[(t - 1) % 2], preferred_element_type=F32)
            acc_ref[hb, :, qb:qb + tqu] = alpha_b * acc_ref[hb, :, qb:qb + tqu] + pv
        if t < n:
            m_prev = m_ref[h, :, q0:q0 + tqu]
            m_new = jnp.maximum(m_prev, jnp.max(mx, axis=0, keepdims=True))
            alpha = jnp.exp2(m_prev - m_new)
            m_ref[h, :, q0:q0 + tqu] = m_new


def _attn_init(m_ref, acc_ref):
    m_ref[...] = jnp.full(m_ref.shape, NEG, F32)
    acc_ref[...] = jnp.zeros(acc_ref.shape, F32)


def _attn_finish(o_ref, acc_ref, nh):
    hd = HEAD_DIM
    out = lambda h: acc_ref[h, :hd, :] / acc_ref[h, hd:hd + 1, :]
    for h in range(0, nh, 2):
        o_ref[:, h * hd:(h + 2) * hd] = jnp.concatenate([out(h), out(h + 1)], axis=0).T.astype(o_ref.dtype)


def _attn_kernel(qi_ref, kj_ref, *refs, mode, nh, tq, tk, nsub):
    if mode == "moba":
        qT_ref, k_ref, vT_ref, sb_ref, o_ref, m_ref, acc_ref, s_ref, p_ref, cb_ref = refs
    elif mode == "causal":
        qT_ref, k_ref, vT_ref, o_ref, m_ref, acc_ref, s_ref, p_ref, cb_ref = refs
    else:
        qT_ref, k_ref, vT_ref, o_ref, m_ref, acc_ref, s_ref, p_ref = refs
    p = pl.program_id(0)
    i = qi_ref[p]
    j = kj_ref[p]
    causal = mode != "none"
    tqu = tq // nsub

    if causal:
        @pl.when(p == 0)
        def _():
            row = lax.broadcasted_iota(jnp.int32, (tk, tqu), 0)
            col = lax.broadcasted_iota(jnp.int32, (tk, tqu), 1)
            cb_ref[...] = jnp.where(row <= col, 0.0, NEG)

    @pl.when(j == 0)
    def _():
        _attn_init(m_ref, acc_ref)

    def biased(s, h, q0, r0, diag):
        if mode == "moba":
            s = s + sb_ref[h, pl.ds(j * (tk // MOBA_BLOCK) + r0 // MOBA_BLOCK, 1), q0:q0 + tqu]
        if diag:
            s = s + cb_ref[r0:r0 + s.shape[0], :]
        return s

    def process(first):
        units = [(h, q * tqu, q == first) for h in range(nh) for q in range(max(first, 0), nsub)]
        _heads_update(units, qT_ref, k_ref, vT_ref, s_ref, p_ref, m_ref, acc_ref, biased)

    if causal:
        d = j - nsub * i
        pl.when(d < 0)(lambda: process(-1))
        for q in range(nsub):
            pl.when(d == q)(functools.partial(process, q))
        last = d == nsub - 1
    else:
        process(-1)
        last = j == 0

    @pl.when(last)
    def _():
        _attn_finish(o_ref, acc_ref, nh)


def _attention(mode, qT, k, vT, extra, tq, tk, nsub):
    nh, _, S = qT.shape
    hd = HEAD_DIM
    nq = S // tq
    if mode == "none":
        pairs = [(i, 0) for i in range(nq)]
    else:
        assert tq == nsub * tk
        pairs = [(i, j) for i in range(nq) for j in range(nsub * (i + 1))]
    qi = jnp.asarray(np.array([a for a, _ in pairs], np.int32))
    kj = jnp.asarray(np.array([b for _, b in pairs], np.int32))
    in_specs = [pl.BlockSpec((nh, KPAD, tq), lambda p, qi, kj: (0, 0, qi[p])),
                pl.BlockSpec((nh, tk, KPAD), lambda p, qi, kj: (0, kj[p], 0)),
                pl.BlockSpec((nh, VROWS, tk), lambda p, qi, kj: (0, 0, kj[p]))]
    if mode == "moba":
        nb = extra[0].shape[1]
        in_specs += [pl.BlockSpec((nh, nb, tq), lambda p, qi, kj: (0, 0, qi[p]))]
    tqu = tq // nsub
    scratch = [pltpu.VMEM((nh, 1, tq), F32), pltpu.VMEM((nh, VROWS, tq), F32),
               pltpu.VMEM((2, tk, tqu), F32), pltpu.VMEM((2, tk, tqu), BF16)]
    if mode != "none":
        scratch.append(pltpu.VMEM((tk, tqu), F32))
    return pl.pallas_call(
        functools.partial(_attn_kernel, mode=mode, nh=nh, tq=tq, tk=tk, nsub=nsub),
        grid_spec=pltpu.PrefetchScalarGridSpec(
            num_scalar_prefetch=2,
            grid=(len(pairs),),
            in_specs=in_specs,
            out_specs=pl.BlockSpec((tq, nh * hd), lambda p, qi, kj: (qi[p], 0)),
            scratch_shapes=scratch),
        out_shape=jax.ShapeDtypeStruct((S, nh * hd), BF16),
        compiler_params=_cparams(),
        name="attn_" + mode,
    )(qi, kj, qT, k, vT, *extra)


def _gate_kernel(qT_ref, km_ref, o_ref, *, nh, nb, tq):
    i = pl.program_id(0)
    n = lax.broadcasted_iota(jnp.int32, (nb, tq), 0)
    own = (i * tq + lax.broadcasted_iota(jnp.int32, (nb, tq), 1)) // MOBA_BLOCK
    ninf = jnp.float32(-jnp.inf)

    def body(h, carry):
        g = jnp.dot(km_ref[h], qT_ref[h], preferred_element_type=F32)
        g = jnp.where(n < own, g, ninf)
        keep = n == own
        for _ in range(MOBA_TOPK):
            mx = jnp.max(g, axis=0, keepdims=True)
            first = jnp.min(jnp.where((g == mx) & (mx > ninf), n, nb), axis=0, keepdims=True)
            pick = n == first
            keep = keep | pick
            g = jnp.where(pick, ninf, g)
        o_ref[h] = jnp.where(keep, 0.0, NEG)
        return carry

    lax.fori_loop(0, nh, body, 0, unroll=4)


def _moba_gate(qT, kmean, tq):
    nh, _, S = qT.shape
    nb = kmean.shape[1]
    return pl.pallas_call(
        functools.partial(_gate_kernel, nh=nh, nb=nb, tq=tq),
        grid=(S // tq,),
        in_specs=[pl.BlockSpec((nh, KPAD, tq), lambda i: (0, 0, i)), _resident((nh, nb, KPAD))],
        out_specs=pl.BlockSpec((nh, nb, tq), lambda i: (0, 0, i)),
        out_shape=jax.ShapeDtypeStruct((nh, nb, S), F32),
        compiler_params=_cparams(),
        name="moba_gate",
    )(qT, kmean)


def _dsa_kernel(qi_ref, kj_ref, ph_ref, ikb_ref, kvb_ref, nk_ref,
                iqT_ref, ik_ref, w_ref, qT_ref, k_ref, vT_ref, o_ref,
                hi_ref, lo_ref, thi_ref, tlo_ref, cut_ref, m_ref, acc_ref, s_ref, p_ref, bias_ref,
                *, nih, nh, tq, tk, topk, seq, chunk):
    p = pl.program_id(0)
    i = qi_ref[p]
    j = kj_ref[p]
    phase = ph_ref[p]
    nkb = nk_ref[p]
    I16, I32 = jnp.int16, jnp.int32
    lo_min = -(2 ** 15)
    one, zero = jnp.ones((), I16), jnp.zeros((), I16)
    row16 = lambda x: x.astype(I16)

    @pl.when(phase == 0)
    def _scores():
        ik = ik_ref[...]
        sc = jnp.zeros((tk, tq), F32)
        for h in range(nih):
            d = jnp.dot(ik, iqT_ref[h], preferred_element_type=F32)
            sc = sc + w_ref[h:h + 1, :] * jnp.maximum(d, 0.0)
        b = lax.bitcast_convert_type(sc, I32)
        key = b ^ ((b >> 31) & 0x7FFFFFFF)
        row = j * tk + lax.broadcasted_iota(I32, (tk, tq), 0)
        col = i * tq + lax.broadcasted_iota(I32, (tk, tq), 1)
        key = jnp.where(row <= col, key, INT_MIN)
        rows = pl.ds(pl.multiple_of(j * tk, tk), tk)
        hi_ref[rows, :] = (key >> 16).astype(I16)
        lo_ref[rows, :] = ((key & 0xFFFF) + lo_min).astype(I16)

    def count(hit):
        def body(c, acc):
            r0 = pl.multiple_of(c * chunk, chunk)
            h = hit(hi_ref[pl.ds(r0, chunk), :], lo_ref[pl.ds(r0, chunk), :], r0)
            part = h[0:2 * SUBLANE]
            for r in range(2 * SUBLANE, chunk, 2 * SUBLANE):
                part = part + h[r:r + 2 * SUBLANE]
            return acc + part.astype(I32)
        acc = lax.fori_loop(0, nkb * (tk // chunk), body, jnp.zeros((2 * SUBLANE, tq), I32))
        return jnp.sum(acc, axis=0, keepdims=True)

    def kth_largest(pick, k):
        c0 = count(lambda hi, lo, r0: jnp.where(pick(hi, lo) >= zero, one, zero))
        t = jnp.where(c0 >= k, 0, lo_min)

        def bit_body(it, t):
            cand = t | jnp.left_shift(jnp.int32(1), 14 - it)
            c16 = row16(cand)
            cnt = count(lambda hi, lo, r0: jnp.where(pick(hi, lo) >= c16, one, zero))
            return jnp.where(cnt >= k, cand, t)

        return lax.fori_loop(0, 15, bit_body, t)

    @pl.when((phase == 1) & (j == 0))
    def _select():
        thi = kth_largest(lambda hi, lo: hi, topk)
        thi16 = row16(thi)
        above = count(lambda hi, lo, r0: jnp.where(hi > thi16, one, zero))
        need = topk - above

        def mark(c, carry):
            rows = pl.ds(pl.multiple_of(c * chunk, chunk), chunk)
            lo_ref[rows, :] = jnp.where(hi_ref[rows, :] == thi16, lo_ref[rows, :], jnp.full((), lo_min, I16))
            return carry

        lax.fori_loop(0, nkb * (tk // chunk), mark, 0)
        tlo = kth_largest(lambda hi, lo: lo, need)
        tlo = jnp.where((thi == lo_min) & (tlo == lo_min), lo_min + 1, tlo)
        tlo16 = row16(tlo)
        thi_ref[...] = thi
        tlo_ref[...] = tlo
        cut_ref[...] = jnp.full((1, tq), seq, I32)
        in_bucket = lambda hi, x: jnp.where(hi == thi16, x, zero)
        cge = above + count(lambda hi, lo, r0: in_bucket(hi, jnp.where(lo >= tlo16, one, zero)))

        @pl.when(jnp.max(cge) > topk)
        def _ties():
            need_tie = need - count(lambda hi, lo, r0: jnp.where(lo > tlo16, one, zero))
            nbits = max(1, int(seq - 1).bit_length())

            def pos_body(it, x):
                cand = x + jnp.left_shift(jnp.int32(1), nbits - 1 - it)
                before = lambda r0: jnp.where(r0 + lax.broadcasted_iota(I32, (chunk, tq), 0) < cand, 1, 0).astype(I16)
                c = count(lambda hi, lo, r0: in_bucket(hi, jnp.where(lo == tlo16, before(r0), zero)))
                return jnp.where(c < need_tie, cand, x)

            x = lax.fori_loop(0, nbits, pos_body, jnp.zeros((1, tq), I32))
            cut_ref[...] = jnp.where(cge > topk, x, seq)

        _attn_init(m_ref, acc_ref)

    @pl.when(phase == 1)
    def _attend():
        rows = pl.ds(pl.multiple_of(j * tk, tk), tk)
        hi, lo = hi_ref[rows, :], lo_ref[rows, :]
        thi16, tlo16 = row16(thi_ref[...]), row16(tlo_ref[...])
        kept = jnp.where(j * tk + lax.broadcasted_iota(I32, (tk, tq), 0) <= cut_ref[...], 1, 0).astype(I16)
        tie = jnp.where(lo == tlo16, kept, zero)
        sel = jnp.where(hi > thi16, one, jnp.where(hi == thi16, jnp.where(lo > tlo16, one, tie), zero))
        bias_ref[...] = jnp.where(sel.astype(I32) > 0, 0.0, NEG)

        def biased(s, h, q0, r0, diag):
            return s + bias_ref[r0:r0 + s.shape[0], :]

        units = [(h, 0, False) for h in range(nh)]
        _heads_update(units, qT_ref, k_ref, vT_ref, s_ref, p_ref, m_ref, acc_ref, biased)

        @pl.when(j == nkb - 1)
        def _():
            _attn_finish(o_ref, acc_ref, nh)


def _dsa_attention(iqT, ik, wT, qT, k, vT, tq, tk, topk):
    nh, _, S = qT.shape
    hd = HEAD_DIM
    nih = iqT.shape[0]
    steps = []
    for i in range(S // tq):
        nkb = (i * tq + tq - 1) // tk + 1
        steps += [(i, j, 0, j, 0, nkb) for j in range(nkb)]
        steps += [(i, j, 1, nkb - 1, j, nkb) for j in range(nkb)]
    tab = [jnp.asarray(np.array([s[c] for s in steps], np.int32)) for c in range(6)]
    chunk = min(tk, 512)
    qmap = lambda p, qi, kj, ph, ikb, kvb, nk: (0, 0, qi[p])
    return pl.pallas_call(
        functools.partial(_dsa_kernel, nih=nih, nh=nh, tq=tq, tk=tk, topk=topk, seq=S, chunk=chunk),
        grid_spec=pltpu.PrefetchScalarGridSpec(
            num_scalar_prefetch=6,
            grid=(len(steps),),
            in_specs=[pl.BlockSpec((nih, KPAD, tq), qmap),
                      pl.BlockSpec((tk, KPAD), lambda p, qi, kj, ph, ikb, kvb, nk: (ikb[p], 0)),
                      pl.BlockSpec((nih, tq), lambda p, qi, kj, ph, ikb, kvb, nk: (0, qi[p])),
                      pl.BlockSpec((nh, KPAD, tq), qmap),
                      pl.BlockSpec((nh, tk, KPAD), lambda p, qi, kj, ph, ikb, kvb, nk: (0, kvb[p], 0)),
                      pl.BlockSpec((nh, VROWS, tk), lambda p, qi, kj, ph, ikb, kvb, nk: (0, 0, kvb[p]))],
            out_specs=pl.BlockSpec((tq, nh * hd), lambda p, qi, kj, ph, ikb, kvb, nk: (qi[p], 0)),
            scratch_shapes=[pltpu.VMEM((S, tq), jnp.int16), pltpu.VMEM((S, tq), jnp.int16),
                            pltpu.VMEM((1, tq), jnp.int32), pltpu.VMEM((1, tq), jnp.int32),
                            pltpu.VMEM((1, tq), jnp.int32),
                            pltpu.VMEM((nh, 1, tq), F32), pltpu.VMEM((nh, VROWS, tq), F32),
                            pltpu.VMEM((2, tk, tq), F32), pltpu.VMEM((2, tk, tq), BF16),
                            pltpu.VMEM((tk, tq), F32)]),
        out_shape=jax.ShapeDtypeStruct((S, nh * hd), BF16),
        compiler_params=_cparams(),
        name="dsa",
    )(*tab, iqT, ik, wT, qT, k, vT)


def kernel(x, mem, ffn1_norm, ffn1_w_in, ffn1_w_out, mix_norm, even_w_in, even_b_forget, fox_q_norm, fox_k_norm, dsa_q_norm, dsa_k_norm, odd_w_in, moba_q_norm, moba_k_norm, mix_w_out, mem_norm_x, mem_norm_m, mem_wq, mem_wkv, mem_q_norm, mem_k_norm, mem_wo, ffn2_norm, ffn2_w_in, ffn2_w_out):
    _, S, D = x.shape
    M = mem.shape[1]
    depth = ffn1_norm.shape[0]
    tm = min(512, S)
    ta = min(512, S)
    nsub = 2 if S % (2 * ta) == 0 else 1
    tdq = min(512, S)
    pos = jnp.arange(S, dtype=F32)
    inv = ROPE_THETA ** (-jnp.arange(0, ROT_DIM, 2, dtype=F32) / ROT_DIM)
    ang = inv[:, None] * pos[None, :]
    rope = (jnp.cos(ang), jnp.sin(ang))
    W8 = 8 * HEAD_DIM
    zcols = lambda n: jnp.zeros((D, n), F32)

    h = x[0]
    m0 = mem[0]
    for layer in range(depth):
        h = _ffn(h, ffn1_norm[layer], ffn1_w_in[layer].astype(BF16), ffn1_w_out[layer].astype(BF16), tm)
        if layer % 2 == 0:
            e = layer // 2
            o = np.cumsum([0, W8, W8, W8, FOX_HEADS, W8, W8, W8, IDX_HEADS * IDX_DIM, IDX_DIM, IDX_HEADS])
            w = even_w_in[e]
            seg = lambda n: w[:, o[n]:o[n + 1]]
            w = jnp.concatenate([seg(0), seg(1), seg(2), seg(4), seg(5), seg(6), seg(7),
                                 seg(8), zcols(KPAD - IDX_DIM),
                                 seg(3), seg(9), zcols(LANE - FOX_HEADS - IDX_HEADS)], axis=1).astype(BF16)
            plan = [("q", 0, 8, 0, 0, False, 0), ("k", W8, 8, 0, 1, False, 1), ("v", 2 * W8, 8, 0, None, False, 2),
                    ("q", 3 * W8, 8, 0, 2, True, 3), ("k", 4 * W8, 8, 0, 3, True, 4), ("v", 5 * W8, 8, 0, None, False, 5),
                    ("q", 6 * W8, 8, 0, None, True, 6), ("k", 7 * W8, 1, 0, None, True, 7)]
            gains = jnp.stack([fox_q_norm[e], fox_k_norm[e], dsa_q_norm[e], dsa_k_norm[e]])
            fqT, fk, fvT, dqT, dk, dvT, iqT, ik, wT = _heads_proj(
                h, mix_norm[layer], w, plan,
                [("q", 8), ("k", 8), ("v", 8), ("q", 8), ("k", 8), ("v", 8), ("q", 8), ("k", 1)],
                tm, gains=gains, rope=rope, b_forget=even_b_forget[e])
            fox = _attention("causal", fqT, fk, fvT, (), nsub * ta, ta, nsub)
            dsa = _dsa_attention(iqT, ik[0], wT, dqT, dk, dvT, tdq, ta, min(DSA_TOPK, S // 4))
            wo = mix_w_out[layer].astype(BF16)
            h = _outproj(h, [fox, dsa], [wo[:W8], wo[W8:]], tm)
        else:
            od = layer // 2
            w = odd_w_in[od].astype(BF16)
            plan = [("q", 0, 8, 0, 0, True, 0), ("q", W8, 8, 8, 0, True, 0),
                    ("k", 2 * W8, 8, 0, 1, True, 1), ("k", 3 * W8, 8, 8, 1, True, 1),
                    ("v", 4 * W8, 8, 0, None, False, 2), ("v", 5 * W8, 8, 8, None, False, 2)]
            gains = jnp.stack([moba_q_norm[od], moba_k_norm[od]])
            qT, k, vT, km = _heads_proj(h, mix_norm[layer], w, plan, [("q", 16), ("k", 16), ("v", 16)],
                                        tm, gains=gains, rope=rope, kmean=True)
            nbt = tm // MOBA_BLOCK
            km = jnp.transpose(km[..., :nbt], (1, 0, 3, 2)).reshape(MOBA_HEADS, S // MOBA_BLOCK, HEAD_DIM)
            km = jnp.pad(km, ((0, 0), (0, 0), (0, KPAD - HEAD_DIM))).astype(BF16)
            selb = _moba_gate(qT, km, ta)
            moba = _attention("moba", qT, k, vT, (selb,), nsub * ta, ta, nsub)
            h = _outproj(h, [moba], [mix_w_out[layer].astype(BF16)], tm)
        mw = MEM_HEADS * HEAD_DIM
        (mqT,) = _heads_proj(h, mem_norm_x[layer], mem_wq[layer].astype(BF16),
                             [("q", 0, MEM_HEADS, 0, 0, False, 0)], [("q", MEM_HEADS)], tm,
                             gains=mem_q_norm[layer][None])
        mk, mvT = _heads_proj(m0, mem_norm_m[layer], mem_wkv[layer].astype(BF16),
                              [("k", 0, MEM_HEADS, 0, 0, False, 0), ("v", mw, MEM_HEADS, 0, None, False, 1)],
                              [("k", MEM_HEADS), ("v", MEM_HEADS)], M, gains=mem_k_norm[layer][None])
        ma = _attention("none", mqT, mk, mvT, (), nsub * ta, M, nsub)
        h = _outproj(h, [ma], [mem_wo[layer].astype(BF16)], tm)
        h = _ffn(h, ffn2_norm[layer], ffn2_w_in[layer].astype(BF16), ffn2_w_out[layer].astype(BF16), tm)
    return h[None]
```

```python
import functools

import numpy as np
import jax
import jax.numpy as jnp
from jax import lax
from jax.experimental import pallas as pl
from jax.experimental.pallas import tpu as pltpu

HEAD_DIM = 64
ROT_DIM = HEAD_DIM // 4
ROPE_THETA = 500000.0
FOX_HEADS = 8
DSA_HEADS = 8
MOBA_HEADS = 16
IDX_HEADS = 8
IDX_DIM = 64
DSA_TOPK = 256
MOBA_BLOCK = 256
MOBA_TOPK = 3
MEM_HEADS = 4
RMS_EPS = 1e-6

NEG = -1e30
INT_MIN = -(2 ** 31)
LANE = 128
SUBLANE = 8
KPAD = 128
VROWS = HEAD_DIM + 16
ATTN_ROWS = 256
VMEM_LIMIT = 56 * 1024 * 1024
LOG2E = 1.4426950408889634
QSCALE = HEAD_DIM ** -0.5 * LOG2E

F32 = jnp.float32
BF16 = jnp.bfloat16


def _cparams(n_axes=1):
    return pltpu.CompilerParams(dimension_semantics=("arbitrary",) * n_axes, vmem_limit_bytes=VMEM_LIMIT)


def _resident(shape):
    nd = len(shape)
    return pl.BlockSpec(shape, lambda *_: (0,) * nd, pipeline_mode=pl.Buffered(1))


def _rms_bf16(x, g):
    ms = jnp.mean(x * x, axis=-1, keepdims=True)
    return (x * lax.rsqrt(ms + RMS_EPS) * g).astype(BF16)


def _ffn_kernel(x_ref, *refs, fc, pre):
    x = x_ref[...]
    if pre:
        x = x + jnp.dot(refs[0][...], refs[1][...], preferred_element_type=F32)
        refs = refs[2:]
    g_ref, wa_ref, wb_ref, wo_ref, o_ref = refs
    xn = _rms_bf16(x, g_ref[...])
    acc = jnp.zeros(x.shape, F32)
    for c in range(wa_ref.shape[1] // fc):
        a = jnp.dot(xn, wa_ref[:, c * fc:(c + 1) * fc], preferred_element_type=F32)
        b = jnp.dot(xn, wb_ref[:, c * fc:(c + 1) * fc], preferred_element_type=F32)
        gate = (a * jax.nn.sigmoid(a) * b).astype(BF16)
        acc = acc + jnp.dot(gate, wo_ref[c * fc:(c + 1) * fc, :], preferred_element_type=F32)
    o_ref[...] = x + 0.5 * acc


def _ffn(h, g, w_in, w_out, tm, pre=None):
    S, D = h.shape
    Fh = w_out.shape[0]
    in_specs, args = [pl.BlockSpec((tm, D), lambda i: (i, 0))], [h]
    if pre is not None:
        in_specs += [pl.BlockSpec((tm, pre[0].shape[1]), lambda i: (i, 0)), _resident(pre[1].shape)]
        args += list(pre)
    in_specs += [_resident((1, D)),
                 pl.BlockSpec((D, Fh), lambda i: (0, 0), pipeline_mode=pl.Buffered(1)),
                 pl.BlockSpec((D, Fh), lambda i: (0, 1), pipeline_mode=pl.Buffered(1)),
                 _resident((Fh, D))]
    return pl.pallas_call(
        functools.partial(_ffn_kernel, fc=256, pre=pre is not None),
        grid=(S // tm,),
        in_specs=in_specs,
        out_specs=pl.BlockSpec((tm, D), lambda i: (i, 0)),
        out_shape=jax.ShapeDtypeStruct((S, D), F32),
        compiler_params=_cparams(),
        name="ffn",
    )(*args, g.reshape(1, D), w_in, w_in, w_out)


def _lanes(ref, idx, tm):
    return jnp.concatenate([ref[idx]] * (tm // LANE), axis=1)


def _bf16_split3(x):
    a = x.astype(BF16).astype(F32)
    b = (x - a).astype(BF16).astype(F32)
    return a, b, x - a - b


def _heads_kernel(*refs, plan, tm, n_gain, rope, fox, kmean, n_pre):
    it = iter(refs)
    x_ref = next(it)
    pre_a = [next(it) for _ in range(n_pre)]
    pre_w = [next(it) for _ in range(n_pre)]
    g_ref, w_ref = next(it), next(it)
    gn_ref = next(it) if n_gain else None
    cos = sin = None
    if rope:
        cos, sin = next(it)[...], next(it)[...]
    bf_ref = next(it) if fox else None
    n_out = 1 + max(p[6] for p in plan)
    outs = [next(it) for _ in range(n_out)]
    wt_ref = next(it) if fox else None
    km_ref = next(it) if kmean else None
    res_ref = next(it) if n_pre else None
    carry_ref = next(it) if fox else None

    x = x_ref[...]
    if n_pre:
        for a_ref, pw_ref in zip(pre_a, pre_w):
            x = x + jnp.dot(a_ref[...], pw_ref[...], preferred_element_type=F32)
        res_ref[...] = x
    xn = _rms_bf16(x, g_ref[...])
    proj_t = lambda col, width: jnp.dot(xn, w_ref[:, col:col + width], preferred_element_type=F32).T
    zeros_pad = jnp.zeros((KPAD - HEAD_DIM, tm), F32)
    half = ROT_DIM // 2

    fq_rows = fk_rows = None
    if fox:
        @pl.when(pl.program_id(0) == 0)
        def _():
            carry_ref[...] = jnp.zeros(carry_ref.shape, F32)

        gT = proj_t(w_ref.shape[1] - LANE, LANE)
        xg = gT[0:FOX_HEADS] + _lanes(bf_ref, slice(None), tm)
        lf = jnp.minimum(xg, 0.0) - jnp.log1p(jnp.exp(-jnp.abs(xg)))
        tri = jnp.where(lax.broadcasted_iota(jnp.int32, (tm, tm), 0)
                        <= lax.broadcasted_iota(jnp.int32, (tm, tm), 1), 1.0, 0.0).astype(BF16)
        cs = sum(jnp.dot(part.astype(BF16), tri, preferred_element_type=F32) for part in _bf16_split3(lf))
        fcum = (carry_ref[...] + cs) * LOG2E
        carry_ref[...] = carry_ref[...] + jnp.sum(lf, axis=1, keepdims=True)
        wt_ref[...] = gT[FOX_HEADS:FOX_HEADS + IDX_HEADS] * (IDX_HEADS ** -0.5 * IDX_DIM ** -0.5)
        rid = lax.broadcasted_iota(jnp.int32, (SUBLANE, tm), 0)
        fq_rows, fk_rows = [], []
        for h in range(FOX_HEADS):
            f1, f2, f3 = _bf16_split3(fcum[h:h + 1])
            sel = lambda a, b, c, one: jnp.where(rid == 0, a, jnp.where(rid == 1, b, jnp.where(rid == 2, c, one)))
            ones_lo = jnp.where(rid < 6, 1.0, 0.0)
            fq_rows.append(sel(f1, f2, f3, ones_lo))
            fk_rows.append(jnp.where(rid < 3, 1.0, jnp.where(rid == 3, -f1, jnp.where(rid == 4, -f2,
                           jnp.where(rid == 5, -f3, 0.0)))))

    for kind, col, nheads, head0, gain, use_rope, out in plan:
        width = -(-nheads * HEAD_DIM // LANE) * LANE
        segT = proj_t(col, width)
        o_ref = outs[out]
        for hh in range(nheads):
            r = segT[hh * HEAD_DIM:(hh + 1) * HEAD_DIM]
            if gain is not None:
                r = r * lax.rsqrt(jnp.mean(r * r, axis=0, keepdims=True) + RMS_EPS) * _lanes(gn_ref, gain, tm)
            if use_rope:
                x1, x2 = r[:half], r[half:ROT_DIM]
                r = jnp.concatenate([x1 * cos - x2 * sin, x2 * cos + x1 * sin, r[ROT_DIM:]], axis=0)
            h = head0 + hh
            if kind == "v":
                o_ref[h] = jnp.concatenate([r, jnp.ones((VROWS - HEAD_DIM, tm), F32)], axis=0).astype(BF16)
                continue
            if kind == "q":
                r = r * QSCALE
            extra = None
            if fox and out == 0:
                extra = fq_rows[h]
            elif fox and out == 1:
                extra = fk_rows[h]
            if extra is None:
                padded = jnp.concatenate([r, zeros_pad], axis=0)
            else:
                padded = jnp.concatenate([r, extra, zeros_pad[SUBLANE:]], axis=0)
            if kind == "q":
                o_ref[h] = padded.astype(BF16)
            else:
                o_ref[h] = padded.T.astype(BF16)
                if kmean:
                    lane = lax.broadcasted_iota(jnp.int32, (HEAD_DIM, LANE), 1)
                    tile = jnp.zeros((HEAD_DIM, LANE), F32)
                    for b in range(tm // MOBA_BLOCK):
                        mb = jnp.sum(r[:, b * MOBA_BLOCK:(b + 1) * MOBA_BLOCK], axis=1, keepdims=True)
                        tile = jnp.where(lane == b, mb * (1.0 / MOBA_BLOCK), tile)
                    km_ref[0, h] = tile


def _heads_proj(x, g, w, plan, out_heads, tm, gains=None, rope=None, b_forget=None, kmean=False, pre=None):
    S, D = x.shape
    N = w.shape[1]
    fox = b_forget is not None
    pre_a, pre_w = pre if pre is not None else ([], [])
    in_specs = [pl.BlockSpec((tm, D), lambda i: (i, 0))]
    in_specs += [pl.BlockSpec((tm, a.shape[1]), lambda i: (i, 0)) for a in pre_a]
    in_specs += [_resident(pw.shape) for pw in pre_w]
    in_specs += [_resident((1, D)), _resident((D, N))]
    args = [x, *pre_a, *pre_w, g.reshape(1, D), w]
    if gains is not None:
        gt = jnp.broadcast_to(gains[:, :, None], gains.shape + (LANE,)).astype(F32)
        in_specs.append(_resident(gt.shape))
        args.append(gt)
    if rope is not None:
        in_specs += [pl.BlockSpec((ROT_DIM // 2, tm), lambda i: (0, i))] * 2
        args += list(rope)
    if fox:
        in_specs.append(_resident((FOX_HEADS, LANE)))
        args.append(jnp.broadcast_to(b_forget[:, None], (FOX_HEADS, LANE)).astype(F32))
    out_specs, out_shapes = [], []
    for kind, nh in out_heads:
        if kind == "q":
            out_specs.append(pl.BlockSpec((nh, KPAD, tm), lambda i: (0, 0, i)))
            out_shapes.append(jax.ShapeDtypeStruct((nh, KPAD, S), BF16))
        elif kind == "k":
            out_specs.append(pl.BlockSpec((nh, tm, KPAD), lambda i: (0, i, 0)))
            out_shapes.append(jax.ShapeDtypeStruct((nh, S, KPAD), BF16))
        else:
            out_specs.append(pl.BlockSpec((nh, VROWS, tm), lambda i: (0, 0, i)))
            out_shapes.append(jax.ShapeDtypeStruct((nh, VROWS, S), BF16))
    if fox:
        out_specs.append(pl.BlockSpec((IDX_HEADS, tm), lambda i: (0, i)))
        out_shapes.append(jax.ShapeDtypeStruct((IDX_HEADS, S), F32))
    if kmean:
        nhk = out_heads[1][1]
        out_specs.append(pl.BlockSpec((1, nhk, HEAD_DIM, LANE), lambda i: (i, 0, 0, 0)))
        out_shapes.append(jax.ShapeDtypeStruct((S // tm, nhk, HEAD_DIM, LANE), F32))
    if pre_a:
        out_specs.append(pl.BlockSpec((tm, D), lambda i: (i, 0)))
        out_shapes.append(jax.ShapeDtypeStruct((S, D), F32))
    return pl.pallas_call(
        functools.partial(_heads_kernel, plan=tuple(plan), tm=tm, n_gain=0 if gains is None else gains.shape[0],
                          rope=rope is not None, fox=fox, kmean=kmean, n_pre=len(pre_a)),
        grid=(S // tm,),
        in_specs=in_specs,
        out_specs=out_specs,
        out_shape=out_shapes,
        scratch_shapes=[pltpu.VMEM((FOX_HEADS, 1), F32)] if fox else [],
        compiler_params=_cparams(),
        name="heads_proj",
    )(*args)


def _heads_update(units, qT_ref, k_ref, vT_ref, s_ref, p_ref, m_ref, acc_ref, biased):
    _, tk, tqu = s_ref.shape
    rc = min(ATTN_ROWS, tk)
    n = len(units)
    m_new = alpha = None
    for t in range(n + 1):
        mx = None
        if t < n:
            h, q0, dg = units[t]
            qT = qT_ref[h, :, q0:q0 + tqu]
        if t >= 1:
            hb, qb, _ = units[t - 1]
            m_b, alpha_b = m_new, alpha
        for r0 in range(0, tk, rc):
            if t < n:
                s = biased(jnp.dot(k_ref[h, r0:r0 + rc, :], qT, preferred_element_type=F32), h, q0, r0, dg)
                s_ref[t % 2, r0:r0 + rc, :] = s
                cm = jnp.max(s.reshape(rc // SUBLANE, SUBLANE, tqu), axis=0)
                mx = cm if mx is None else jnp.maximum(mx, cm)
            if t >= 1:
                p_ref[(t - 1) % 2, r0:r0 + rc, :] = jnp.exp2(s_ref[(t - 1) % 2, r0:r0 + rc, :] - m_b).astype(BF16)
        if t >= 1:
            pv = jnp.dot(vT_ref[hb], p_ref[(t - 1) % 2], preferred_element_type=F32)
            acc_ref[hb, :, qb:qb + tqu] = alpha_b * acc_ref[hb, :, qb:qb + tqu] + pv
        if t < n:
            m_prev = m_ref[h, :, q0:q0 + tqu]
            m_new = jnp.maximum(m_prev, jnp.max(mx, axis=0, keepdims=True))
            alpha = jnp.exp2(m_prev - m_new)
            m_ref[h, :, q0:q0 + tqu] = m_new


def _attn_init(m_ref, acc_ref):
    m_ref[...] = jnp.full(m_ref.shape, NEG, F32)
    acc_ref[...] = jnp.zeros(acc_ref.shape, F32)


def _attn_finish(o_ref, acc_ref, nh):
    hd = HEAD_DIM
    out = lambda h: acc_ref[h, :hd, :] / acc_ref[h, hd:hd + 1, :]
    for h in range(0, nh, 2):
        o_ref[:, h * hd:(h + 2) * hd] = jnp.concatenate([out(h), out(h + 1)], axis=0).T.astype(o_ref.dtype)


def _attn_kernel(qi_ref, kj_ref, *refs, mode, nh, tq, tk, nsub):
    if mode == "moba":
        qT_ref, k_ref, vT_ref, sb_ref, o_ref, m_ref, acc_ref, s_ref, p_ref, cb_ref = refs
    elif mode == "causal":
        qT_ref, k_ref, vT_ref, o_ref, m_ref, acc_ref, s_ref, p_ref, cb_ref = refs
    else:
        qT_ref, k_ref, vT_ref, o_ref, m_ref, acc_ref, s_ref, p_ref = refs
    p = pl.program_id(0)
    i = qi_ref[p]
    j = kj_ref[p]
    causal = mode != "none"
    tqu = tq // nsub

    if causal:
        @pl.when(p == 0)
        def _():
            row = lax.broadcasted_iota(jnp.int32, (tk, tqu), 0)
            col = lax.broadcasted_iota(jnp.int32, (tk, tqu), 1)
            cb_ref[...] = jnp.where(row <= col, 0.0, NEG)

    @pl.when(j == 0)
    def _():
        _attn_init(m_ref, acc_ref)

    def biased(s, h, q0, r0, diag):
        if mode == "moba":
            s = s + sb_ref[h, pl.ds(j * (tk // MOBA_BLOCK) + r0 // MOBA_BLOCK, 1), q0:q0 + tqu]
        if diag:
            s = s + cb_ref[r0:r0 + s.shape[0], :]
        return s

    def process(first):
        units = [(h, q * tqu, q == first) for h in range(nh) for q in range(max(first, 0), nsub)]
        _heads_update(units, qT_ref, k_ref, vT_ref, s_ref, p_ref, m_ref, acc_ref, biased)

    if causal:
        d = j - nsub * i
        pl.when(d < 0)(lambda: process(-1))
        for q in range(nsub):
            pl.when(d == q)(functools.partial(process, q))
        last = d == nsub - 1
    else:
        process(-1)
        last = j == 0

    @pl.when(last)
    def _():
        _attn_finish(o_ref, acc_ref, nh)


def _attention(mode, qT, k, vT, extra, tq, tk, nsub):
    nh, _, S = qT.shape
    hd = HEAD_DIM
    nq = S // tq
    if mode == "none":
        pairs = [(i, 0) for i in range(nq)]
    else:
        assert tq == nsub * tk
        pairs = [(i, j) for i in range(nq) for j in range(nsub * (i + 1))]
    qi = jnp.asarray(np.array([a for a, _ in pairs], np.int32))
    kj = jnp.asarray(np.array([b for _, b in pairs], np.int32))
    in_specs = [pl.BlockSpec((nh, KPAD, tq), lambda p, qi, kj: (0, 0, qi[p])),
                pl.BlockSpec((nh, tk, KPAD), lambda p, qi, kj: (0, kj[p], 0)),
                pl.BlockSpec((nh, VROWS, tk), lambda p, qi, kj: (0, 0, kj[p]))]
    if mode == "moba":
        nb = extra[0].shape[1]
        in_specs += [pl.BlockSpec((nh, nb, tq), lambda p, qi, kj: (0, 0, qi[p]))]
    tqu = tq // nsub
    scratch = [pltpu.VMEM((nh, 1, tq), F32), pltpu.VMEM((nh, VROWS, tq), F32),
               pltpu.VMEM((2, tk, tqu), F32), pltpu.VMEM((2, tk, tqu), BF16)]
    if mode != "none":
        scratch.append(pltpu.VMEM((tk, tqu), F32))
    return pl.pallas_call(
        functools.partial(_attn_kernel, mode=mode, nh=nh, tq=tq, tk=tk, nsub=nsub),
        grid_spec=pltpu.PrefetchScalarGridSpec(
            num_scalar_prefetch=2,
            grid=(len(pairs),),
            in_specs=in_specs,
            out_specs=pl.BlockSpec((tq, nh * hd), lambda p, qi, kj: (qi[p], 0)),
            scratch_shapes=scratch),
        out_shape=jax.ShapeDtypeStruct((S, nh * hd), BF16),
        compiler_params=_cparams(),
        name="attn_" + mode,
    )(qi, kj, qT, k, vT, *extra)


def _gate_kernel(qT_ref, km_ref, o_ref, *, nh, nb, tq):
    i = pl.program_id(0)
    n = lax.broadcasted_iota(jnp.int32, (nb, tq), 0)
    own = (i * tq + lax.broadcasted_iota(jnp.int32, (nb, tq), 1)) // MOBA_BLOCK
    ninf = jnp.float32(-jnp.inf)

    def body(h, carry):
        g = jnp.dot(km_ref[h], qT_ref[h], preferred_element_type=F32)
        g = jnp.where(n < own, g, ninf)
        keep = n == own
        for _ in range(MOBA_TOPK):
            mx = jnp.max(g, axis=0, keepdims=True)
            first = jnp.min(jnp.where((g == mx) & (mx > ninf), n, nb), axis=0, keepdims=True)
            pick = n == first
            keep = keep | pick
            g = jnp.where(pick, ninf, g)
        o_ref[h] = jnp.where(keep, 0.0, NEG)
        return carry

    lax.fori_loop(0, nh, body, 0, unroll=4)


def _moba_gate(qT, kmean, tq):
    nh, _, S = qT.shape
    nb = kmean.shape[1]
    return pl.pallas_call(
        functools.partial(_gate_kernel, nh=nh, nb=nb, tq=tq),
        grid=(S // tq,),
        in_specs=[pl.BlockSpec((nh, KPAD, tq), lambda i: (0, 0, i)), _resident((nh, nb, KPAD))],
        out_specs=pl.BlockSpec((nh, nb, tq), lambda i: (0, 0, i)),
        out_shape=jax.ShapeDtypeStruct((nh, nb, S), F32),
        compiler_params=_cparams(),
        name="moba_gate",
    )(qT, kmean)


def _dsa_kernel(qi_ref, kj_ref, ph_ref, ikb_ref, kvb_ref, nk_ref,
                iqT_ref, ik_ref, w_ref, qT_ref, k_ref, vT_ref, o_ref,
                hi_ref, lo_ref, thi_ref, tlo_ref, cut_ref, m_ref, acc_ref, s_ref, p_ref, bias_ref,
                *, nih, nh, tq, tk, topk, seq, chunk):
    p = pl.program_id(0)
    i = qi_ref[p]
    j = kj_ref[p]
    phase = ph_ref[p]
    nkb = nk_ref[p]
    I16, I32 = jnp.int16, jnp.int32
    lo_min = -(2 ** 15)
    one, zero = jnp.ones((), I16), jnp.zeros((), I16)
    row16 = lambda x: x.astype(I16)

    @pl.when(phase == 0)
    def _scores():
        ik = ik_ref[...]
        sc = jnp.zeros((tk, tq), F32)
        for h in range(nih):
            d = jnp.dot(ik, iqT_ref[h], preferred_element_type=F32)
            sc = sc + w_ref[h:h + 1, :] * jnp.maximum(d, 0.0)
        b = lax.bitcast_convert_type(sc, I32)
        key = b ^ ((b >> 31) & 0x7FFFFFFF)
        row = j * tk + lax.broadcasted_iota(I32, (tk, tq), 0)
        col = i * tq + lax.broadcasted_iota(I32, (tk, tq), 1)
        key = jnp.where(row <= col, key, INT_MIN)
        rows = pl.ds(pl.multiple_of(j * tk, tk), tk)
        hi_ref[rows, :] = (key >> 16).astype(I16)
        lo_ref[rows, :] = ((key & 0xFFFF) + lo_min).astype(I16)

    def count(hit):
        def body(c, acc):
            r0 = pl.multiple_of(c * chunk, chunk)
            h = hit(hi_ref[pl.ds(r0, chunk), :], lo_ref[pl.ds(r0, chunk), :], r0)
            part = h[0:2 * SUBLANE]
            for r in range(2 * SUBLANE, chunk, 2 * SUBLANE):
                part = part + h[r:r + 2 * SUBLANE]
            return acc + part.astype(I32)
        acc = lax.fori_loop(0, nkb * (tk // chunk), body, jnp.zeros((2 * SUBLANE, tq), I32))
        return jnp.sum(acc, axis=0, keepdims=True)

    def kth_largest(pick, k):
        c0 = count(lambda hi, lo, r0: jnp.where(pick(hi, lo) >= zero, one, zero))
        t = jnp.where(c0 >= k, 0, lo_min)

        def bit_body(it, t):
            cand = t | jnp.left_shift(jnp.int32(1), 14 - it)
            c16 = row16(cand)
            cnt = count(lambda hi, lo, r0: jnp.where(pick(hi, lo) >= c16, one, zero))
            return jnp.where(cnt >= k, cand, t)

        return lax.fori_loop(0, 15, bit_body, t)

    @pl.when((phase == 1) & (j == 0))
    def _select():
        thi = kth_largest(lambda hi, lo: hi, topk)
        thi16 = row16(thi)
        above = count(lambda hi, lo, r0: jnp.where(hi > thi16, one, zero))
        need = topk - above

        def mark(c, carry):
            rows = pl.ds(pl.multiple_of(c * chunk, chunk), chunk)
            lo_ref[rows, :] = jnp.where(hi_ref[rows, :] == thi16, lo_ref[rows, :], jnp.full((), lo_min, I16))
            return carry

        lax.fori_loop(0, nkb * (tk // chunk), mark, 0)
        tlo = kth_largest(lambda hi, lo: lo, need)
        tlo = jnp.where((thi == lo_min) & (tlo == lo_min), lo_min + 1, tlo)
        tlo16 = row16(tlo)
        thi_ref[...] = thi
        tlo_ref[...] = tlo
        cut_ref[...] = jnp.full((1, tq), seq, I32)
        in_bucket = lambda hi, x: jnp.where(hi == thi16, x, zero)
        cge = above + count(lambda hi, lo, r0: in_bucket(hi, jnp.where(lo >= tlo16, one, zero)))

        @pl.when(jnp.max(cge) > topk)
        def _ties():
            need_tie = need - count(lambda hi, lo, r0: jnp.where(lo > tlo16, one, zero))
            nbits = max(1, int(seq - 1).bit_length())

            def pos_body(it, x):
                cand = x + jnp.left_shift(jnp.int32(1), nbits - 1 - it)
                before = lambda r0: jnp.where(r0 + lax.broadcasted_iota(I32, (chunk, tq), 0) < cand, 1, 0).astype(I16)
                c = count(lambda hi, lo, r0: in_bucket(hi, jnp.where(lo == tlo16, before(r0), zero)))
                return jnp.where(c < need_tie, cand, x)

            x = lax.fori_loop(0, nbits, pos_body, jnp.zeros((1, tq), I32))
            cut_ref[...] = jnp.where(cge > topk, x, seq)

        _attn_init(m_ref, acc_ref)

    @pl.when(phase == 1)
    def _attend():
        rows = pl.ds(pl.multiple_of(j * tk, tk), tk)
        hi, lo = hi_ref[rows, :], lo_ref[rows, :]
        thi16, tlo16 = row16(thi_ref[...]), row16(tlo_ref[...])
        kept = jnp.where(j * tk + lax.broadcasted_iota(I32, (tk, tq), 0) <= cut_ref[...], 1, 0).astype(I16)
        tie = jnp.where(lo == tlo16, kept, zero)
        sel = jnp.where(hi > thi16, one, jnp.where(hi == thi16, jnp.where(lo > tlo16, one, tie), zero))
        bias_ref[...] = jnp.where(sel.astype(I32) > 0, 0.0, NEG)

        def biased(s, h, q0, r0, diag):
            return s + bias_ref[r0:r0 + s.shape[0], :]

        units = [(h, 0, False) for h in range(nh)]
        _heads_update(units, qT_ref, k_ref, vT_ref, s_ref, p_ref, m_ref, acc_ref, biased)

        @pl.when(j == nkb - 1)
        def _():
            _attn_finish(o_ref, acc_ref, nh)


def _dsa_attention(iqT, ik, wT, qT, k, vT, tq, tk, topk):
    nh, _, S = qT.shape
    hd = HEAD_DIM
    nih = iqT.shape[0]
    steps = []
    for i in range(S // tq):
        nkb = (i * tq + tq - 1) // tk + 1
        steps += [(i, j, 0, j, 0, nkb) for j in range(nkb)]
        steps += [(i, j, 1, nkb - 1, j, nkb) for j in range(nkb)]
    tab = [jnp.asarray(np.array([s[c] for s in steps], np.int32)) for c in range(6)]
    chunk = min(tk, 512)
    qmap = lambda p, qi, kj, ph, ikb, kvb, nk: (0, 0, qi[p])
    return pl.pallas_call(
        functools.partial(_dsa_kernel, nih=nih, nh=nh, tq=tq, tk=tk, topk=topk, seq=S, chunk=chunk),
        grid_spec=pltpu.PrefetchScalarGridSpec(
            num_scalar_prefetch=6,
            grid=(len(steps),),
            in_specs=[pl.BlockSpec((nih, KPAD, tq), qmap),
                      pl.BlockSpec((tk, KPAD), lambda p, qi, kj, ph, ikb, kvb, nk: (ikb[p], 0)),
                      pl.BlockSpec((nih, tq), lambda p, qi, kj, ph, ikb, kvb, nk: (0, qi[p])),
                      pl.BlockSpec((nh, KPAD, tq), qmap),
                      pl.BlockSpec((nh, tk, KPAD), lambda p, qi, kj, ph, ikb, kvb, nk: (0, kvb[p], 0)),
                      pl.BlockSpec((nh, VROWS, tk), lambda p, qi, kj, ph, ikb, kvb, nk: (0, 0, kvb[p]))],
            out_specs=pl.BlockSpec((tq, nh * hd), lambda p, qi, kj, ph, ikb, kvb, nk: (qi[p], 0)),
            scratch_shapes=[pltpu.VMEM((S, tq), jnp.int16), pltpu.VMEM((S, tq), jnp.int16),
                            pltpu.VMEM((1, tq), jnp.int32), pltpu.VMEM((1, tq), jnp.int32),
                            pltpu.VMEM((1, tq), jnp.int32),
                            pltpu.VMEM((nh, 1, tq), F32), pltpu.VMEM((nh, VROWS, tq), F32),
                            pltpu.VMEM((2, tk, tq), F32), pltpu.VMEM((2, tk, tq), BF16),
                            pltpu.VMEM((tk, tq), F32)]),
        out_shape=jax.ShapeDtypeStruct((S, nh * hd), BF16),
        compiler_params=_cparams(),
        name="dsa",
    )(*tab, iqT, ik, wT, qT, k, vT)


def kernel(x, mem, ffn1_norm, ffn1_w_in, ffn1_w_out, mix_norm, even_w_in, even_b_forget, fox_q_norm, fox_k_norm, dsa_q_norm, dsa_k_norm, odd_w_in, moba_q_norm, moba_k_norm, mix_w_out, mem_norm_x, mem_norm_m, mem_wq, mem_wkv, mem_q_norm, mem_k_norm, mem_wo, ffn2_norm, ffn2_w_in, ffn2_w_out):
    _, S, D = x.shape
    M = mem.shape[1]
    depth = ffn1_norm.shape[0]
    tm = min(512, S)
    tf = min(1024, S)
    ta = min(512, S)
    nsub = 2 if S % (2 * ta) == 0 else 1
    tdq = min(512, S)
    pos = jnp.arange(S, dtype=F32)
    inv = ROPE_THETA ** (-jnp.arange(0, ROT_DIM, 2, dtype=F32) / ROT_DIM)
    ang = inv[:, None] * pos[None, :]
    rope = (jnp.cos(ang), jnp.sin(ang))
    W8 = 8 * HEAD_DIM
    zcols = lambda n: jnp.zeros((D, n), F32)

    h = x[0]
    m0 = mem[0]
    for layer in range(depth):
        h = _ffn(h, ffn1_norm[layer], ffn1_w_in[layer].astype(BF16), ffn1_w_out[layer].astype(BF16), tf)
        if layer % 2 == 0:
            e = layer // 2
            o = np.cumsum([0, W8, W8, W8, FOX_HEADS, W8, W8, W8, IDX_HEADS * IDX_DIM, IDX_DIM, IDX_HEADS])
            w = even_w_in[e]
            seg = lambda n: w[:, o[n]:o[n + 1]]
            w = jnp.concatenate([seg(0), seg(1), seg(2), seg(4), seg(5), seg(6), seg(7),
                                 seg(8), zcols(KPAD - IDX_DIM),
                                 seg(3), seg(9), zcols(LANE - FOX_HEADS - IDX_HEADS)], axis=1).astype(BF16)
            plan = [("q", 0, 8, 0, 0, False, 0), ("k", W8, 8, 0, 1, False, 1), ("v", 2 * W8, 8, 0, None, False, 2),
                    ("q", 3 * W8, 8, 0, 2, True, 3), ("k", 4 * W8, 8, 0, 3, True, 4), ("v", 5 * W8, 8, 0, None, False, 5),
                    ("q", 6 * W8, 8, 0, None, True, 6), ("k", 7 * W8, 1, 0, None, True, 7)]
            gains = jnp.stack([fox_q_norm[e], fox_k_norm[e], dsa_q_norm[e], dsa_k_norm[e]])
            fqT, fk, fvT, dqT, dk, dvT, iqT, ik, wT = _heads_proj(
                h, mix_norm[layer], w, plan,
                [("q", 8), ("k", 8), ("v", 8), ("q", 8), ("k", 8), ("v", 8), ("q", 8), ("k", 1)],
                tm, gains=gains, rope=rope, b_forget=even_b_forget[e])
            fox = _attention("causal", fqT, fk, fvT, (), nsub * ta, ta, nsub)
            dsa = _dsa_attention(iqT, ik[0], wT, dqT, dk, dvT, tdq, ta, min(DSA_TOPK, S // 4))
            wo = mix_w_out[layer].astype(BF16)
            mixed = ([fox, dsa], [wo[:W8], wo[W8:]])
        else:
            od = layer // 2
            w = odd_w_in[od].astype(BF16)
            plan = [("q", 0, 8, 0, 0, True, 0), ("q", W8, 8, 8, 0, True, 0),
                    ("k", 2 * W8, 8, 0, 1, True, 1), ("k", 3 * W8, 8, 8, 1, True, 1),
                    ("v", 4 * W8, 8, 0, None, False, 2), ("v", 5 * W8, 8, 8, None, False, 2)]
            gains = jnp.stack([moba_q_norm[od], moba_k_norm[od]])
            qT, k, vT, km = _heads_proj(h, mix_norm[layer], w, plan, [("q", 16), ("k", 16), ("v", 16)],
                                        tm, gains=gains, rope=rope, kmean=True)
            nbt = tm // MOBA_BLOCK
            km = jnp.transpose(km[..., :nbt], (1, 0, 3, 2)).reshape(MOBA_HEADS, S // MOBA_BLOCK, HEAD_DIM)
            km = jnp.pad(km, ((0, 0), (0, 0), (0, KPAD - HEAD_DIM))).astype(BF16)
            selb = _moba_gate(qT, km, ta)
            moba = _attention("moba", qT, k, vT, (selb,), nsub * ta, ta, nsub)
            mixed = ([moba], [mix_w_out[layer].astype(BF16)])
        mw = MEM_HEADS * HEAD_DIM
        mqT, h = _heads_proj(h, mem_norm_x[layer], mem_wq[layer].astype(BF16),
                             [("q", 0, MEM_HEADS, 0, 0, False, 0)], [("q", MEM_HEADS)], tm,
                             gains=mem_q_norm[layer][None], pre=mixed)
        mk, mvT = _heads_proj(m0, mem_norm_m[layer], mem_wkv[layer].astype(BF16),
                              [("k", 0, MEM_HEADS, 0, 0, False, 0), ("v", mw, MEM_HEADS, 0, None, False, 1)],
                              [("k", MEM_HEADS), ("v", MEM_HEADS)], M, gains=mem_k_norm[layer][None])
        ma = _attention("none", mqT, mk, mvT, (), nsub * ta, M, nsub)
        h = _ffn(h, ffn2_norm[layer], ffn2_w_in[layer].astype(BF16), ffn2_w_out[layer].astype(BF16), tf,
                 pre=(ma, mem_wo[layer].astype(BF16)))
    return h[None]
```

```python
import functools

import numpy as np
import jax
import jax.numpy as jnp
from jax import lax
from jax.experimental import pallas as pl
from jax.experimental.pallas import tpu as pltpu

HEAD_DIM = 64
ROT_DIM = HEAD_DIM // 4
ROPE_THETA = 500000.0
FOX_HEADS = 8
DSA_HEADS = 8
MOBA_HEADS = 16
IDX_HEADS = 8
IDX_DIM = 64
DSA_TOPK = 256
MOBA_BLOCK = 256
MOBA_TOPK = 3
MEM_HEADS = 4
RMS_EPS = 1e-6

NEG = -1e30
INT_MIN = -(2 ** 31)
LANE = 128
SUBLANE = 8
KPAD = 128
VROWS = HEAD_DIM + 16
ATTN_ROWS = 256
VMEM_LIMIT = 56 * 1024 * 1024
LOG2E = 1.4426950408889634
QSCALE = HEAD_DIM ** -0.5 * LOG2E

F32 = jnp.float32
BF16 = jnp.bfloat16


def _cparams(n_axes=1):
    return pltpu.CompilerParams(dimension_semantics=("arbitrary",) * n_axes, vmem_limit_bytes=VMEM_LIMIT)


def _resident(shape):
    nd = len(shape)
    return pl.BlockSpec(shape, lambda *_: (0,) * nd, pipeline_mode=pl.Buffered(1))


def _rms_bf16(x, g):
    ms = jnp.mean(x * x, axis=-1, keepdims=True)
    return (x * lax.rsqrt(ms + RMS_EPS) * g).astype(BF16)


def _ffn_kernel(x_ref, *refs, fc, pre):
    x = x_ref[...]
    if pre:
        x = x + jnp.dot(refs[0][...], refs[1][...], preferred_element_type=F32)
        refs = refs[2:]
    g_ref, wa_ref, wb_ref, wo_ref, o_ref = refs
    xn = _rms_bf16(x, g_ref[...])
    acc = jnp.zeros(x.shape, F32)
    for c in range(wa_ref.shape[1] // fc):
        a = jnp.dot(xn, wa_ref[:, c * fc:(c + 1) * fc], preferred_element_type=F32)
        b = jnp.dot(xn, wb_ref[:, c * fc:(c + 1) * fc], preferred_element_type=F32)
        gate = (a * jax.nn.sigmoid(a) * b).astype(BF16)
        acc = acc + jnp.dot(gate, wo_ref[c * fc:(c + 1) * fc, :], preferred_element_type=F32)
    o_ref[...] = x + 0.5 * acc


def _ffn(h, g, w_in, w_out, tm, pre=None):
    S, D = h.shape
    Fh = w_out.shape[0]
    in_specs, args = [pl.BlockSpec((tm, D), lambda i: (i, 0))], [h]
    if pre is not None:
        in_specs += [pl.BlockSpec((tm, pre[0].shape[1]), lambda i: (i, 0)), _resident(pre[1].shape)]
        args += list(pre)
    in_specs += [_resident((1, D)),
                 pl.BlockSpec((D, Fh), lambda i: (0, 0), pipeline_mode=pl.Buffered(1)),
                 pl.BlockSpec((D, Fh), lambda i: (0, 1), pipeline_mode=pl.Buffered(1)),
                 _resident((Fh, D))]
    return pl.pallas_call(
        functools.partial(_ffn_kernel, fc=256, pre=pre is not None),
        grid=(S // tm,),
        in_specs=in_specs,
        out_specs=pl.BlockSpec((tm, D), lambda i: (i, 0)),
        out_shape=jax.ShapeDtypeStruct((S, D), F32),
        compiler_params=_cparams(),
        name="ffn",
    )(*args, g.reshape(1, D), w_in, w_in, w_out)


def _lanes(ref, idx, tm):
    return jnp.concatenate([ref[idx]] * (tm // LANE), axis=1)


def _bf16_split3(x):
    a = x.astype(BF16).astype(F32)
    b = (x - a).astype(BF16).astype(F32)
    return a, b, x - a - b


def _heads_kernel(*refs, plan, tm, n_gain, rope, fox, kmean, n_pre):
    it = iter(refs)
    x_ref = next(it)
    pre_a = [next(it) for _ in range(n_pre)]
    pre_w = [next(it) for _ in range(n_pre)]
    g_ref, w_ref = next(it), next(it)
    gn_ref = next(it) if n_gain else None
    cos = sin = None
    if rope:
        cos, sin = next(it)[...], next(it)[...]
    bf_ref = next(it) if fox else None
    n_out = 1 + max(p[6] for p in plan)
    outs = [next(it) for _ in range(n_out)]
    wt_ref = next(it) if fox else None
    km_ref = next(it) if kmean else None
    res_ref = next(it) if n_pre else None
    carry_ref = next(it) if fox else None

    x = x_ref[...]
    if n_pre:
        for a_ref, pw_ref in zip(pre_a, pre_w):
            x = x + jnp.dot(a_ref[...], pw_ref[...], preferred_element_type=F32)
        res_ref[...] = x
    xn = _rms_bf16(x, g_ref[...])
    proj_t = lambda col, width: jnp.dot(xn, w_ref[:, col:col + width], preferred_element_type=F32).T
    zeros_pad = jnp.zeros((KPAD - HEAD_DIM, tm), F32)
    half = ROT_DIM // 2

    fq_rows = fk_rows = None
    if fox:
        @pl.when(pl.program_id(0) == 0)
        def _():
            carry_ref[...] = jnp.zeros(carry_ref.shape, F32)

        gT = proj_t(w_ref.shape[1] - LANE, LANE)
        xg = gT[0:FOX_HEADS] + _lanes(bf_ref, slice(None), tm)
        lf = jnp.minimum(xg, 0.0) - jnp.log1p(jnp.exp(-jnp.abs(xg)))
        tri = jnp.where(lax.broadcasted_iota(jnp.int32, (tm, tm), 0)
                        <= lax.broadcasted_iota(jnp.int32, (tm, tm), 1), 1.0, 0.0).astype(BF16)
        cs = sum(jnp.dot(part.astype(BF16), tri, preferred_element_type=F32) for part in _bf16_split3(lf))
        fcum = (carry_ref[...] + cs) * LOG2E
        carry_ref[...] = carry_ref[...] + jnp.sum(lf, axis=1, keepdims=True)
        wt_ref[...] = gT[FOX_HEADS:FOX_HEADS + IDX_HEADS] * (IDX_HEADS ** -0.5 * IDX_DIM ** -0.5)
        rid = lax.broadcasted_iota(jnp.int32, (SUBLANE, tm), 0)
        fq_rows, fk_rows = [], []
        for h in range(FOX_HEADS):
            f1, f2, f3 = _bf16_split3(fcum[h:h + 1])
            sel = lambda a, b, c, one: jnp.where(rid == 0, a, jnp.where(rid == 1, b, jnp.where(rid == 2, c, one)))
            ones_lo = jnp.where(rid < 6, 1.0, 0.0)
            fq_rows.append(sel(f1, f2, f3, ones_lo))
            fk_rows.append(jnp.where(rid < 3, 1.0, jnp.where(rid == 3, -f1, jnp.where(rid == 4, -f2,
                           jnp.where(rid == 5, -f3, 0.0)))))

    for kind, col, nheads, head0, gain, use_rope, out in plan:
        width = -(-nheads * HEAD_DIM // LANE) * LANE
        segT = proj_t(col, width)
        o_ref = outs[out]
        for hh in range(nheads):
            r = segT[hh * HEAD_DIM:(hh + 1) * HEAD_DIM]
            if gain is not None:
                r = r * lax.rsqrt(jnp.mean(r * r, axis=0, keepdims=True) + RMS_EPS) * _lanes(gn_ref, gain, tm)
            if use_rope:
                x1, x2 = r[:half], r[half:ROT_DIM]
                r = jnp.concatenate([x1 * cos - x2 * sin, x2 * cos + x1 * sin, r[ROT_DIM:]], axis=0)
            h = head0 + hh
            if kind == "v":
                o_ref[h] = jnp.concatenate([r, jnp.ones((VROWS - HEAD_DIM, tm), F32)], axis=0).astype(BF16)
                continue
            if kind == "q":
                r = r * QSCALE
            extra = None
            if fox and out == 0:
                extra = fq_rows[h]
            elif fox and out == 1:
                extra = fk_rows[h]
            if extra is None:
                padded = jnp.concatenate([r, zeros_pad], axis=0)
            else:
                padded = jnp.concatenate([r, extra, zeros_pad[SUBLANE:]], axis=0)
            if kind == "q":
                o_ref[h] = padded.astype(BF16)
            else:
                o_ref[h] = padded.T.astype(BF16)
                if kmean:
                    lane = lax.broadcasted_iota(jnp.int32, (HEAD_DIM, LANE), 1)
                    tile = jnp.zeros((HEAD_DIM, LANE), F32)
                    for b in range(tm // MOBA_BLOCK):
                        mb = jnp.sum(r[:, b * MOBA_BLOCK:(b + 1) * MOBA_BLOCK], axis=1, keepdims=True)
                        tile = jnp.where(lane == b, mb * (1.0 / MOBA_BLOCK), tile)
                    km_ref[0, h] = tile


def _heads_proj(x, g, w, plan, out_heads, tm, gains=None, rope=None, b_forget=None, kmean=False, pre=None):
    S, D = x.shape
    N = w.shape[1]
    fox = b_forget is not None
    pre_a, pre_w = pre if pre is not None else ([], [])
    in_specs = [pl.BlockSpec((tm, D), lambda i: (i, 0))]
    in_specs += [pl.BlockSpec((tm, a.shape[1]), lambda i: (i, 0)) for a in pre_a]
    in_specs += [_resident(pw.shape) for pw in pre_w]
    in_specs += [_resident((1, D)), _resident((D, N))]
    args = [x, *pre_a, *pre_w, g.reshape(1, D), w]
    if gains is not None:
        gt = jnp.broadcast_to(gains[:, :, None], gains.shape + (LANE,)).astype(F32)
        in_specs.append(_resident(gt.shape))
        args.append(gt)
    if rope is not None:
        in_specs += [pl.BlockSpec((ROT_DIM // 2, tm), lambda i: (0, i))] * 2
        args += list(rope)
    if fox:
        in_specs.append(_resident((FOX_HEADS, LANE)))
        args.append(jnp.broadcast_to(b_forget[:, None], (FOX_HEADS, LANE)).astype(F32))
    out_specs, out_shapes = [], []
    for kind, nh in out_heads:
        if kind == "q":
            out_specs.append(pl.BlockSpec((nh, KPAD, tm), lambda i: (0, 0, i)))
            out_shapes.append(jax.ShapeDtypeStruct((nh, KPAD, S), BF16))
        elif kind == "k":
            out_specs.append(pl.BlockSpec((nh, tm, KPAD), lambda i: (0, i, 0)))
            out_shapes.append(jax.ShapeDtypeStruct((nh, S, KPAD), BF16))
        else:
            out_specs.append(pl.BlockSpec((nh, VROWS, tm), lambda i: (0, 0, i)))
            out_shapes.append(jax.ShapeDtypeStruct((nh, VROWS, S), BF16))
    if fox:
        out_specs.append(pl.BlockSpec((IDX_HEADS, tm), lambda i: (0, i)))
        out_shapes.append(jax.ShapeDtypeStruct((IDX_HEADS, S), F32))
    if kmean:
        nhk = out_heads[1][1]
        out_specs.append(pl.BlockSpec((1, nhk, HEAD_DIM, LANE), lambda i: (i, 0, 0, 0)))
        out_shapes.append(jax.ShapeDtypeStruct((S // tm, nhk, HEAD_DIM, LANE), F32))
    if pre_a:
        out_specs.append(pl.BlockSpec((tm, D), lambda i: (i, 0)))
        out_shapes.append(jax.ShapeDtypeStruct((S, D), F32))
    return pl.pallas_call(
        functools.partial(_heads_kernel, plan=tuple(plan), tm=tm, n_gain=0 if gains is None else gains.shape[0],
                          rope=rope is not None, fox=fox, kmean=kmean, n_pre=len(pre_a)),
        grid=(S // tm,),
        in_specs=in_specs,
        out_specs=out_specs,
        out_shape=out_shapes,
        scratch_shapes=[pltpu.VMEM((FOX_HEADS, 1), F32)] if fox else [],
        compiler_params=_cparams(),
        name="heads_proj",
    )(*args)


def _heads_update(units, qT_ref, k_ref, vT_ref, s_ref, p_ref, m_ref, acc_ref, biased):
    _, tku, tqu = s_ref.shape
    rc = min(ATTN_ROWS, tku)
    n = len(units)
    m_new = alpha = None
    for t in range(n + 1):
        mx = None
        if t < n:
            h, q0, dg, k0 = units[t]
            qT = qT_ref[h, :, q0:q0 + tqu]
        if t >= 1:
            hb, qb, _, kb = units[t - 1]
            m_b, alpha_b = m_new, alpha
        for r0 in range(0, tku, rc):
            if t < n:
                s = jnp.dot(k_ref[h, k0 + r0:k0 + r0 + rc, :], qT, preferred_element_type=F32)
                s = biased(s, h, q0, r0, dg, k0)
                s_ref[t % 2, r0:r0 + rc, :] = s
                cm = jnp.max(s.reshape(rc // SUBLANE, SUBLANE, tqu), axis=0)
                mx = cm if mx is None else jnp.maximum(mx, cm)
            if t >= 1:
                p_ref[(t - 1) % 2, r0:r0 + rc, :] = jnp.exp2(s_ref[(t - 1) % 2, r0:r0 + rc, :] - m_b).astype(BF16)
        if t >= 1:
            pv = jnp.dot(vT_ref[hb, :, kb:kb + tku], p_ref[(t - 1) % 2], preferred_element_type=F32)
            acc_ref[hb, :, qb:qb + tqu] = alpha_b * acc_ref[hb, :, qb:qb + tqu] + pv
        if t < n:
            m_prev = m_ref[h, :, q0:q0 + tqu]
            m_new = jnp.maximum(m_prev, jnp.max(mx, axis=0, keepdims=True))
            alpha = jnp.exp2(m_prev - m_new)
            m_ref[h, :, q0:q0 + tqu] = m_new


def _attn_init(m_ref, acc_ref):
    m_ref[...] = jnp.full(m_ref.shape, NEG, F32)
    acc_ref[...] = jnp.zeros(acc_ref.shape, F32)


def _attn_finish(o_ref, acc_ref, nh):
    hd = HEAD_DIM
    out = lambda h: acc_ref[h, :hd, :] / acc_ref[h, hd:hd + 1, :]
    for h in range(0, nh, 2):
        o_ref[:, h * hd:(h + 2) * hd] = jnp.concatenate([out(h), out(h + 1)], axis=0).T.astype(o_ref.dtype)


def _attn_kernel(qi_ref, kj_ref, *refs, mode, nh, tq, tk, nsub):
    if mode == "moba":
        qT_ref, k_ref, vT_ref, sb_ref, o_ref, m_ref, acc_ref, s_ref, p_ref, cb_ref = refs
    elif mode == "causal":
        qT_ref, k_ref, vT_ref, o_ref, m_ref, acc_ref, s_ref, p_ref, cb_ref = refs
    else:
        qT_ref, k_ref, vT_ref, o_ref, m_ref, acc_ref, s_ref, p_ref = refs
    p = pl.program_id(0)
    i = qi_ref[p]
    j = kj_ref[p]
    causal = mode != "none"
    tqu = tq // nsub

    if causal:
        @pl.when(p == 0)
        def _():
            row = lax.broadcasted_iota(jnp.int32, (tk, tqu), 0)
            col = lax.broadcasted_iota(jnp.int32, (tk, tqu), 1)
            cb_ref[...] = jnp.where(row <= col, 0.0, NEG)

    @pl.when(j == 0)
    def _():
        _attn_init(m_ref, acc_ref)

    def biased(s, h, q0, r0, diag, k0):
        if mode == "moba":
            s = s + sb_ref[h, pl.ds(j * (tk // MOBA_BLOCK) + r0 // MOBA_BLOCK, 1), q0:q0 + tqu]
        if diag:
            s = s + cb_ref[r0:r0 + s.shape[0], :]
        return s

    def process(first):
        units = [(h, q * tqu, q == first, 0) for h in range(nh) for q in range(max(first, 0), nsub)]
        _heads_update(units, qT_ref, k_ref, vT_ref, s_ref, p_ref, m_ref, acc_ref, biased)

    if causal:
        d = j - nsub * i
        pl.when(d < 0)(lambda: process(-1))
        for q in range(nsub):
            pl.when(d == q)(functools.partial(process, q))
        last = d == nsub - 1
    else:
        process(-1)
        last = j == 0

    @pl.when(last)
    def _():
        _attn_finish(o_ref, acc_ref, nh)


def _attention(mode, qT, k, vT, extra, tq, tk, nsub):
    nh, _, S = qT.shape
    hd = HEAD_DIM
    nq = S // tq
    if mode == "none":
        pairs = [(i, 0) for i in range(nq)]
    else:
        assert tq == nsub * tk
        pairs = [(i, j) for i in range(nq) for j in range(nsub * (i + 1))]
    qi = jnp.asarray(np.array([a for a, _ in pairs], np.int32))
    kj = jnp.asarray(np.array([b for _, b in pairs], np.int32))
    in_specs = [pl.BlockSpec((nh, KPAD, tq), lambda p, qi, kj: (0, 0, qi[p])),
                pl.BlockSpec((nh, tk, KPAD), lambda p, qi, kj: (0, kj[p], 0)),
                pl.BlockSpec((nh, VROWS, tk), lambda p, qi, kj: (0, 0, kj[p]))]
    if mode == "moba":
        nb = extra[0].shape[1]
        in_specs += [pl.BlockSpec((nh, nb, tq), lambda p, qi, kj: (0, 0, qi[p]))]
    tqu = tq // nsub
    scratch = [pltpu.VMEM((nh, 1, tq), F32), pltpu.VMEM((nh, VROWS, tq), F32),
               pltpu.VMEM((2, tk, tqu), F32), pltpu.VMEM((2, tk, tqu), BF16)]
    if mode != "none":
        scratch.append(pltpu.VMEM((tk, tqu), F32))
    return pl.pallas_call(
        functools.partial(_attn_kernel, mode=mode, nh=nh, tq=tq, tk=tk, nsub=nsub),
        grid_spec=pltpu.PrefetchScalarGridSpec(
            num_scalar_prefetch=2,
            grid=(len(pairs),),
            in_specs=in_specs,
            out_specs=pl.BlockSpec((tq, nh * hd), lambda p, qi, kj: (qi[p], 0)),
            scratch_shapes=scratch),
        out_shape=jax.ShapeDtypeStruct((S, nh * hd), BF16),
        compiler_params=_cparams(),
        name="attn_" + mode,
    )(qi, kj, qT, k, vT, *extra)


def _gate_kernel(qT_ref, km_ref, o_ref, *, nh, nb, tq):
    i = pl.program_id(0)
    n = lax.broadcasted_iota(jnp.int32, (nb, tq), 0)
    own = (i * tq + lax.broadcasted_iota(jnp.int32, (nb, tq), 1)) // MOBA_BLOCK
    ninf = jnp.float32(-jnp.inf)

    def body(h, carry):
        g = jnp.dot(km_ref[h], qT_ref[h], preferred_element_type=F32)
        g = jnp.where(n < own, g, ninf)
        keep = n == own
        for _ in range(MOBA_TOPK):
            mx = jnp.max(g, axis=0, keepdims=True)
            first = jnp.min(jnp.where((g == mx) & (mx > ninf), n, nb), axis=0, keepdims=True)
            pick = n == first
            keep = keep | pick
            g = jnp.where(pick, ninf, g)
        o_ref[h] = jnp.where(keep, 0.0, NEG)
        return carry

    lax.fori_loop(0, nh, body, 0, unroll=4)


def _moba_gate(qT, kmean, tq):
    nh, _, S = qT.shape
    nb = kmean.shape[1]
    return pl.pallas_call(
        functools.partial(_gate_kernel, nh=nh, nb=nb, tq=tq),
        grid=(S // tq,),
        in_specs=[pl.BlockSpec((nh, KPAD, tq), lambda i: (0, 0, i)), _resident((nh, nb, KPAD))],
        out_specs=pl.BlockSpec((nh, nb, tq), lambda i: (0, 0, i)),
        out_shape=jax.ShapeDtypeStruct((nh, nb, S), F32),
        compiler_params=_cparams(),
        name="moba_gate",
    )(qT, kmean)


def _dsa_kernel(qi_ref, kj_ref, ph_ref, ikb_ref, kvb_ref, nk_ref,
                iqT_ref, ik_ref, w_ref, qT_ref, k_ref, vT_ref, o_ref,
                hi_ref, lo_ref, thi_ref, tlo_ref, cut_ref, m_ref, acc_ref, s_ref, p_ref, bias_ref,
                *, nih, nh, tq, tk, kpa, topk, seq, chunk):
    p = pl.program_id(0)
    i = qi_ref[p]
    j = kj_ref[p]
    phase = ph_ref[p]
    nkb = nk_ref[p]
    I16, I32 = jnp.int16, jnp.int32
    lo_min = -(2 ** 15)
    one, zero = jnp.ones((), I16), jnp.zeros((), I16)
    row16 = lambda x: x.astype(I16)

    @pl.when(phase == 0)
    def _scores():
        ik = ik_ref[...]
        sc = jnp.zeros((tk, tq), F32)
        for h in range(nih):
            d = jnp.dot(ik, iqT_ref[h], preferred_element_type=F32)
            sc = sc + w_ref[h:h + 1, :] * jnp.maximum(d, 0.0)
        b = lax.bitcast_convert_type(sc, I32)
        key = b ^ ((b >> 31) & 0x7FFFFFFF)
        row = j * tk + lax.broadcasted_iota(I32, (tk, tq), 0)
        col = i * tq + lax.broadcasted_iota(I32, (tk, tq), 1)
        key = jnp.where(row <= col, key, INT_MIN)
        rows = pl.ds(pl.multiple_of(j * tk, tk), tk)
        hi_ref[rows, :] = (key >> 16).astype(I16)
        lo_ref[rows, :] = ((key & 0xFFFF) + lo_min).astype(I16)

        @pl.when((j == nkb - 1) & (nkb % kpa != 0))
        def _():
            for u in range(1, kpa):
                rest = pl.ds(pl.multiple_of((j + u) * tk, tk), tk)
                hi_ref[rest, :] = jnp.full((tk, tq), lo_min, I16)
                lo_ref[rest, :] = jnp.full((tk, tq), lo_min, I16)

    def count(hit):
        def body(c, acc):
            r0 = pl.multiple_of(c * chunk, chunk)
            h = hit(hi_ref[pl.ds(r0, chunk), :], lo_ref[pl.ds(r0, chunk), :], r0)
            part = h[0:2 * SUBLANE]
            for r in range(2 * SUBLANE, chunk, 2 * SUBLANE):
                part = part + h[r:r + 2 * SUBLANE]
            return acc + part.astype(I32)
        acc = lax.fori_loop(0, nkb * (tk // chunk), body, jnp.zeros((2 * SUBLANE, tq), I32))
        return jnp.sum(acc, axis=0, keepdims=True)

    def kth_largest(pick, k):
        c0 = count(lambda hi, lo, r0: jnp.where(pick(hi, lo) >= zero, one, zero))
        t = jnp.where(c0 >= k, 0, lo_min)

        def bit_body(it, t):
            cand = t | jnp.left_shift(jnp.int32(1), 14 - it)
            c16 = row16(cand)
            cnt = count(lambda hi, lo, r0: jnp.where(pick(hi, lo) >= c16, one, zero))
            return jnp.where(cnt >= k, cand, t)

        return lax.fori_loop(0, 15, bit_body, t)

    @pl.when((phase == 1) & (j == 0))
    def _select():
        thi = kth_largest(lambda hi, lo: hi, topk)
        thi16 = row16(thi)
        above = count(lambda hi, lo, r0: jnp.where(hi > thi16, one, zero))
        need = topk - above

        def mark(c, carry):
            rows = pl.ds(pl.multiple_of(c * chunk, chunk), chunk)
            lo_ref[rows, :] = jnp.where(hi_ref[rows, :] == thi16, lo_ref[rows, :], jnp.full((), lo_min, I16))
            return carry

        lax.fori_loop(0, nkb * (tk // chunk), mark, 0)
        tlo = kth_largest(lambda hi, lo: lo, need)
        tlo = jnp.where((thi == lo_min) & (tlo == lo_min), lo_min + 1, tlo)
        tlo16 = row16(tlo)
        thi_ref[...] = thi
        tlo_ref[...] = tlo
        cut_ref[...] = jnp.full((1, tq), seq, I32)
        in_bucket = lambda hi, x: jnp.where(hi == thi16, x, zero)
        cge = above + count(lambda hi, lo, r0: in_bucket(hi, jnp.where(lo >= tlo16, one, zero)))

        @pl.when(jnp.max(cge) > topk)
        def _ties():
            need_tie = need - count(lambda hi, lo, r0: jnp.where(lo > tlo16, one, zero))
            nbits = max(1, int(seq - 1).bit_length())

            def pos_body(it, x):
                cand = x + jnp.left_shift(jnp.int32(1), nbits - 1 - it)
                before = lambda r0: jnp.where(r0 + lax.broadcasted_iota(I32, (chunk, tq), 0) < cand, 1, 0).astype(I16)
                c = count(lambda hi, lo, r0: in_bucket(hi, jnp.where(lo == tlo16, before(r0), zero)))
                return jnp.where(c < need_tie, cand, x)

            x = lax.fori_loop(0, nbits, pos_body, jnp.zeros((1, tq), I32))
            cut_ref[...] = jnp.where(cge > topk, x, seq)

        _attn_init(m_ref, acc_ref)

    @pl.when(phase == 1)
    def _attend():
        tka = kpa * tk
        rows = pl.ds(pl.multiple_of(j * tka, tka), tka)
        hi, lo = hi_ref[rows, :], lo_ref[rows, :]
        thi16, tlo16 = row16(thi_ref[...]), row16(tlo_ref[...])
        pos = j * tka + lax.broadcasted_iota(I32, (tka, tq), 0)
        kept = jnp.where(pos <= cut_ref[...], 1, 0).astype(I16)
        tie = jnp.where(lo == tlo16, kept, zero)
        sel = jnp.where(hi > thi16, one, jnp.where(hi == thi16, jnp.where(lo > tlo16, one, tie), zero))
        bias_ref[...] = jnp.where(sel.astype(I32) > 0, 0.0, NEG)

        def biased(s, h, q0, r0, diag, k0):
            return s + bias_ref[k0 + r0:k0 + r0 + s.shape[0], :]

        units = [(h, 0, False, k0) for k0 in range(0, tka, tk) for h in range(nh)]
        _heads_update(units, qT_ref, k_ref, vT_ref, s_ref, p_ref, m_ref, acc_ref, biased)

        @pl.when(j == (nkb + kpa - 1) // kpa - 1)
        def _():
            _attn_finish(o_ref, acc_ref, nh)


def _dsa_attention(iqT, ik, wT, qT, k, vT, tq, tk, topk):
    nh, _, S = qT.shape
    hd = HEAD_DIM
    nih = iqT.shape[0]
    kpa = 2 if S % (2 * tk) == 0 else 1
    tka = kpa * tk
    steps = []
    for i in range(S // tq):
        nkb = (i * tq + tq - 1) // tk + 1
        steps += [(i, j, 0, j, 0, nkb) for j in range(nkb)]
        steps += [(i, j, 1, nkb - 1, j, nkb) for j in range(-(-nkb // kpa))]
    tab = [jnp.asarray(np.array([s[c] for s in steps], np.int32)) for c in range(6)]
    chunk = min(tk, 512)
    qmap = lambda p, qi, kj, ph, ikb, kvb, nk: (0, 0, qi[p])
    return pl.pallas_call(
        functools.partial(_dsa_kernel, nih=nih, nh=nh, tq=tq, tk=tk, kpa=kpa, topk=topk, seq=S, chunk=chunk),
        grid_spec=pltpu.PrefetchScalarGridSpec(
            num_scalar_prefetch=6,
            grid=(len(steps),),
            in_specs=[pl.BlockSpec((nih, KPAD, tq), qmap),
                      pl.BlockSpec((tk, KPAD), lambda p, qi, kj, ph, ikb, kvb, nk: (ikb[p], 0)),
                      pl.BlockSpec((nih, tq), lambda p, qi, kj, ph, ikb, kvb, nk: (0, qi[p])),
                      pl.BlockSpec((nh, KPAD, tq), qmap),
                      pl.BlockSpec((nh, tka, KPAD), lambda p, qi, kj, ph, ikb, kvb, nk: (0, kvb[p], 0)),
                      pl.BlockSpec((nh, VROWS, tka), lambda p, qi, kj, ph, ikb, kvb, nk: (0, 0, kvb[p]))],
            out_specs=pl.BlockSpec((tq, nh * hd), lambda p, qi, kj, ph, ikb, kvb, nk: (qi[p], 0)),
            scratch_shapes=[pltpu.VMEM((S, tq), jnp.int16), pltpu.VMEM((S, tq), jnp.int16),
                            pltpu.VMEM((1, tq), jnp.int32), pltpu.VMEM((1, tq), jnp.int32),
                            pltpu.VMEM((1, tq), jnp.int32),
                            pltpu.VMEM((nh, 1, tq), F32), pltpu.VMEM((nh, VROWS, tq), F32),
                            pltpu.VMEM((2, tk, tq), F32), pltpu.VMEM((2, tk, tq), BF16),
                            pltpu.VMEM((tka, tq), F32)]),
        out_shape=jax.ShapeDtypeStruct((S, nh * hd), BF16),
        compiler_params=_cparams(),
        name="dsa",
    )(*tab, iqT, ik, wT, qT, k, vT)


def kernel(x, mem, ffn1_norm, ffn1_w_in, ffn1_w_out, mix_norm, even_w_in, even_b_forget, fox_q_norm, fox_k_norm, dsa_q_norm, dsa_k_norm, odd_w_in, moba_q_norm, moba_k_norm, mix_w_out, mem_norm_x, mem_norm_m, mem_wq, mem_wkv, mem_q_norm, mem_k_norm, mem_wo, ffn2_norm, ffn2_w_in, ffn2_w_out):
    _, S, D = x.shape
    M = mem.shape[1]
    depth = ffn1_norm.shape[0]
    tm = min(512, S)
    ta = min(512, S)
    nsub = 2 if S % (2 * ta) == 0 else 1
    tdq = min(512, S)
    pos = jnp.arange(S, dtype=F32)
    inv = ROPE_THETA ** (-jnp.arange(0, ROT_DIM, 2, dtype=F32) / ROT_DIM)
    ang = inv[:, None] * pos[None, :]
    rope = (jnp.cos(ang), jnp.sin(ang))
    W8 = 8 * HEAD_DIM
    zcols = lambda n: jnp.zeros((D, n), F32)

    h = x[0]
    m0 = mem[0]
    for layer in range(depth):
        h = _ffn(h, ffn1_norm[layer], ffn1_w_in[layer].astype(BF16), ffn1_w_out[layer].astype(BF16), tm)
        if layer % 2 == 0:
            e = layer // 2
            o = np.cumsum([0, W8, W8, W8, FOX_HEADS, W8, W8, W8, IDX_HEADS * IDX_DIM, IDX_DIM, IDX_HEADS])
            w = even_w_in[e]
            seg = lambda n: w[:, o[n]:o[n + 1]]
            w = jnp.concatenate([seg(0), seg(1), seg(2), seg(4), seg(5), seg(6), seg(7),
                                 seg(8), zcols(KPAD - IDX_DIM),
                                 seg(3), seg(9), zcols(LANE - FOX_HEADS - IDX_HEADS)], axis=1).astype(BF16)
            plan = [("q", 0, 8, 0, 0, False, 0), ("k", W8, 8, 0, 1, False, 1), ("v", 2 * W8, 8, 0, None, False, 2),
                    ("q", 3 * W8, 8, 0, 2, True, 3), ("k", 4 * W8, 8, 0, 3, True, 4), ("v", 5 * W8, 8, 0, None, False, 5),
                    ("q", 6 * W8, 8, 0, None, True, 6), ("k", 7 * W8, 1, 0, None, True, 7)]
            gains = jnp.stack([fox_q_norm[e], fox_k_norm[e], dsa_q_norm[e], dsa_k_norm[e]])
            fqT, fk, fvT, dqT, dk, dvT, iqT, ik, wT = _heads_proj(
                h, mix_norm[layer], w, plan,
                [("q", 8), ("k", 8), ("v", 8), ("q", 8), ("k", 8), ("v", 8), ("q", 8), ("k", 1)],
                tm, gains=gains, rope=rope, b_forget=even_b_forget[e])
            fox = _attention("causal", fqT, fk, fvT, (), nsub * ta, ta, nsub)
            dsa = _dsa_attention(iqT, ik[0], wT, dqT, dk, dvT, tdq, ta, min(DSA_TOPK, S // 4))
            wo = mix_w_out[layer].astype(BF16)
            mixed = ([fox, dsa], [wo[:W8], wo[W8:]])
        else:
            od = layer // 2
            w = odd_w_in[od].astype(BF16)
            plan = [("q", 0, 8, 0, 0, True, 0), ("q", W8, 8, 8, 0, True, 0),
                    ("k", 2 * W8, 8, 0, 1, True, 1), ("k", 3 * W8, 8, 8, 1, True, 1),
                    ("v", 4 * W8, 8, 0, None, False, 2), ("v", 5 * W8, 8, 8, None, False, 2)]
            gains = jnp.stack([moba_q_norm[od], moba_k_norm[od]])
            qT, k, vT, km = _heads_proj(h, mix_norm[layer], w, plan, [("q", 16), ("k", 16), ("v", 16)],
                                        tm, gains=gains, rope=rope, kmean=True)
            nbt = tm // MOBA_BLOCK
            km = jnp.transpose(km[..., :nbt], (1, 0, 3, 2)).reshape(MOBA_HEADS, S // MOBA_BLOCK, HEAD_DIM)
            km = jnp.pad(km, ((0, 0), (0, 0), (0, KPAD - HEAD_DIM))).astype(BF16)
            selb = _moba_gate(qT, km, ta)
            moba = _attention("moba", qT, k, vT, (selb,), nsub * ta, ta, nsub)
            mixed = ([moba], [mix_w_out[layer].astype(BF16)])
        mw = MEM_HEADS * HEAD_DIM
        mqT, h = _heads_proj(h, mem_norm_x[layer], mem_wq[layer].astype(BF16),
                             [("q", 0, MEM_HEADS, 0, 0, False, 0)], [("q", MEM_HEADS)], tm,
                             gains=mem_q_norm[layer][None], pre=mixed)
        mk, mvT = _heads_proj(m0, mem_norm_m[layer], mem_wkv[layer].astype(BF16),
                              [("k", 0, MEM_HEADS, 0, 0, False, 0), ("v", mw, MEM_HEADS, 0, None, False, 1)],
                              [("k", MEM_HEADS), ("v", MEM_HEADS)], M, gains=mem_k_norm[layer][None])
        ma = _attention("none", mqT, mk, mvT, (), nsub * ta, M, nsub)
        h = _ffn(h, ffn2_norm[layer], ffn2_w_in[layer].astype(BF16), ffn2_w_out[layer].astype(BF16), tm,
                 pre=(ma, mem_wo[layer].astype(BF16)))
    return h[None]
```

```python
import functools

import numpy as np
import jax
import jax.numpy as jnp
from jax import lax
from jax.experimental import pallas as pl
from jax.experimental.pallas import tpu as pltpu

HEAD_DIM = 64
ROT_DIM = HEAD_DIM // 4
ROPE_THETA = 500000.0
FOX_HEADS = 8
DSA_HEADS = 8
MOBA_HEADS = 16
IDX_HEADS = 8
IDX_DIM = 64
DSA_TOPK = 256
MOBA_BLOCK = 256
MOBA_TOPK = 3
MEM_HEADS = 4
RMS_EPS = 1e-6

NEG = -1e30
INT_MIN = -(2 ** 31)
LANE = 128
SUBLANE = 8
KPAD = 128
VROWS = HEAD_DIM + 16
ATTN_ROWS = 256
VMEM_LIMIT = 56 * 1024 * 1024
LOG2E = 1.4426950408889634
QSCALE = HEAD_DIM ** -0.5 * LOG2E

F32 = jnp.float32
BF16 = jnp.bfloat16


def _cparams(n_axes=1):
    return pltpu.CompilerParams(dimension_semantics=("arbitrary",) * n_axes, vmem_limit_bytes=VMEM_LIMIT)


def _resident(shape):
    nd = len(shape)
    return pl.BlockSpec(shape, lambda *_: (0,) * nd, pipeline_mode=pl.Buffered(1))


def _rms_bf16(x, g):
    ms = jnp.mean(x * x, axis=-1, keepdims=True)
    return (x * lax.rsqrt(ms + RMS_EPS) * g).astype(BF16)


def _ffn_kernel(x_ref, *refs, fc, pre):
    x = x_ref[...]
    if pre:
        x = x + jnp.dot(refs[0][...], refs[1][...], preferred_element_type=F32)
        refs = refs[2:]
    g_ref, wa_ref, wb_ref, wo_ref, o_ref = refs
    xn = _rms_bf16(x, g_ref[...])
    acc = jnp.zeros(x.shape, F32)
    for c in range(wa_ref.shape[1] // fc):
        a = jnp.dot(xn, wa_ref[:, c * fc:(c + 1) * fc], preferred_element_type=F32)
        b = jnp.dot(xn, wb_ref[:, c * fc:(c + 1) * fc], preferred_element_type=F32)
        gate = (a * jax.nn.sigmoid(a) * b).astype(BF16)
        acc = acc + jnp.dot(gate, wo_ref[c * fc:(c + 1) * fc, :], preferred_element_type=F32)
    o_ref[...] = x + 0.5 * acc


def _ffn(h, g, w_in, w_out, tm, pre=None):
    S, D = h.shape
    Fh = w_out.shape[0]
    in_specs, args = [pl.BlockSpec((tm, D), lambda i: (i, 0))], [h]
    if pre is not None:
        in_specs += [pl.BlockSpec((tm, pre[0].shape[1]), lambda i: (i, 0)), _resident(pre[1].shape)]
        args += list(pre)
    in_specs += [_resident((1, D)),
                 pl.BlockSpec((D, Fh), lambda i: (0, 0), pipeline_mode=pl.Buffered(1)),
                 pl.BlockSpec((D, Fh), lambda i: (0, 1), pipeline_mode=pl.Buffered(1)),
                 _resident((Fh, D))]
    return pl.pallas_call(
        functools.partial(_ffn_kernel, fc=256, pre=pre is not None),
        grid=(S // tm,),
        in_specs=in_specs,
        out_specs=pl.BlockSpec((tm, D), lambda i: (i, 0)),
        out_shape=jax.ShapeDtypeStruct((S, D), F32),
        compiler_params=_cparams(),
        name="ffn",
    )(*args, g.reshape(1, D), w_in, w_in, w_out)


def _lanes(ref, idx, tm):
    return jnp.concatenate([ref[idx]] * (tm // LANE), axis=1)


def _bf16_split3(x):
    a = x.astype(BF16).astype(F32)
    b = (x - a).astype(BF16).astype(F32)
    return a, b, x - a - b


def _heads_kernel(*refs, plan, tm, n_gain, rope, fox, kmean, n_pre):
    it = iter(refs)
    x_ref = next(it)
    pre_a = [next(it) for _ in range(n_pre)]
    pre_w = [next(it) for _ in range(n_pre)]
    g_ref, w_ref = next(it), next(it)
    gn_ref = next(it) if n_gain else None
    cos = sin = None
    if rope:
        cos, sin = next(it)[...], next(it)[...]
    bf_ref = next(it) if fox else None
    n_out = 1 + max(p[6] for p in plan)
    outs = [next(it) for _ in range(n_out)]
    wt_ref = next(it) if fox else None
    km_ref = next(it) if kmean else None
    res_ref = next(it) if n_pre else None
    carry_ref = next(it) if fox else None

    x = x_ref[...]
    if n_pre:
        for a_ref, pw_ref in zip(pre_a, pre_w):
            x = x + jnp.dot(a_ref[...], pw_ref[...], preferred_element_type=F32)
        res_ref[...] = x
    xn = _rms_bf16(x, g_ref[...])
    proj_t = lambda col, width: jnp.dot(xn, w_ref[:, col:col + width], preferred_element_type=F32).T
    zeros_pad = jnp.zeros((KPAD - HEAD_DIM, tm), F32)
    half = ROT_DIM // 2

    fq_rows = fk_rows = None
    if fox:
        @pl.when(pl.program_id(0) == 0)
        def _():
            carry_ref[...] = jnp.zeros(carry_ref.shape, F32)

        gT = proj_t(w_ref.shape[1] - LANE, LANE)
        xg = gT[0:FOX_HEADS] + _lanes(bf_ref, slice(None), tm)
        lf = jnp.minimum(xg, 0.0) - jnp.log1p(jnp.exp(-jnp.abs(xg)))
        tri = jnp.where(lax.broadcasted_iota(jnp.int32, (tm, tm), 0)
                        <= lax.broadcasted_iota(jnp.int32, (tm, tm), 1), 1.0, 0.0).astype(BF16)
        cs = sum(jnp.dot(part.astype(BF16), tri, preferred_element_type=F32) for part in _bf16_split3(lf))
        fcum = (carry_ref[...] + cs) * LOG2E
        carry_ref[...] = carry_ref[...] + jnp.sum(lf, axis=1, keepdims=True)
        wt_ref[...] = gT[FOX_HEADS:FOX_HEADS + IDX_HEADS] * (IDX_HEADS ** -0.5 * IDX_DIM ** -0.5)
        rid = lax.broadcasted_iota(jnp.int32, (SUBLANE, tm), 0)
        fq_rows, fk_rows = [], []
        for h in range(FOX_HEADS):
            f1, f2, f3 = _bf16_split3(fcum[h:h + 1])
            sel = lambda a, b, c, one: jnp.where(rid == 0, a, jnp.where(rid == 1, b, jnp.where(rid == 2, c, one)))
            ones_lo = jnp.where(rid < 6, 1.0, 0.0)
            fq_rows.append(sel(f1, f2, f3, ones_lo))
            fk_rows.append(jnp.where(rid < 3, 1.0, jnp.where(rid == 3, -f1, jnp.where(rid == 4, -f2,
                           jnp.where(rid == 5, -f3, 0.0)))))

    for kind, col, nheads, head0, gain, use_rope, out in plan:
        width = -(-nheads * HEAD_DIM // LANE) * LANE
        segT = proj_t(col, width)
        o_ref = outs[out]
        for hh in range(nheads):
            r = segT[hh * HEAD_DIM:(hh + 1) * HEAD_DIM]
            if gain is not None:
                r = r * lax.rsqrt(jnp.mean(r * r, axis=0, keepdims=True) + RMS_EPS) * _lanes(gn_ref, gain, tm)
            if use_rope:
                x1, x2 = r[:half], r[half:ROT_DIM]
                r = jnp.concatenate([x1 * cos - x2 * sin, x2 * cos + x1 * sin, r[ROT_DIM:]], axis=0)
            h = head0 + hh
            if kind == "v":
                o_ref[h] = jnp.concatenate([r, jnp.ones((VROWS - HEAD_DIM, tm), F32)], axis=0).astype(BF16)
                continue
            if kind == "q":
                r = r * QSCALE
            extra = None
            if fox and out == 0:
                extra = fq_rows[h]
            elif fox and out == 1:
                extra = fk_rows[h]
            if extra is None:
                padded = jnp.concatenate([r, zeros_pad], axis=0)
            else:
                padded = jnp.concatenate([r, extra, zeros_pad[SUBLANE:]], axis=0)
            if kind == "q":
                o_ref[h] = padded.astype(BF16)
            else:
                o_ref[h] = padded.T.astype(BF16)
                if kmean:
                    lane = lax.broadcasted_iota(jnp.int32, (HEAD_DIM, LANE), 1)
                    tile = jnp.zeros((HEAD_DIM, LANE), F32)
                    for b in range(tm // MOBA_BLOCK):
                        mb = jnp.sum(r[:, b * MOBA_BLOCK:(b + 1) * MOBA_BLOCK], axis=1, keepdims=True)
                        tile = jnp.where(lane == b, mb * (1.0 / MOBA_BLOCK), tile)
                    km_ref[0, h] = tile


def _heads_proj(x, g, w, plan, out_heads, tm, gains=None, rope=None, b_forget=None, kmean=False, pre=None):
    S, D = x.shape
    N = w.shape[1]
    fox = b_forget is not None
    pre_a, pre_w = pre if pre is not None else ([], [])
    in_specs = [pl.BlockSpec((tm, D), lambda i: (i, 0))]
    in_specs += [pl.BlockSpec((tm, a.shape[1]), lambda i: (i, 0)) for a in pre_a]
    in_specs += [_resident(pw.shape) for pw in pre_w]
    in_specs += [_resident((1, D)), _resident((D, N))]
    args = [x, *pre_a, *pre_w, g.reshape(1, D), w]
    if gains is not None:
        gt = jnp.broadcast_to(gains[:, :, None], gains.shape + (LANE,)).astype(F32)
        in_specs.append(_resident(gt.shape))
        args.append(gt)
    if rope is not None:
        in_specs += [pl.BlockSpec((ROT_DIM // 2, tm), lambda i: (0, i))] * 2
        args += list(rope)
    if fox:
        in_specs.append(_resident((FOX_HEADS, LANE)))
        args.append(jnp.broadcast_to(b_forget[:, None], (FOX_HEADS, LANE)).astype(F32))
    out_specs, out_shapes = [], []
    for kind, nh in out_heads:
        if kind == "q":
            out_specs.append(pl.BlockSpec((nh, KPAD, tm), lambda i: (0, 0, i)))
            out_shapes.append(jax.ShapeDtypeStruct((nh, KPAD, S), BF16))
        elif kind == "k":
            out_specs.append(pl.BlockSpec((nh, tm, KPAD), lambda i: (0, i, 0)))
            out_shapes.append(jax.ShapeDtypeStruct((nh, S, KPAD), BF16))
        else:
            out_specs.append(pl.BlockSpec((nh, VROWS, tm), lambda i: (0, 0, i)))
            out_shapes.append(jax.ShapeDtypeStruct((nh, VROWS, S), BF16))
    if fox:
        out_specs.append(pl.BlockSpec((IDX_HEADS, tm), lambda i: (0, i)))
        out_shapes.append(jax.ShapeDtypeStruct((IDX_HEADS, S), F32))
    if kmean:
        nhk = out_heads[1][1]
        out_specs.append(pl.BlockSpec((1, nhk, HEAD_DIM, LANE), lambda i: (i, 0, 0, 0)))
        out_shapes.append(jax.ShapeDtypeStruct((S // tm, nhk, HEAD_DIM, LANE), F32))
    if pre_a:
        out_specs.append(pl.BlockSpec((tm, D), lambda i: (i, 0)))
        out_shapes.append(jax.ShapeDtypeStruct((S, D), F32))
    return pl.pallas_call(
        functools.partial(_heads_kernel, plan=tuple(plan), tm=tm, n_gain=0 if gains is None else gains.shape[0],
                          rope=rope is not None, fox=fox, kmean=kmean, n_pre=len(pre_a)),
        grid=(S // tm,),
        in_specs=in_specs,
        out_specs=out_specs,
        out_shape=out_shapes,
        scratch_shapes=[pltpu.VMEM((FOX_HEADS, 1), F32)] if fox else [],
        compiler_params=_cparams(),
        name="heads_proj",
    )(*args)


def _heads_update(units, qT_ref, k_ref, vT_ref, s_ref, p_ref, m_ref, acc_ref, biased):
    _, tk, tqu = s_ref.shape
    rc = min(ATTN_ROWS, tk)
    n = len(units)
    m_new = alpha = None
    for t in range(n + 1):
        mx = None
        if t < n:
            h, q0, dg = units[t]
            qT = qT_ref[h, :, q0:q0 + tqu]
        if t >= 1:
            hb, qb, _ = units[t - 1]
            m_b, alpha_b = m_new, alpha
        for r0 in range(0, tk, rc):
            if t < n:
                s = biased(jnp.dot(k_ref[h, r0:r0 + rc, :], qT, preferred_element_type=F32), h, q0, r0, dg)
                s_ref[t % 2, r0:r0 + rc, :] = s
                cm = jnp.max(s.reshape(rc // SUBLANE, SUBLANE, tqu), axis=0)
                mx = cm if mx is None else jnp.maximum(mx, cm)
            if t >= 1:
                p_ref[(t - 1) % 2, r0:r0 + rc, :] = jnp.exp2(s_ref[(t - 1) % 2, r0:r0 + rc, :] - m_b).astype(BF16)
        if t >= 1:
            pv = jnp.dot(vT_ref[hb], p_ref[(t - 1) % 2], preferred_element_type=F32)
            acc_ref[hb, :, qb:qb + tqu] = alpha_b * acc_ref[hb, :, qb:qb + tqu] + pv
        if t < n:
            m_prev = m_ref[h, :, q0:q0 + tqu]
            m_new = jnp.maximum(m_prev, jnp.max(mx, axis=0, keepdims=True))
            alpha = jnp.exp2(m_prev - m_new)
            m_ref[h, :, q0:q0 + tqu] = m_new


def _attn_init(m_ref, acc_ref):
    m_ref[...] = jnp.full(m_ref.shape, NEG, F32)
    acc_ref[...] = jnp.zeros(acc_ref.shape, F32)


def _attn_finish(o_ref, acc_ref, nh):
    hd = HEAD_DIM
    out = lambda h: acc_ref[h, :hd, :] / acc_ref[h, hd:hd + 1, :]
    for h in range(0, nh, 2):
        o_ref[:, h * hd:(h + 2) * hd] = jnp.concatenate([out(h), out(h + 1)], axis=0).T.astype(o_ref.dtype)


def _attn_kernel(qi_ref, kj_ref, *refs, mode, nh, tq, tk, nsub):
    if mode == "moba":
        qT_ref, k_ref, vT_ref, sb_ref, o_ref, m_ref, acc_ref, s_ref, p_ref, cb_ref = refs
    elif mode == "causal":
        qT_ref, k_ref, vT_ref, o_ref, m_ref, acc_ref, s_ref, p_ref, cb_ref = refs
    else:
        qT_ref, k_ref, vT_ref, o_ref, m_ref, acc_ref, s_ref, p_ref = refs
    p = pl.program_id(0)
    i = qi_ref[p]
    j = kj_ref[p]
    causal = mode != "none"
    tqu = tq // nsub

    if causal:
        @pl.when(p == 0)
        def _():
            row = lax.broadcasted_iota(jnp.int32, (tk, tqu), 0)
            col = lax.broadcasted_iota(jnp.int32, (tk, tqu), 1)
            cb_ref[...] = jnp.where(row <= col, 0.0, NEG)

    @pl.when(j == 0)
    def _():
        _attn_init(m_ref, acc_ref)

    def biased(s, h, q0, r0, diag):
        if mode == "moba":
            s = s + sb_ref[h, pl.ds(j * (tk // MOBA_BLOCK) + r0 // MOBA_BLOCK, 1), q0:q0 + tqu]
        if diag:
            s = s + cb_ref[r0:r0 + s.shape[0], :]
        return s

    def process(first):
        units = [(h, q * tqu, q == first) for h in range(nh) for q in range(max(first, 0), nsub)]
        _heads_update(units, qT_ref, k_ref, vT_ref, s_ref, p_ref, m_ref, acc_ref, biased)

    if causal:
        d = j - nsub * i
        pl.when(d < 0)(lambda: process(-1))
        for q in range(nsub):
            pl.when(d == q)(functools.partial(process, q))
        last = d == nsub - 1
    else:
        process(-1)
        last = j == 0

    @pl.when(last)
    def _():
        _attn_finish(o_ref, acc_ref, nh)


def _attention(mode, qT, k, vT, extra, tq, tk, nsub):
    nh, _, S = qT.shape
    hd = HEAD_DIM
    nq = S // tq
    if mode == "none":
        pairs = [(i, 0) for i in range(nq)]
    else:
        assert tq == nsub * tk
        pairs = [(i, j) for i in range(nq) for j in range(nsub * (i + 1))]
    qi = jnp.asarray(np.array([a for a, _ in pairs], np.int32))
    kj = jnp.asarray(np.array([b for _, b in pairs], np.int32))
    in_specs = [pl.BlockSpec((nh, KPAD, tq), lambda p, qi, kj: (0, 0, qi[p])),
                pl.BlockSpec((nh, tk, KPAD), lambda p, qi, kj: (0, kj[p], 0)),
                pl.BlockSpec((nh, VROWS, tk), lambda p, qi, kj: (0, 0, kj[p]))]
    if mode == "moba":
        nb = extra[0].shape[1]
        in_specs += [pl.BlockSpec((nh, nb, tq), lambda p, qi, kj: (0, 0, qi[p]))]
    tqu = tq // nsub
    scratch = [pltpu.VMEM((nh, 1, tq), F32), pltpu.VMEM((nh, VROWS, tq), F32),
               pltpu.VMEM((2, tk, tqu), F32), pltpu.VMEM((2, tk, tqu), BF16)]
    if mode != "none":
        scratch.append(pltpu.VMEM((tk, tqu), F32))
    return pl.pallas_call(
        functools.partial(_attn_kernel, mode=mode, nh=nh, tq=tq, tk=tk, nsub=nsub),
        grid_spec=pltpu.PrefetchScalarGridSpec(
            num_scalar_prefetch=2,
            grid=(len(pairs),),
            in_specs=in_specs,
            out_specs=pl.BlockSpec((tq, nh * hd), lambda p, qi, kj: (qi[p], 0)),
            scratch_shapes=scratch),
        out_shape=jax.ShapeDtypeStruct((S, nh * hd), BF16),
        compiler_params=_cparams(),
        name="attn_" + mode,
    )(qi, kj, qT, k, vT, *extra)


def _gate_kernel(qT_ref, km_ref, o_ref, *, nh, nb, tq):
    i = pl.program_id(0)
    n = lax.broadcasted_iota(jnp.int32, (nb, tq), 0)
    own = (i * tq + lax.broadcasted_iota(jnp.int32, (nb, tq), 1)) // MOBA_BLOCK
    ninf = jnp.float32(-jnp.inf)

    def body(h, carry):
        g = jnp.dot(km_ref[h], qT_ref[h], preferred_element_type=F32)
        g = jnp.where(n < own, g, ninf)
        keep = n == own
        for _ in range(MOBA_TOPK):
            mx = jnp.max(g, axis=0, keepdims=True)
            first = jnp.min(jnp.where((g == mx) & (mx > ninf), n, nb), axis=0, keepdims=True)
            pick = n == first
            keep = keep | pick
            g = jnp.where(pick, ninf, g)
        o_ref[h] = jnp.where(keep, 0.0, NEG)
        return carry

    lax.fori_loop(0, nh, body, 0, unroll=4)


def _moba_gate(qT, kmean, tq):
    nh, _, S = qT.shape
    nb = kmean.shape[1]
    return pl.pallas_call(
        functools.partial(_gate_kernel, nh=nh, nb=nb, tq=tq),
        grid=(S // tq,),
        in_specs=[pl.BlockSpec((nh, KPAD, tq), lambda i: (0, 0, i)), _resident((nh, nb, KPAD))],
        out_specs=pl.BlockSpec((nh, nb, tq), lambda i: (0, 0, i)),
        out_shape=jax.ShapeDtypeStruct((nh, nb, S), F32),
        compiler_params=_cparams(),
        name="moba_gate",
    )(qT, kmean)


def _dsa_kernel(qi_ref, kj_ref, ph_ref, ikb_ref, kvb_ref, nk_ref,
                iqT_ref, ik_ref, w_ref, qT_ref, k_ref, vT_ref, o_ref,
                sc_ref, thr_ref, cut_ref, m_ref, acc_ref, s_ref, p_ref, bias_ref,
                *, nih, nh, tq, tk, topk, seq, chunk):
    p = pl.program_id(0)
    i = qi_ref[p]
    j = kj_ref[p]
    phase = ph_ref[p]
    nkb = nk_ref[p]
    I32 = jnp.int32
    lowest = float(jnp.finfo(jnp.float32).min)

    def as_float(code):
        return lax.bitcast_convert_type(code ^ ((code >> 31) & 0x7FFFFFFF), F32)

    @pl.when(phase == 0)
    def _scores():
        ik = ik_ref[...]
        sc = jnp.zeros((tk, tq), F32)
        for h in range(nih):
            d = jnp.dot(ik, iqT_ref[h], preferred_element_type=F32)
            sc = sc + w_ref[h:h + 1, :] * jnp.maximum(d, 0.0)
        row = j * tk + lax.broadcasted_iota(I32, (tk, tq), 0)
        col = i * tq + lax.broadcasted_iota(I32, (tk, tq), 1)
        sc_ref[pl.ds(pl.multiple_of(j * tk, tk), tk), :] = jnp.where(row <= col, sc, -jnp.inf)

    def count(hit):
        def body(c, acc):
            r0 = pl.multiple_of(c * chunk, chunk)
            return acc + jnp.sum(hit(sc_ref[pl.ds(r0, chunk), :], r0).reshape(chunk // SUBLANE, SUBLANE, tq), axis=0)
        acc = lax.fori_loop(0, nkb * (tk // chunk), body, jnp.zeros((SUBLANE, tq), I32))
        return jnp.sum(acc, axis=0, keepdims=True)

    @pl.when((phase == 1) & (j == 0))
    def _select():
        c0 = count(lambda blk, r0: jnp.where(blk >= 0.0, 1, 0))
        code = jnp.where(c0 >= topk, 0, INT_MIN)

        def bit_body(it, code):
            cand = code | jnp.left_shift(jnp.int32(1), 30 - it)
            cand_f = as_float(cand)
            cnt = count(lambda blk, r0: jnp.where(blk >= cand_f, 1, 0))
            return jnp.where(cnt >= topk, cand, code)

        code = lax.fori_loop(0, 31, bit_body, code)
        thr = jnp.where(code == INT_MIN, lowest, as_float(code))
        thr_ref[...] = thr
        cut_ref[...] = jnp.full((1, tq), seq, I32)
        cge = count(lambda blk, r0: jnp.where(blk >= thr, 1, 0))

        @pl.when(jnp.max(cge) > topk)
        def _ties():
            need = topk - count(lambda blk, r0: jnp.where(blk > thr, 1, 0))
            nbits = max(1, int(seq - 1).bit_length())

            def pos_body(it, x):
                cand = x + jnp.left_shift(jnp.int32(1), nbits - 1 - it)
                pos = lambda r0: r0 + lax.broadcasted_iota(I32, (chunk, tq), 0)
                c = count(lambda blk, r0: jnp.where(blk == thr, jnp.where(pos(r0) < cand, 1, 0), 0))
                return jnp.where(c < need, cand, x)

            x = lax.fori_loop(0, nbits, pos_body, jnp.zeros((1, tq), I32))
            cut_ref[...] = jnp.where(cge > topk, x, seq)

        _attn_init(m_ref, acc_ref)

    @pl.when(phase == 1)
    def _attend():
        sc = sc_ref[pl.ds(pl.multiple_of(j * tk, tk), tk), :]
        thr = thr_ref[...]
        pos = j * tk + lax.broadcasted_iota(I32, (tk, tq), 0)
        tie = jnp.where(sc == thr, jnp.where(pos <= cut_ref[...], 0.0, NEG), NEG)
        bias_ref[...] = jnp.where(sc > thr, 0.0, tie)

        def biased(s, h, q0, r0, diag):
            return s + bias_ref[r0:r0 + s.shape[0], :]

        units = [(h, 0, False) for h in range(nh)]
        _heads_update(units, qT_ref, k_ref, vT_ref, s_ref, p_ref, m_ref, acc_ref, biased)

        @pl.when(j == nkb - 1)
        def _():
            _attn_finish(o_ref, acc_ref, nh)


def _dsa_attention(iqT, ik, wT, qT, k, vT, tq, tk, topk):
    nh, _, S = qT.shape
    hd = HEAD_DIM
    nih = iqT.shape[0]
    steps = []
    for i in range(S // tq):
        nkb = (i * tq + tq - 1) // tk + 1
        steps += [(i, j, 0, j, 0, nkb) for j in range(nkb)]
        steps += [(i, j, 1, nkb - 1, j, nkb) for j in range(nkb)]
    tab = [jnp.asarray(np.array([s[c] for s in steps], np.int32)) for c in range(6)]
    chunk = min(tk, 512)
    qmap = lambda p, qi, kj, ph, ikb, kvb, nk: (0, 0, qi[p])
    return pl.pallas_call(
        functools.partial(_dsa_kernel, nih=nih, nh=nh, tq=tq, tk=tk, topk=topk, seq=S, chunk=chunk),
        grid_spec=pltpu.PrefetchScalarGridSpec(
            num_scalar_prefetch=6,
            grid=(len(steps),),
            in_specs=[pl.BlockSpec((nih, KPAD, tq), qmap),
                      pl.BlockSpec((tk, KPAD), lambda p, qi, kj, ph, ikb, kvb, nk: (ikb[p], 0)),
                      pl.BlockSpec((nih, tq), lambda p, qi, kj, ph, ikb, kvb, nk: (0, qi[p])),
                      pl.BlockSpec((nh, KPAD, tq), qmap),
                      pl.BlockSpec((nh, tk, KPAD), lambda p, qi, kj, ph, ikb, kvb, nk: (0, kvb[p], 0)),
                      pl.BlockSpec((nh, VROWS, tk), lambda p, qi, kj, ph, ikb, kvb, nk: (0, 0, kvb[p]))],
            out_specs=pl.BlockSpec((tq, nh * hd), lambda p, qi, kj, ph, ikb, kvb, nk: (qi[p], 0)),
            scratch_shapes=[pltpu.VMEM((S, tq), F32),
                            pltpu.VMEM((1, tq), F32), pltpu.VMEM((1, tq), jnp.int32),
                            pltpu.VMEM((nh, 1, tq), F32), pltpu.VMEM((nh, VROWS, tq), F32),
                            pltpu.VMEM((2, tk, tq), F32), pltpu.VMEM((2, tk, tq), BF16),
                            pltpu.VMEM((tk, tq), F32)]),
        out_shape=jax.ShapeDtypeStruct((S, nh * hd), BF16),
        compiler_params=_cparams(),
        name="dsa",
    )(*tab, iqT, ik, wT, qT, k, vT)


def kernel(x, mem, ffn1_norm, ffn1_w_in, ffn1_w_out, mix_norm, even_w_in, even_b_forget, fox_q_norm, fox_k_norm, dsa_q_norm, dsa_k_norm, odd_w_in, moba_q_norm, moba_k_norm, mix_w_out, mem_norm_x, mem_norm_m, mem_wq, mem_wkv, mem_q_norm, mem_k_norm, mem_wo, ffn2_norm, ffn2_w_in, ffn2_w_out):
    _, S, D = x.shape
    M = mem.shape[1]
    depth = ffn1_norm.shape[0]
    tm = min(512, S)
    ta = min(512, S)
    nsub = 2 if S % (2 * ta) == 0 else 1
    tdq = min(512, S)
    pos = jnp.arange(S, dtype=F32)
    inv = ROPE_THETA ** (-jnp.arange(0, ROT_DIM, 2, dtype=F32) / ROT_DIM)
    ang = inv[:, None] * pos[None, :]
    rope = (jnp.cos(ang), jnp.sin(ang))
    W8 = 8 * HEAD_DIM
    zcols = lambda n: jnp.zeros((D, n), F32)

    h = x[0]
    m0 = mem[0]
    for layer in range(depth):
        h = _ffn(h, ffn1_norm[layer], ffn1_w_in[layer].astype(BF16), ffn1_w_out[layer].astype(BF16), tm)
        if layer % 2 == 0:
            e = layer // 2
            o = np.cumsum([0, W8, W8, W8, FOX_HEADS, W8, W8, W8, IDX_HEADS * IDX_DIM, IDX_DIM, IDX_HEADS])
            w = even_w_in[e]
            seg = lambda n: w[:, o[n]:o[n + 1]]
            w = jnp.concatenate([seg(0), seg(1), seg(2), seg(4), seg(5), seg(6), seg(7),
                                 seg(8), zcols(KPAD - IDX_DIM),
                                 seg(3), seg(9), zcols(LANE - FOX_HEADS - IDX_HEADS)], axis=1).astype(BF16)
            plan = [("q", 0, 8, 0, 0, False, 0), ("k", W8, 8, 0, 1, False, 1), ("v", 2 * W8, 8, 0, None, False, 2),
                    ("q", 3 * W8, 8, 0, 2, True, 3), ("k", 4 * W8, 8, 0, 3, True, 4), ("v", 5 * W8, 8, 0, None, False, 5),
                    ("q", 6 * W8, 8, 0, None, True, 6), ("k", 7 * W8, 1, 0, None, True, 7)]
            gains = jnp.stack([fox_q_norm[e], fox_k_norm[e], dsa_q_norm[e], dsa_k_norm[e]])
            fqT, fk, fvT, dqT, dk, dvT, iqT, ik, wT = _heads_proj(
                h, mix_norm[layer], w, plan,
                [("q", 8), ("k", 8), ("v", 8), ("q", 8), ("k", 8), ("v", 8), ("q", 8), ("k", 1)],
                tm, gains=gains, rope=rope, b_forget=even_b_forget[e])
            fox = _attention("causal", fqT, fk, fvT, (), nsub * ta, ta, nsub)
            dsa = _dsa_attention(iqT, ik[0], wT, dqT, dk, dvT, tdq, ta, min(DSA_TOPK, S // 4))
            wo = mix_w_out[layer].astype(BF16)
            mixed = ([fox, dsa], [wo[:W8], wo[W8:]])
        else:
            od = layer // 2
            w = odd_w_in[od].astype(BF16)
            plan = [("q", 0, 8, 0, 0, True, 0), ("q", W8, 8, 8, 0, True, 0),
                    ("k", 2 * W8, 8, 0, 1, True, 1), ("k", 3 * W8, 8, 8, 1, True, 1),
                    ("v", 4 * W8, 8, 0, None, False, 2), ("v", 5 * W8, 8, 8, None, False, 2)]
            gains = jnp.stack([moba_q_norm[od], moba_k_norm[od]])
            qT, k, vT, km = _heads_proj(h, mix_norm[layer], w, plan, [("q", 16), ("k", 16), ("v", 16)],
                                        tm, gains=gains, rope=rope, kmean=True)
            nbt = tm // MOBA_BLOCK
            km = jnp.transpose(km[..., :nbt], (1, 0, 3, 2)).reshape(MOBA_HEADS, S // MOBA_BLOCK, HEAD_DIM)
            km = jnp.pad(km, ((0, 0), (0, 0), (0, KPAD - HEAD_DIM))).astype(BF16)
            selb = _moba_gate(qT, km, ta)
            moba = _attention("moba", qT, k, vT, (selb,), nsub * ta, ta, nsub)
            mixed = ([moba], [mix_w_out[layer].astype(BF16)])
        mw = MEM_HEADS * HEAD_DIM
        mqT, h = _heads_proj(h, mem_norm_x[layer], mem_wq[layer].astype(BF16),
                             [("q", 0, MEM_HEADS, 0, 0, False, 0)], [("q", MEM_HEADS)], tm,
                             gains=mem_q_norm[layer][None], pre=mixed)
        mk, mvT = _heads_proj(m0, mem_norm_m[layer], mem_wkv[layer].astype(BF16),
                              [("k", 0, MEM_HEADS, 0, 0, False, 0), ("v", mw, MEM_HEADS, 0, None, False, 1)],
                              [("k", MEM_HEADS), ("v", MEM_HEADS)], M, gains=mem_k_norm[layer][None])
        ma = _attention("none", mqT, mk, mvT, (), nsub * ta, M, nsub)
        h = _ffn(h, ffn2_norm[layer], ffn2_w_in[layer].astype(BF16), ffn2_w_out[layer].astype(BF16), tm,
                 pre=(ma, mem_wo[layer].astype(BF16)))
    return h[None]
```

```python
import functools

import numpy as np
import jax
import jax.numpy as jnp
from jax import lax
from jax.experimental import pallas as pl
from jax.experimental.pallas import tpu as pltpu

HEAD_DIM = 64
ROT_DIM = HEAD_DIM // 4
ROPE_THETA = 500000.0
FOX_HEADS = 8
DSA_HEADS = 8
MOBA_HEADS = 16
IDX_HEADS = 8
IDX_DIM = 64
DSA_TOPK = 256
MOBA_BLOCK = 256
MOBA_TOPK = 3
MEM_HEADS = 4
RMS_EPS = 1e-6

NEG = -1e30
INT_MIN = -(2 ** 31)
LANE = 128
SUBLANE = 8
KPAD = 128
VROWS = HEAD_DIM + 16
ATTN_ROWS = 256
VMEM_LIMIT = 56 * 1024 * 1024
LOG2E = 1.4426950408889634
QSCALE = HEAD_DIM ** -0.5 * LOG2E

F32 = jnp.float32
BF16 = jnp.bfloat16


def _cparams(n_axes=1):
    return pltpu.CompilerParams(dimension_semantics=("arbitrary",) * n_axes, vmem_limit_bytes=VMEM_LIMIT)


def _resident(shape):
    nd = len(shape)
    return pl.BlockSpec(shape, lambda *_: (0,) * nd, pipeline_mode=pl.Buffered(1))


def _rms_bf16(x, g):
    ms = jnp.mean(x * x, axis=-1, keepdims=True)
    return (x * lax.rsqrt(ms + RMS_EPS) * g).astype(BF16)


def _ffn_kernel(x_ref, *refs, fc, pre):
    x = x_ref[...]
    if pre:
        x = x + jnp.dot(refs[0][...], refs[1][...], preferred_element_type=F32)
        refs = refs[2:]
    g_ref, wa_ref, wb_ref, wo_ref, o_ref = refs
    xn = _rms_bf16(x, g_ref[...])
    acc = jnp.zeros(x.shape, F32)
    for c in range(wa_ref.shape[1] // fc):
        a = jnp.dot(xn, wa_ref[:, c * fc:(c + 1) * fc], preferred_element_type=F32)
        b = jnp.dot(xn, wb_ref[:, c * fc:(c + 1) * fc], preferred_element_type=F32)
        gate = (a * jax.nn.sigmoid(a) * b).astype(BF16)
        acc = acc + jnp.dot(gate, wo_ref[c * fc:(c + 1) * fc, :], preferred_element_type=F32)
    o_ref[...] = x + 0.5 * acc


def _ffn(h, g, w_in, w_out, tm, pre=None):
    S, D = h.shape
    Fh = w_out.shape[0]
    in_specs, args = [pl.BlockSpec((tm, D), lambda i: (i, 0))], [h]
    if pre is not None:
        in_specs += [pl.BlockSpec((tm, pre[0].shape[1]), lambda i: (i, 0)), _resident(pre[1].shape)]
        args += list(pre)
    in_specs += [_resident((1, D)),
                 pl.BlockSpec((D, Fh), lambda i: (0, 0), pipeline_mode=pl.Buffered(1)),
                 pl.BlockSpec((D, Fh), lambda i: (0, 1), pipeline_mode=pl.Buffered(1)),
                 _resident((Fh, D))]
    return pl.pallas_call(
        functools.partial(_ffn_kernel, fc=256, pre=pre is not None),
        grid=(S // tm,),
        in_specs=in_specs,
        out_specs=pl.BlockSpec((tm, D), lambda i: (i, 0)),
        out_shape=jax.ShapeDtypeStruct((S, D), F32),
        compiler_params=_cparams(),
        name="ffn",
    )(*args, g.reshape(1, D), w_in, w_in, w_out)


def _lanes(ref, idx, tm):
    return jnp.concatenate([ref[idx]] * (tm // LANE), axis=1)


def _bf16_split3(x):
    a = x.astype(BF16).astype(F32)
    b = (x - a).astype(BF16).astype(F32)
    return a, b, x - a - b


def _heads_kernel(*refs, plan, tm, n_gain, rope, fox, kmean, n_pre):
    it = iter(refs)
    x_ref = next(it)
    pre_a = [next(it) for _ in range(n_pre)]
    pre_w = [next(it) for _ in range(n_pre)]
    g_ref, w_ref = next(it), next(it)
    gn_ref = next(it) if n_gain else None
    cos = sin = None
    if rope:
        cos, sin = next(it)[...], next(it)[...]
    bf_ref = next(it) if fox else None
    n_out = 1 + max(p[6] for p in plan)
    outs = [next(it) for _ in range(n_out)]
    wt_ref = next(it) if fox else None
    km_ref = next(it) if kmean else None
    res_ref = next(it) if n_pre else None
    carry_ref = next(it) if fox else None

    x = x_ref[...]
    if n_pre:
        for a_ref, pw_ref in zip(pre_a, pre_w):
            x = x + jnp.dot(a_ref[...], pw_ref[...], preferred_element_type=F32)
        res_ref[...] = x
    xn = _rms_bf16(x, g_ref[...])
    proj_t = lambda col, width: jnp.dot(xn, w_ref[:, col:col + width], preferred_element_type=F32).T
    zeros_pad = jnp.zeros((KPAD - HEAD_DIM, tm), F32)
    half = ROT_DIM // 2

    fq_rows = fk_rows = None
    if fox:
        @pl.when(pl.program_id(0) == 0)
        def _():
            carry_ref[...] = jnp.zeros(carry_ref.shape, F32)

        gT = proj_t(w_ref.shape[1] - LANE, LANE)
        xg = gT[0:FOX_HEADS] + _lanes(bf_ref, slice(None), tm)
        lf = jnp.minimum(xg, 0.0) - jnp.log1p(jnp.exp(-jnp.abs(xg)))
        tri = jnp.where(lax.broadcasted_iota(jnp.int32, (tm, tm), 0)
                        <= lax.broadcasted_iota(jnp.int32, (tm, tm), 1), 1.0, 0.0).astype(BF16)
        cs = sum(jnp.dot(part.astype(BF16), tri, preferred_element_type=F32) for part in _bf16_split3(lf))
        fcum = (carry_ref[...] + cs) * LOG2E
        carry_ref[...] = carry_ref[...] + jnp.sum(lf, axis=1, keepdims=True)
        wt_ref[...] = gT[FOX_HEADS:FOX_HEADS + IDX_HEADS] * (IDX_HEADS ** -0.5 * IDX_DIM ** -0.5)
        rid = lax.broadcasted_iota(jnp.int32, (SUBLANE, tm), 0)
        fq_rows, fk_rows = [], []
        for h in range(FOX_HEADS):
            f1, f2, f3 = _bf16_split3(fcum[h:h + 1])
            sel = lambda a, b, c, one: jnp.where(rid == 0, a, jnp.where(rid == 1, b, jnp.where(rid == 2, c, one)))
            ones_lo = jnp.where(rid < 6, 1.0, 0.0)
            fq_rows.append(sel(f1, f2, f3, ones_lo))
            fk_rows.append(jnp.where(rid < 3, 1.0, jnp.where(rid == 3, -f1, jnp.where(rid == 4, -f2,
                           jnp.where(rid == 5, -f3, 0.0)))))

    for kind, col, nheads, head0, gain, use_rope, out in plan:
        width = -(-nheads * HEAD_DIM // LANE) * LANE
        segT = proj_t(col, width)
        o_ref = outs[out]
        for hh in range(nheads):
            r = segT[hh * HEAD_DIM:(hh + 1) * HEAD_DIM]
            if gain is not None:
                r = r * lax.rsqrt(jnp.mean(r * r, axis=0, keepdims=True) + RMS_EPS) * _lanes(gn_ref, gain, tm)
            if use_rope:
                x1, x2 = r[:half], r[half:ROT_DIM]
                r = jnp.concatenate([x1 * cos - x2 * sin, x2 * cos + x1 * sin, r[ROT_DIM:]], axis=0)
            h = head0 + hh
            if kind == "v":
                o_ref[h] = jnp.concatenate([r, jnp.ones((VROWS - HEAD_DIM, tm), F32)], axis=0).astype(BF16)
                continue
            if kind == "q":
                r = r * QSCALE
            extra = None
            if fox and out == 0:
                extra = fq_rows[h]
            elif fox and out == 1:
                extra = fk_rows[h]
            if extra is None:
                padded = jnp.concatenate([r, zeros_pad], axis=0)
            else:
                padded = jnp.concatenate([r, extra, zeros_pad[SUBLANE:]], axis=0)
            if kind == "q":
                o_ref[h] = padded.astype(BF16)
            else:
                o_ref[h] = padded.T.astype(BF16)
                if kmean:
                    lane = lax.broadcasted_iota(jnp.int32, (HEAD_DIM, LANE), 1)
                    tile = jnp.zeros((HEAD_DIM, LANE), F32)
                    for b in range(tm // MOBA_BLOCK):
                        mb = jnp.sum(r[:, b * MOBA_BLOCK:(b + 1) * MOBA_BLOCK], axis=1, keepdims=True)
                        tile = jnp.where(lane == b, mb * (1.0 / MOBA_BLOCK), tile)
                    km_ref[0, h] = tile


def _heads_proj(x, g, w, plan, out_heads, tm, gains=None, rope=None, b_forget=None, kmean=False, pre=None):
    S, D = x.shape
    N = w.shape[1]
    fox = b_forget is not None
    pre_a, pre_w = pre if pre is not None else ([], [])
    in_specs = [pl.BlockSpec((tm, D), lambda i: (i, 0))]
    in_specs += [pl.BlockSpec((tm, a.shape[1]), lambda i: (i, 0)) for a in pre_a]
    in_specs += [_resident(pw.shape) for pw in pre_w]
    in_specs += [_resident((1, D)), _resident((D, N))]
    args = [x, *pre_a, *pre_w, g.reshape(1, D), w]
    if gains is not None:
        gt = jnp.broadcast_to(gains[:, :, None], gains.shape + (LANE,)).astype(F32)
        in_specs.append(_resident(gt.shape))
        args.append(gt)
    if rope is not None:
        in_specs += [pl.BlockSpec((ROT_DIM // 2, tm), lambda i: (0, i))] * 2
        args += list(rope)
    if fox:
        in_specs.append(_resident((FOX_HEADS, LANE)))
        args.append(jnp.broadcast_to(b_forget[:, None], (FOX_HEADS, LANE)).astype(F32))
    out_specs, out_shapes = [], []
    for kind, nh in out_heads:
        if kind == "q":
            out_specs.append(pl.BlockSpec((nh, KPAD, tm), lambda i: (0, 0, i)))
            out_shapes.append(jax.ShapeDtypeStruct((nh, KPAD, S), BF16))
        elif kind == "k":
            out_specs.append(pl.BlockSpec((nh, tm, KPAD), lambda i: (0, i, 0)))
            out_shapes.append(jax.ShapeDtypeStruct((nh, S, KPAD), BF16))
        else:
            out_specs.append(pl.BlockSpec((nh, VROWS, tm), lambda i: (0, 0, i)))
            out_shapes.append(jax.ShapeDtypeStruct((nh, VROWS, S), BF16))
    if fox:
        out_specs.append(pl.BlockSpec((IDX_HEADS, tm), lambda i: (0, i)))
        out_shapes.append(jax.ShapeDtypeStruct((IDX_HEADS, S), F32))
    if kmean:
        nhk = out_heads[1][1]
        out_specs.append(pl.BlockSpec((1, nhk, HEAD_DIM, LANE), lambda i: (i, 0, 0, 0)))
        out_shapes.append(jax.ShapeDtypeStruct((S // tm, nhk, HEAD_DIM, LANE), F32))
    if pre_a:
        out_specs.append(pl.BlockSpec((tm, D), lambda i: (i, 0)))
        out_shapes.append(jax.ShapeDtypeStruct((S, D), F32))
    return pl.pallas_call(
        functools.partial(_heads_kernel, plan=tuple(plan), tm=tm, n_gain=0 if gains is None else gains.shape[0],
                          rope=rope is not None, fox=fox, kmean=kmean, n_pre=len(pre_a)),
        grid=(S // tm,),
        in_specs=in_specs,
        out_specs=out_specs,
        out_shape=out_shapes,
        scratch_shapes=[pltpu.VMEM((FOX_HEADS, 1), F32)] if fox else [],
        compiler_params=_cparams(),
        name="heads_proj",
    )(*args)


def _heads_update(units, qT_ref, k_ref, vT_ref, s_ref, p_ref, m_ref, acc_ref, biased):
    _, tk, tqu = s_ref.shape
    rc = min(ATTN_ROWS, tk)
    n = len(units)
    m_new = alpha = None
    for t in range(n + 1):
        mx = None
        if t < n:
            h, q0, dg = units[t]
            qT = qT_ref[h, :, q0:q0 + tqu]
        if t >= 1:
            hb, qb, _ = units[t - 1]
            m_b, alpha_b = m_new, alpha
        for r0 in range(0, tk, rc):
            if t < n:
                s = biased(jnp.dot(k_ref[h, r0:r0 + rc, :], qT, preferred_element_type=F32), h, q0, r0, dg)
                s_ref[t % 2, r0:r0 + rc, :] = s
                cm = jnp.max(s.reshape(rc // SUBLANE, SUBLANE, tqu), axis=0)
                mx = cm if mx is None else jnp.maximum(mx, cm)
            if t >= 1:
                p_ref[(t - 1) % 2, r0:r0 + rc, :] = jnp.exp2(s_ref[(t - 1) % 2, r0:r0 + rc, :] - m_b).astype(BF16)
        if t >= 1:
            pv = jnp.dot(vT_ref[hb], p_ref[(t - 1) % 2], preferred_element_type=F32)
            acc_ref[hb, :, qb:qb + tqu] = alpha_b * acc_ref[hb, :, qb:qb + tqu] + pv
        if t < n:
            m_prev = m_ref[h, :, q0:q0 + tqu]
            m_new = jnp.maximum(m_prev, jnp.max(mx, axis=0, keepdims=True))
            alpha = jnp.exp2(m_prev - m_new)
            m_ref[h, :, q0:q0 + tqu] = m_new


def _attn_init(m_ref, acc_ref):
    m_ref[...] = jnp.full(m_ref.shape, NEG, F32)
    acc_ref[...] = jnp.zeros(acc_ref.shape, F32)


def _attn_finish(o_ref, acc_ref, nh):
    hd = HEAD_DIM
    out = lambda h: acc_ref[h, :hd, :] / acc_ref[h, hd:hd + 1, :]
    for h in range(0, nh, 2):
        o_ref[:, h * hd:(h + 2) * hd] = jnp.concatenate([out(h), out(h + 1)], axis=0).T.astype(o_ref.dtype)


def _attn_kernel(qi_ref, kj_ref, *refs, mode, nh, tq, tk, nsub):
    if mode == "moba":
        qT_ref, k_ref, vT_ref, km_ref, o_ref, m_ref, acc_ref, s_ref, p_ref, cb_ref, sb_ref = refs
    elif mode == "causal":
        qT_ref, k_ref, vT_ref, o_ref, m_ref, acc_ref, s_ref, p_ref, cb_ref = refs
    else:
        qT_ref, k_ref, vT_ref, o_ref, m_ref, acc_ref, s_ref, p_ref = refs
    p = pl.program_id(0)
    i = qi_ref[p]
    j = kj_ref[p]
    causal = mode != "none"
    tqu = tq // nsub

    if causal:
        @pl.when(p == 0)
        def _():
            row = lax.broadcasted_iota(jnp.int32, (tk, tqu), 0)
            col = lax.broadcasted_iota(jnp.int32, (tk, tqu), 1)
            cb_ref[...] = jnp.where(row <= col, 0.0, NEG)

    @pl.when(j == 0)
    def _():
        _attn_init(m_ref, acc_ref)
        if mode == "moba":
            _moba_gate(i, qT_ref, km_ref, sb_ref)

    def biased(s, h, q0, r0, diag):
        if mode == "moba":
            s = s + sb_ref[h, pl.ds(j * (tk // MOBA_BLOCK) + r0 // MOBA_BLOCK, 1), q0:q0 + tqu]
        if diag:
            s = s + cb_ref[r0:r0 + s.shape[0], :]
        return s

    def process(first):
        units = [(h, q * tqu, q == first) for h in range(nh) for q in range(max(first, 0), nsub)]
        _heads_update(units, qT_ref, k_ref, vT_ref, s_ref, p_ref, m_ref, acc_ref, biased)

    if causal:
        d = j - nsub * i
        pl.when(d < 0)(lambda: process(-1))
        for q in range(nsub):
            pl.when(d == q)(functools.partial(process, q))
        last = d == nsub - 1
    else:
        process(-1)
        last = j == 0

    @pl.when(last)
    def _():
        _attn_finish(o_ref, acc_ref, nh)


def _attention(mode, qT, k, vT, extra, tq, tk, nsub):
    nh, _, S = qT.shape
    hd = HEAD_DIM
    nq = S // tq
    if mode == "none":
        pairs = [(i, 0) for i in range(nq)]
    else:
        assert tq == nsub * tk
        pairs = [(i, j) for i in range(nq) for j in range(nsub * (i + 1))]
    qi = jnp.asarray(np.array([a for a, _ in pairs], np.int32))
    kj = jnp.asarray(np.array([b for _, b in pairs], np.int32))
    in_specs = [pl.BlockSpec((nh, KPAD, tq), lambda p, qi, kj: (0, 0, qi[p])),
                pl.BlockSpec((nh, tk, KPAD), lambda p, qi, kj: (0, kj[p], 0)),
                pl.BlockSpec((nh, VROWS, tk), lambda p, qi, kj: (0, 0, kj[p]))]
    if mode == "moba":
        in_specs += [_resident(extra[0].shape)]
    tqu = tq // nsub
    scratch = [pltpu.VMEM((nh, 1, tq), F32), pltpu.VMEM((nh, VROWS, tq), F32),
               pltpu.VMEM((2, tk, tqu), F32), pltpu.VMEM((2, tk, tqu), BF16)]
    if mode != "none":
        scratch.append(pltpu.VMEM((tk, tqu), F32))
    if mode == "moba":
        scratch.append(pltpu.VMEM((nh, extra[0].shape[1], tq), F32))
    return pl.pallas_call(
        functools.partial(_attn_kernel, mode=mode, nh=nh, tq=tq, tk=tk, nsub=nsub),
        grid_spec=pltpu.PrefetchScalarGridSpec(
            num_scalar_prefetch=2,
            grid=(len(pairs),),
            in_specs=in_specs,
            out_specs=pl.BlockSpec((tq, nh * hd), lambda p, qi, kj: (qi[p], 0)),
            scratch_shapes=scratch),
        out_shape=jax.ShapeDtypeStruct((S, nh * hd), BF16),
        compiler_params=_cparams(),
        name="attn_" + mode,
    )(qi, kj, qT, k, vT, *extra)


def _moba_gate(i, qT_ref, km_ref, sb_ref):
    nh, nb, tq = sb_ref.shape
    n = lax.broadcasted_iota(jnp.int32, (nb, tq), 0)
    own = (i * tq + lax.broadcasted_iota(jnp.int32, (nb, tq), 1)) // MOBA_BLOCK
    ninf = jnp.float32(-jnp.inf)

    def body(h, carry):
        g = jnp.dot(km_ref[h], qT_ref[h], preferred_element_type=F32)
        g = jnp.where(n < own, g, ninf)
        keep = n == own
        for _ in range(MOBA_TOPK):
            mx = jnp.max(g, axis=0, keepdims=True)
            first = jnp.min(jnp.where((g == mx) & (mx > ninf), n, nb), axis=0, keepdims=True)
            pick = n == first
            keep = keep | pick
            g = jnp.where(pick, ninf, g)
        sb_ref[h] = jnp.where(keep, 0.0, NEG)
        return carry

    lax.fori_loop(0, nh, body, 0, unroll=4)


def _dsa_kernel(qi_ref, kj_ref, ph_ref, ikb_ref, kvb_ref, nk_ref,
                iqT_ref, ik_ref, w_ref, qT_ref, k_ref, vT_ref, o_ref,
                sc_ref, thr_ref, cut_ref, m_ref, acc_ref, s_ref, p_ref, bias_ref,
                *, nih, nh, tq, tk, topk, seq, chunk):
    p = pl.program_id(0)
    i = qi_ref[p]
    j = kj_ref[p]
    phase = ph_ref[p]
    nkb = nk_ref[p]
    I32 = jnp.int32
    lowest = float(jnp.finfo(jnp.float32).min)

    def as_float(code):
        return lax.bitcast_convert_type(code ^ ((code >> 31) & 0x7FFFFFFF), F32)

    @pl.when(phase == 0)
    def _scores():
        ik = ik_ref[...]
        sc = jnp.zeros((tk, tq), F32)
        for h in range(nih):
            d = jnp.dot(ik, iqT_ref[h], preferred_element_type=F32)
            sc = sc + w_ref[h:h + 1, :] * jnp.maximum(d, 0.0)
        row = j * tk + lax.broadcasted_iota(I32, (tk, tq), 0)
        col = i * tq + lax.broadcasted_iota(I32, (tk, tq), 1)
        sc_ref[pl.ds(pl.multiple_of(j * tk, tk), tk), :] = jnp.where(row <= col, sc, -jnp.inf)

    def count(hit):
        def body(c, acc):
            r0 = pl.multiple_of(c * chunk, chunk)
            return acc + jnp.sum(hit(sc_ref[pl.ds(r0, chunk), :], r0).reshape(chunk // SUBLANE, SUBLANE, tq), axis=0)
        acc = lax.fori_loop(0, nkb * (tk // chunk), body, jnp.zeros((SUBLANE, tq), I32))
        return jnp.sum(acc, axis=0, keepdims=True)

    @pl.when((phase == 1) & (j == 0))
    def _select():
        c0 = count(lambda blk, r0: jnp.where(blk >= 0.0, 1, 0))
        code = jnp.where(c0 >= topk, 0, INT_MIN)

        def bit_body(it, code):
            cand = code | jnp.left_shift(jnp.int32(1), 30 - it)
            cand_f = as_float(cand)
            cnt = count(lambda blk, r0: jnp.where(blk >= cand_f, 1, 0))
            return jnp.where(cnt >= topk, cand, code)

        code = lax.fori_loop(0, 31, bit_body, code)
        thr = jnp.where(code == INT_MIN, lowest, as_float(code))
        thr_ref[...] = thr
        cut_ref[...] = jnp.full((1, tq), seq, I32)
        cge = count(lambda blk, r0: jnp.where(blk >= thr, 1, 0))

        @pl.when(jnp.max(cge) > topk)
        def _ties():
            need = topk - count(lambda blk, r0: jnp.where(blk > thr, 1, 0))
            nbits = max(1, int(seq - 1).bit_length())

            def pos_body(it, x):
                cand = x + jnp.left_shift(jnp.int32(1), nbits - 1 - it)
                pos = lambda r0: r0 + lax.broadcasted_iota(I32, (chunk, tq), 0)
                c = count(lambda blk, r0: jnp.where(blk == thr, jnp.where(pos(r0) < cand, 1, 0), 0))
                return jnp.where(c < need, cand, x)

            x = lax.fori_loop(0, nbits, pos_body, jnp.zeros((1, tq), I32))
            cut_ref[...] = jnp.where(cge > topk, x, seq)

        _attn_init(m_ref, acc_ref)

    @pl.when(phase == 1)
    def _attend():
        sc = sc_ref[pl.ds(pl.multiple_of(j * tk, tk), tk), :]
        thr = thr_ref[...]
        pos = j * tk + lax.broadcasted_iota(I32, (tk, tq), 0)
        tie = jnp.where(sc == thr, jnp.where(pos <= cut_ref[...], 0.0, NEG), NEG)
        bias_ref[...] = jnp.where(sc > thr, 0.0, tie)

        def biased(s, h, q0, r0, diag):
            return s + bias_ref[r0:r0 + s.shape[0], :]

        units = [(h, 0, False) for h in range(nh)]
        _heads_update(units, qT_ref, k_ref, vT_ref, s_ref, p_ref, m_ref, acc_ref, biased)

        @pl.when(j == nkb - 1)
        def _():
            _attn_finish(o_ref, acc_ref, nh)


def _dsa_attention(iqT, ik, wT, qT, k, vT, tq, tk, topk):
    nh, _, S = qT.shape
    hd = HEAD_DIM
    nih = iqT.shape[0]
    steps = []
    for i in range(S // tq):
        nkb = (i * tq + tq - 1) // tk + 1
        steps += [(i, j, 0, j, 0, nkb) for j in range(nkb)]
        steps += [(i, j, 1, nkb - 1, j, nkb) for j in range(nkb)]
    tab = [jnp.asarray(np.array([s[c] for s in steps], np.int32)) for c in range(6)]
    chunk = min(tk, 512)
    qmap = lambda p, qi, kj, ph, ikb, kvb, nk: (0, 0, qi[p])
    return pl.pallas_call(
        functools.partial(_dsa_kernel, nih=nih, nh=nh, tq=tq, tk=tk, topk=topk, seq=S, chunk=chunk),
        grid_spec=pltpu.PrefetchScalarGridSpec(
            num_scalar_prefetch=6,
            grid=(len(steps),),
            in_specs=[pl.BlockSpec((nih, KPAD, tq), qmap),
                      pl.BlockSpec((tk, KPAD), lambda p, qi, kj, ph, ikb, kvb, nk: (ikb[p], 0)),
                      pl.BlockSpec((nih, tq), lambda p, qi, kj, ph, ikb, kvb, nk: (0, qi[p])),
                      pl.BlockSpec((nh, KPAD, tq), qmap),
                      pl.BlockSpec((nh, tk, KPAD), lambda p, qi, kj, ph, ikb, kvb, nk: (0, kvb[p], 0)),
                      pl.BlockSpec((nh, VROWS, tk), lambda p, qi, kj, ph, ikb, kvb, nk: (0, 0, kvb[p]))],
            out_specs=pl.BlockSpec((tq, nh * hd), lambda p, qi, kj, ph, ikb, kvb, nk: (qi[p], 0)),
            scratch_shapes=[pltpu.VMEM((S, tq), F32),
                            pltpu.VMEM((1, tq), F32), pltpu.VMEM((1, tq), jnp.int32),
                            pltpu.VMEM((nh, 1, tq), F32), pltpu.VMEM((nh, VROWS, tq), F32),
                            pltpu.VMEM((2, tk, tq), F32), pltpu.VMEM((2, tk, tq), BF16),
                            pltpu.VMEM((tk, tq), F32)]),
        out_shape=jax.ShapeDtypeStruct((S, nh * hd), BF16),
        compiler_params=_cparams(),
        name="dsa",
    )(*tab, iqT, ik, wT, qT, k, vT)


def kernel(x, mem, ffn1_norm, ffn1_w_in, ffn1_w_out, mix_norm, even_w_in, even_b_forget, fox_q_norm, fox_k_norm, dsa_q_norm, dsa_k_norm, odd_w_in, moba_q_norm, moba_k_norm, mix_w_out, mem_norm_x, mem_norm_m, mem_wq, mem_wkv, mem_q_norm, mem_k_norm, mem_wo, ffn2_norm, ffn2_w_in, ffn2_w_out):
    _, S, D = x.shape
    M = mem.shape[1]
    depth = ffn1_norm.shape[0]
    tm = min(512, S)
    ta = min(512, S)
    nsub = 2 if S % (2 * ta) == 0 else 1
    tdq = min(512, S)
    pos = jnp.arange(S, dtype=F32)
    inv = ROPE_THETA ** (-jnp.arange(0, ROT_DIM, 2, dtype=F32) / ROT_DIM)
    ang = inv[:, None] * pos[None, :]
    rope = (jnp.cos(ang), jnp.sin(ang))
    W8 = 8 * HEAD_DIM
    zcols = lambda n: jnp.zeros((D, n), F32)

    h = x[0]
    m0 = mem[0]
    for layer in range(depth):
        h = _ffn(h, ffn1_norm[layer], ffn1_w_in[layer].astype(BF16), ffn1_w_out[layer].astype(BF16), tm)
        if layer % 2 == 0:
            e = layer // 2
            o = np.cumsum([0, W8, W8, W8, FOX_HEADS, W8, W8, W8, IDX_HEADS * IDX_DIM, IDX_DIM, IDX_HEADS])
            w = even_w_in[e]
            seg = lambda n: w[:, o[n]:o[n + 1]]
            w = jnp.concatenate([seg(0), seg(1), seg(2), seg(4), seg(5), seg(6), seg(7),
                                 seg(8), zcols(KPAD - IDX_DIM),
                                 seg(3), seg(9), zcols(LANE - FOX_HEADS - IDX_HEADS)], axis=1).astype(BF16)
            plan = [("q", 0, 8, 0, 0, False, 0), ("k", W8, 8, 0, 1, False, 1), ("v", 2 * W8, 8, 0, None, False, 2),
                    ("q", 3 * W8, 8, 0, 2, True, 3), ("k", 4 * W8, 8, 0, 3, True, 4), ("v", 5 * W8, 8, 0, None, False, 5),
                    ("q", 6 * W8, 8, 0, None, True, 6), ("k", 7 * W8, 1, 0, None, True, 7)]
            gains = jnp.stack([fox_q_norm[e], fox_k_norm[e], dsa_q_norm[e], dsa_k_norm[e]])
            fqT, fk, fvT, dqT, dk, dvT, iqT, ik, wT = _heads_proj(
                h, mix_norm[layer], w, plan,
                [("q", 8), ("k", 8), ("v", 8), ("q", 8), ("k", 8), ("v", 8), ("q", 8), ("k", 1)],
                tm, gains=gains, rope=rope, b_forget=even_b_forget[e])
            fox = _attention("causal", fqT, fk, fvT, (), nsub * ta, ta, nsub)
            dsa = _dsa_attention(iqT, ik[0], wT, dqT, dk, dvT, tdq, ta, min(DSA_TOPK, S // 4))
            wo = mix_w_out[layer].astype(BF16)
            mixed = ([fox, dsa], [wo[:W8], wo[W8:]])
        else:
            od = layer // 2
            w = odd_w_in[od].astype(BF16)
            plan = [("q", 0, 8, 0, 0, True, 0), ("q", W8, 8, 8, 0, True, 0),
                    ("k", 2 * W8, 8, 0, 1, True, 1), ("k", 3 * W8, 8, 8, 1, True, 1),
                    ("v", 4 * W8, 8, 0, None, False, 2), ("v", 5 * W8, 8, 8, None, False, 2)]
            gains = jnp.stack([moba_q_norm[od], moba_k_norm[od]])
            qT, k, vT, km = _heads_proj(h, mix_norm[layer], w, plan, [("q", 16), ("k", 16), ("v", 16)],
                                        tm, gains=gains, rope=rope, kmean=True)
            nbt = tm // MOBA_BLOCK
            km = jnp.transpose(km[..., :nbt], (1, 0, 3, 2)).reshape(MOBA_HEADS, S // MOBA_BLOCK, HEAD_DIM)
            km = jnp.pad(km, ((0, 0), (0, 0), (0, KPAD - HEAD_DIM))).astype(BF16)
            moba = _attention("moba", qT, k, vT, (km,), nsub * ta, ta, nsub)
            mixed = ([moba], [mix_w_out[layer].astype(BF16)])
        mw = MEM_HEADS * HEAD_DIM
        mqT, h = _heads_proj(h, mem_norm_x[layer], mem_wq[layer].astype(BF16),
                             [("q", 0, MEM_HEADS, 0, 0, False, 0)], [("q", MEM_HEADS)], tm,
                             gains=mem_q_norm[layer][None], pre=mixed)
        mk, mvT = _heads_proj(m0, mem_norm_m[layer], mem_wkv[layer].astype(BF16),
                              [("k", 0, MEM_HEADS, 0, 0, False, 0), ("v", mw, MEM_HEADS, 0, None, False, 1)],
                              [("k", MEM_HEADS), ("v", MEM_HEADS)], M, gains=mem_k_norm[layer][None])
        ma = _attention("none", mqT, mk, mvT, (), nsub * ta, M, nsub)
        h = _ffn(h, ffn2_norm[layer], ffn2_w_in[layer].astype(BF16), ffn2_w_out[layer].astype(BF16), tm,
                 pre=(ma, mem_wo[layer].astype(BF16)))
    return h[None]
```
